```python
import math
import jax, jax.numpy as jnp
from jax import lax
import numpy as np

D_MODEL = 1024
BATCH = 32
SEQ = 256
DEPTH = 2
DEC_BATCH = 8
DEC_SEQ = 2048
PAST_LEN = 256

GRID_W = 64
Q_BLOCK = 128
EPS = 1e-6
ROPE_BASE = 10000.0
A_HEADS = 8
A_KV = 2
A_GROUP = A_HEADS // A_KV
HEAD_DIM = 64
M_HEADS = 4
M_Q_RANK = 256
M_KV_RANK = 128
M_NOPE = 64
M_ROPE = 32
M_V = 64
C_GROUPS = 4
C_DIM = 64
C_CHUNK = 128
A_Q_W = A_HEADS * HEAD_DIM
A_KV_W = A_KV * HEAD_DIM
C_W = C_GROUPS * C_DIM
IN_SIZES = [A_Q_W, A_KV_W, A_KV_W, M_Q_RANK, M_KV_RANK, M_ROPE, C_W, C_W]
IN_SPLITS = [int(v) for v in np.cumsum(IN_SIZES)[:-1]]
IN_W = int(sum(IN_SIZES))
MIX_W = A_HEADS * HEAD_DIM + M_HEADS * M_V + C_W
PEER_HEADS = 8
N_KEYS = 128
N_EXPERTS = N_KEYS * N_KEYS
PEER_DK = 256
PEER_HALF = PEER_DK // 2
PEER_TOPK = 16
ALPHA = (2.0 * DEPTH) ** 0.25
BETA = (8.0 * DEPTH) ** -0.25

kernel_name = "hybrid_diffusion_gqa_mla_gmlp_peer_step"


def _ln(x, g=None, b=None):
    xf = x.astype(jnp.float32)
    mu = jnp.mean(xf, axis=-1, keepdims=True)
    var = jnp.mean(jnp.square(xf - mu), axis=-1, keepdims=True)
    y = (xf - mu) * lax.rsqrt(var + EPS)
    if g is not None:
        y = y * g.astype(jnp.float32) + b.astype(jnp.float32)
    return y.astype(x.dtype)


def _rms(x, g):
    xf = x.astype(jnp.float32)
    y = xf * lax.rsqrt(jnp.mean(jnp.square(xf), axis=-1, keepdims=True) + EPS) * g.astype(jnp.float32)
    return y.astype(x.dtype)


def _rot(x, ang):
    m = ang.shape[-1]
    c = jnp.cos(ang)[None, :, None, :]
    s = jnp.sin(ang)[None, :, None, :]
    x1, x2 = x[..., :m], x[..., m:]
    return jnp.concatenate([x1 * c - x2 * s, x1 * s + x2 * c], axis=-1)


def axial_rope(x):
    S, d = x.shape[1], x.shape[-1]
    rows = S // GRID_W
    m = d // 4
    row_idx = jnp.repeat(jnp.arange(rows), GRID_W).astype(jnp.float32)
    col_idx = jnp.tile(jnp.arange(GRID_W), rows).astype(jnp.float32)
    freqs = ROPE_BASE ** (-jnp.arange(m, dtype=jnp.float32) / m)
    xf = x.astype(jnp.float32)
    out = jnp.concatenate([_rot(xf[..., :2 * m], row_idx[:, None] * freqs),
                           _rot(xf[..., 2 * m:], col_idx[:, None] * freqs)], axis=-1)
    return out.astype(x.dtype)


def block_attention(q, k, v):
    B, Sq, Hk, G, dk = q.shape
    dv = v.shape[-1]
    nb = Sq // Q_BLOCK
    qb = jnp.moveaxis(q.reshape(B, nb, Q_BLOCK, Hk, G, dk), 1, 0)
    scale = 1.0 / math.sqrt(dk)

    def one(qblk):
        s = jnp.einsum('bqhgd,bkhd->bhgqk', qblk, k).astype(jnp.float32) * scale
        p = jax.nn.softmax(s, axis=-1).astype(v.dtype)
        return jnp.einsum('bhgqk,bkhe->bqhge', p, v)

    o = lax.map(one, qb)
    return jnp.moveaxis(o, 0, 1).reshape(B, Sq, Hk * G * dv)


def chunk_gmlp(u, v, w_s, b_s):
    B, S, _ = u.shape
    vg = _ln(v.reshape(B, S // C_CHUNK, C_CHUNK, C_GROUPS, C_DIM))
    mixed = jnp.einsum('gpq,bnqgd->bnpgd', w_s, vg) + b_s.T[None, None, :, :, None]
    return u * mixed.reshape(B, S, C_GROUPS * C_DIM)


def peer(h, w_q, k1, k2, u_tab, v_tab):
    B, S, D = h.shape
    xt = h.reshape(-1, Q_BLOCK, D)

    def one(xb):
        q = (xb @ w_q).reshape(Q_BLOCK, PEER_HEADS, 2, PEER_HALF)
        s1 = jnp.einsum('thd,nd->thn', q[:, :, 0], k1)
        s2 = jnp.einsum('thd,nd->thn', q[:, :, 1], k2)
        v1, i1 = lax.top_k(s1, PEER_TOPK)
        v2, i2 = lax.top_k(s2, PEER_TOPK)
        cand = (v1[..., :, None] + v2[..., None, :]).reshape(Q_BLOCK, PEER_HEADS, PEER_TOPK * PEER_TOPK)
        cidx = (i1[..., :, None] * N_KEYS + i2[..., None, :]).reshape(Q_BLOCK, PEER_HEADS, PEER_TOPK * PEER_TOPK)
        top, pos = lax.top_k(cand, PEER_TOPK)
        idx = jnp.take_along_axis(cidx, pos, axis=-1)
        g = jax.nn.softmax(top.astype(jnp.float32), axis=-1).astype(xb.dtype)
        a = jax.nn.gelu(jnp.einsum('thkd,td->thk', u_tab[idx], xb))
        return jnp.einsum('thk,thkd->td', g * a, v_tab[idx])

    return lax.map(one, xt).reshape(B, S, D)


def _layer(x, mod, P, cache):
    B, S, _ = x.shape
    sh1, sc1, g1, sh2, sc2, g2 = jnp.split(mod[:, None, :], 6, axis=-1)
    h = _ln(x) * (1 + sc1) + sh1
    q_a, k_a, v_a, cq, ckv, krope, u_c, v_c = jnp.split(h @ P['w_in'], IN_SPLITS, axis=-1)
    q_a = _rms(q_a.reshape(B, S, A_HEADS, HEAD_DIM), P['aqn'])
    k_a = _rms(k_a.reshape(B, S, A_KV, HEAD_DIM), P['akn'])
    v_a = v_a.reshape(B, S, A_KV, HEAD_DIM)
    cq = _rms(cq, P['mqn'])
    ckv = _rms(ckv, P['mkvn'])
    qm = (cq @ P['w_uq']).reshape(B, S, M_HEADS, M_NOPE + M_ROPE)
    qm_nope, qm_rope = qm[..., :M_NOPE], qm[..., M_NOPE:]
    if cache is None:
        new = (k_a, v_a, ckv, krope)
    else:
        new = None
        c_k, c_v, c_ckv, c_krope = cache
        q_a = axial_rope(q_a)
        qm_rope = axial_rope(qm_rope)
        k_a = jnp.concatenate([c_k, axial_rope(k_a)], axis=1)
        v_a = jnp.concatenate([c_v, v_a], axis=1)
        ckv = jnp.concatenate([c_ckv, ckv], axis=1)
        krope = jnp.concatenate([c_krope, axial_rope(krope[:, :, None, :])[:, :, 0]], axis=1)
    L = ckv.shape[1]
    kv = (ckv @ P['w_ukv']).reshape(B, L, M_HEADS, M_NOPE + M_V)
    k_m = jnp.concatenate([kv[..., :M_NOPE],
                           jnp.broadcast_to(krope[:, :, None, :], (B, L, M_HEADS, M_ROPE))], axis=-1)
    v_m = kv[..., M_NOPE:]
    q_m = jnp.concatenate([qm_nope, qm_rope], axis=-1)
    o_a = block_attention(q_a.reshape(B, S, A_KV, A_GROUP, HEAD_DIM), k_a, v_a)
    o_m = block_attention(q_m[:, :, :, None, :], k_m, v_m)
    o_c = chunk_gmlp(u_c, v_c, P['ws'], P['bs'])
    mix = jnp.concatenate([o_a, o_m, o_c], axis=-1) @ P['w_o']
    x = _ln(ALPHA * x + g1 * mix, P['ln1_g'], P['ln1_b'])
    h2 = _ln(x) * (1 + sc2) + sh2
    ff = peer(h2, P['pwq'], P['pk1'], P['pk2'], P['pu'], P['pv'])
    x = _ln(ALPHA * x + g2 * ff, P['ln2_g'], P['ln2_b'])
    return x, new


def setup_inputs(seed: int = 0) -> dict:
    key = jax.random.key(seed)
    ks = jax.random.split(key, 32)
    f32 = jnp.float32

    def nrm(k, shape, s):
        return jax.random.normal(k, shape, f32) * s

    D = D_MODEL
    return {
        "x_prompt": nrm(ks[0], (BATCH, SEQ, D), 1.0),
        "x_sample": nrm(ks[1], (DEC_BATCH, DEC_SEQ, D), 1.0),
        "cache_attn_k": nrm(ks[2], (DEC_BATCH, DEPTH, PAST_LEN, A_KV, HEAD_DIM), 1.0),
        "cache_attn_v": nrm(ks[3], (DEC_BATCH, DEPTH, PAST_LEN, A_KV, HEAD_DIM), 1.0),
        "cache_mla_ckv": nrm(ks[4], (DEC_BATCH, DEPTH, PAST_LEN, M_KV_RANK), 1.0),
        "cache_mla_krope": nrm(ks[5], (DEC_BATCH, DEPTH, PAST_LEN, M_ROPE), 1.0),
        "c": nrm(ks[6], (DEC_BATCH, D), 1.0),
        "c_ctx": nrm(ks[7], (D,), 1.0),
        "w_mod": nrm(ks[8], (DEPTH, D, 6 * D), 0.5 * D ** -0.5),
        "b_mod": nrm(ks[9], (DEPTH, 6 * D), 0.02),
        "w_in": nrm(ks[10], (DEPTH, D, IN_W), D ** -0.5),
        "attn_q_norm": 1.0 + nrm(ks[11], (DEPTH, HEAD_DIM), 0.02),
        "attn_k_norm": 1.0 + nrm(ks[12], (DEPTH, HEAD_DIM), 0.02),
        "mla_q_norm": 1.0 + nrm(ks[13], (DEPTH, M_Q_RANK), 0.02),
        "mla_kv_norm": 1.0 + nrm(ks[14], (DEPTH, M_KV_RANK), 0.02),
        "w_uq": nrm(ks[15], (DEPTH, M_Q_RANK, M_HEADS * (M_NOPE + M_ROPE)), M_Q_RANK ** -0.5),
        "w_ukv": nrm(ks[16], (DEPTH, M_KV_RANK, M_HEADS * (M_NOPE + M_V)), M_KV_RANK ** -0.5),
        "gmlp_ws": nrm(ks[17], (DEPTH, C_GROUPS, C_CHUNK, C_CHUNK), C_CHUNK ** -0.5),
        "gmlp_b": 1.0 + nrm(ks[18], (DEPTH, C_GROUPS, C_CHUNK), 0.02),
        "w_o": nrm(ks[19], (DEPTH, MIX_W, D), BETA * MIX_W ** -0.5),
        "ln1_g": 1.0 + nrm(ks[20], (DEPTH, D), 0.02),
        "ln1_b": nrm(ks[21], (DEPTH, D), 0.02),
        "ln2_g": 1.0 + nrm(ks[22], (DEPTH, D), 0.02),
        "ln2_b": nrm(ks[23], (DEPTH, D), 0.02),
        "peer_wq": nrm(ks[24], (DEPTH, D, PEER_HEADS * PEER_DK), D ** -0.5),
        "peer_k1": nrm(ks[25], (DEPTH, N_KEYS, PEER_HALF), PEER_HALF ** -0.5),
        "peer_k2": nrm(ks[26], (DEPTH, N_KEYS, PEER_HALF), PEER_HALF ** -0.5),
        "peer_u": nrm(ks[27], (DEPTH, N_EXPERTS, D), D ** -0.5),
        "peer_v": nrm(ks[28], (DEPTH, N_EXPERTS, D), BETA * PEER_HEADS ** -0.5),
    }


def reference(x_prompt, x_sample, cache_attn_k, cache_attn_v, cache_mla_ckv, cache_mla_krope, c, c_ctx,
              w_mod, b_mod, w_in, attn_q_norm, attn_k_norm, mla_q_norm, mla_kv_norm, w_uq, w_ukv,
              gmlp_ws, gmlp_b, w_o, ln1_g, ln1_b, ln2_g, ln2_b, peer_wq, peer_k1, peer_k2, peer_u, peer_v):
    xp, xs = x_prompt, x_sample
    st_k, st_v, st_ckv, st_kr = [], [], [], []
    for l in range(DEPTH):
        P = {'w_in': w_in[l], 'aqn': attn_q_norm[l], 'akn': attn_k_norm[l], 'mqn': mla_q_norm[l],
             'mkvn': mla_kv_norm[l], 'w_uq': w_uq[l], 'w_ukv': w_ukv[l], 'ws': gmlp_ws[l], 'bs': gmlp_b[l],
             'w_o': w_o[l], 'ln1_g': ln1_g[l], 'ln1_b': ln1_b[l], 'ln2_g': ln2_g[l], 'ln2_b': ln2_b[l],
             'pwq': peer_wq[l], 'pk1': peer_k1[l], 'pk2': peer_k2[l], 'pu': peer_u[l], 'pv': peer_v[l]}
        mod_ctx = (jax.nn.silu(c_ctx) @ w_mod[l] + b_mod[l])[None]
        mod_lat = jax.nn.silu(c) @ w_mod[l] + b_mod[l]
        xp, (k_l, v_l, ckv_l, kr_l) = _layer(xp, mod_ctx, P, None)
        st_k.append(k_l)
        st_v.append(v_l)
        st_ckv.append(ckv_l)
        st_kr.append(kr_l)
        cache_l = (cache_attn_k[:, l], cache_attn_v[:, l], cache_mla_ckv[:, l], cache_mla_krope[:, l])
        xs, _ = _layer(xs, mod_lat, P, cache_l)
    new_attn_k = jnp.stack(st_k, axis=1)
    new_attn_v = jnp.stack(st_v, axis=1)
    new_mla_ckv = jnp.stack(st_ckv, axis=1)
    new_mla_krope = jnp.stack(st_kr, axis=1)
    return (xp, xs, new_attn_k, new_attn_v, new_mla_ckv, new_mla_krope)
```

```python
import functools
import math

import numpy as np
import jax
import jax.numpy as jnp
from jax import lax
from jax.experimental import pallas as pl
from jax.experimental.pallas import tpu as pltpu

A_HEADS = 8
A_KV = 2
HEAD_DIM = 64
M_HEADS = 4
M_Q_RANK = 256
M_KV_RANK = 128
M_NOPE = 64
M_ROPE = 32
M_V = 64
C_GROUPS = 4
C_DIM = 64
C_CHUNK = 128
PEER_HEADS = 8
N_KEYS = 128
PEER_HALF = 128
PEER_TOPK = 16
GRID_W = 64
EPS = 1e-6
ROPE_BASE = 10000.0

LANES = 128
NEG_BIG = -3.0e38

QA_W = A_HEADS * HEAD_DIM
KA_W = A_KV * HEAD_DIM
CQ_OFF = QA_W + 2 * KA_W
CKV_OFF = CQ_OFF + M_Q_RANK
KR_OFF = CKV_OFF + M_KV_RANK
UC_OFF = KR_OFF + M_HEADS * M_ROPE
CW = C_GROUPS * C_DIM
VC_OFF = UC_OFF + CW
IN_WP = VC_OFF + CW

PRE_TB = 256
ATT_QB = 256
PEER_T = 512
PEER_EB = 1024
VMEM_LIMIT = 56 * 1024 * 1024

_bf16 = jnp.bfloat16
_f32 = jnp.float32


def _dot(a, b):
    return jnp.dot(a, b, preferred_element_type=_f32)


def _dot_nt(a, b):
    return lax.dot_general(a, b, (((1,), (1,)), ((), ())), preferred_element_type=_f32)


def _split(x):
    hi = x.astype(_bf16)
    lo = (x - hi.astype(_f32)).astype(_bf16)
    return hi, lo


def _segsum(x, ones_bd):
    hi, lo = _split(x)
    return _dot(hi, ones_bd) + _dot(lo, ones_bd)


def _ln_rows(x):
    mu = jnp.mean(x, axis=-1, keepdims=True)
    d = x - mu
    var = jnp.mean(d * d, axis=-1, keepdims=True)
    return d * lax.rsqrt(var + EPS)


def _rot(x, cos, sin, half):
    lane = lax.broadcasted_iota(jnp.int32, x.shape, 1)
    up = pltpu.roll(x, LANES - half, axis=1)
    dn = pltpu.roll(x, half, axis=1)
    partner = jnp.where((lane & half) == 0, up, dn)
    return x * cos + partner * sin


def _mod_kernel(c_ref, w_ref, b_ref, o_ref):
    c = c_ref[...]
    a = c * (1.0 / (1.0 + jnp.exp(-c)))
    a_hi, a_lo = _split(a)
    w_hi, w_lo = _split(w_ref[0])
    o_ref[0] = _dot(a_hi, w_hi) + _dot(a_lo, w_hi) + _dot(a_hi, w_lo) + b_ref[0]


def _mod_call(c_all, w_mod, b_mod):
    depth, d, d6 = w_mod.shape
    rows = c_all.shape[0]
    tn = 1024
    return pl.pallas_call(
        _mod_kernel,
        grid=(depth, d6 // tn),
        in_specs=[pl.BlockSpec((rows, d), lambda l, j: (0, 0)),
                  pl.BlockSpec((1, d, tn), lambda l, j: (l, 0, j)),
                  pl.BlockSpec((1, 1, tn), lambda l, j: (l, 0, j))],
        out_specs=pl.BlockSpec((1, rows, tn), lambda l, j: (l, 0, j)),
        out_shape=jax.ShapeDtypeStruct((depth, rows, d6), _f32),
        compiler_params=pltpu.CompilerParams(dimension_semantics=("parallel", "parallel"),
                                             vmem_limit_bytes=VMEM_LIMIT),
        name="mod",
    )(c_all, w_mod, b_mod.reshape(depth, 1, d6))


def _pre_kernel(*refs, latent, d_model):
    if latent:
        (x_ref, mod_ref, win_ref, aqn_ref, akn_ref, mqn_ref, mkvn_ref, wuq_ref, wukv_ref, ws_ref, bsf_ref,
         bd_ref, cosa_ref, sina_ref, cosm_ref, sinm_ref,
         qa_ref, ka_ref, va_ref, qm_ref, kvm_ref, kr_ref, oc_ref) = refs
    else:
        (x_ref, mod_ref, win_ref, aqn_ref, akn_ref, mqn_ref, mkvn_ref, wuq_ref, wukv_ref, ws_ref, bsf_ref,
         bd_ref,
         qa_ref, ka_ref, va_ref, qm_ref, kvm_ref, kr_ref, oc_ref,
         nk_ref, nv_ref, nckv_ref, nkr_ref) = refs
    D = d_model
    x = x_ref[...]
    mod = mod_ref[0]
    sh1 = mod[:, 0:D]
    sc1 = mod[:, D:2 * D]
    h = _ln_rows(x) * (1.0 + sc1) + sh1
    proj = _dot(h.astype(_bf16), win_ref[...])

    qa = proj[:, 0:QA_W]
    ms = _segsum(qa * qa, bd_ref[...]) * (1.0 / HEAD_DIM)
    qa = qa * lax.rsqrt(ms + EPS) * aqn_ref[...]
    ka = proj[:, QA_W:QA_W + KA_W]
    ms = _segsum(ka * ka, bd_ref[0:KA_W, 0:KA_W]) * (1.0 / HEAD_DIM)
    ka = ka * lax.rsqrt(ms + EPS) * akn_ref[...]
    va = proj[:, QA_W + KA_W:CQ_OFF]
    if not latent:
        nk_ref[...] = ka
        nv_ref[...] = va
    for j in range(QA_W // LANES):
        slab = qa[:, j * LANES:(j + 1) * LANES]
        if latent:
            slab = _rot(slab, cosa_ref[...], sina_ref[...], HEAD_DIM // 4)
        qa_ref[:, j * LANES:(j + 1) * LANES] = (slab * (1.0 / math.sqrt(HEAD_DIM))).astype(_bf16)
    if latent:
        ka = _rot(ka, cosa_ref[...], sina_ref[...], HEAD_DIM // 4)
    ka_ref[...] = ka.astype(_bf16)
    va_ref[...] = va.astype(_bf16)

    cq = proj[:, CQ_OFF:CKV_OFF]
    cq = cq * lax.rsqrt(jnp.mean(cq * cq, axis=-1, keepdims=True) + EPS) * mqn_ref[...]
    qm = _dot(cq.astype(_bf16), wuq_ref[...])
    nope_w = M_HEADS * M_NOPE
    qscale = 1.0 / math.sqrt(M_NOPE + M_ROPE)
    qm_ref[:, 0:nope_w] = (qm[:, 0:nope_w] * qscale).astype(_bf16)
    qr = qm[:, nope_w:]
    if latent:
        qr = _rot(qr, cosm_ref[...], sinm_ref[...], M_ROPE // 4)
    qm_ref[:, nope_w:] = (qr * qscale).astype(_bf16)
    ckv = proj[:, CKV_OFF:KR_OFF]
    ckv = ckv * lax.rsqrt(jnp.mean(ckv * ckv, axis=-1, keepdims=True) + EPS) * mkvn_ref[...]
    kvm_ref[...] = _dot(ckv.astype(_bf16), wukv_ref[...]).astype(_bf16)
    kr = proj[:, KR_OFF:UC_OFF]
    if not latent:
        nckv_ref[...] = ckv
        nkr_ref[...] = kr[:, 0:M_ROPE]
    else:
        kr = _rot(kr, cosm_ref[...], sinm_ref[...], M_ROPE // 4)
    kr_ref[...] = kr.astype(_bf16)

    uc = proj[:, UC_OFF:VC_OFF]
    vc = proj[:, VC_OFF:IN_WP]
    bd_c = bd_ref[0:CW, 0:CW]
    mu = _segsum(vc, bd_c) * (1.0 / C_DIM)
    dv = vc - mu
    var = _segsum(dv * dv, bd_c) * (1.0 / C_DIM)
    vg = (dv * lax.rsqrt(var + EPS)).astype(_bf16)
    lane_c = lax.broadcasted_iota(jnp.int32, (C_CHUNK, CW), 1)
    for r in range(x.shape[0] // C_CHUNK):
        rows = slice(r * C_CHUNK, (r + 1) * C_CHUNK)
        vgc = vg[rows]
        mixed = bsf_ref[...]
        for g in range(C_GROUPS):
            mg = _dot(ws_ref[g], vgc)
            mixed = mixed + jnp.where(lane_c // C_DIM == g, mg, 0.0)
        oc_ref[rows, :] = (uc[rows] * mixed).astype(_bf16)


def _pre_call(x2, mod, seq, wts, rope, latent):
    n, d = x2.shape
    tb = PRE_TB
    nblk_seq = seq // tb
    row = lambda i: (i, 0)
    const = lambda i: (0, 0)
    in_specs = [pl.BlockSpec((tb, d), row),
                pl.BlockSpec((1, 1, mod.shape[-1]), lambda i: (i // nblk_seq, 0, 0)),
                pl.BlockSpec(wts['win'].shape, const),
                pl.BlockSpec((1, QA_W), const), pl.BlockSpec((1, KA_W), const),
                pl.BlockSpec((1, M_Q_RANK), const), pl.BlockSpec((1, M_KV_RANK), const),
                pl.BlockSpec(wts['wuq'].shape, const), pl.BlockSpec(wts['wukv'].shape, const),
                pl.BlockSpec(wts['ws'].shape, lambda i: (0, 0, 0)),
                pl.BlockSpec((C_CHUNK, CW), const),
                pl.BlockSpec((QA_W, QA_W), const)]
    args = [x2, mod, wts['win'], wts['aqn'], wts['akn'], wts['mqn'], wts['mkvn'], wts['wuq'], wts['wukv'],
            wts['ws'], wts['bsf'], wts['bd']]
    if latent:
        pos = lambda i: (i % nblk_seq, 0)
        in_specs += [pl.BlockSpec((tb, LANES), pos)] * 4
        args += list(rope)
    widths = [QA_W, KA_W, KA_W, M_HEADS * (M_NOPE + M_ROPE), M_HEADS * (M_NOPE + M_V), M_HEADS * M_ROPE, CW]
    out_specs = [pl.BlockSpec((tb, w), row) for w in widths]
    out_shape = [jax.ShapeDtypeStruct((n, w), _bf16) for w in widths]
    if not latent:
        for w in (KA_W, KA_W, M_KV_RANK, M_ROPE):
            out_specs.append(pl.BlockSpec((tb, w), row))
            out_shape.append(jax.ShapeDtypeStruct((n, w), _f32))
    return pl.pallas_call(
        functools.partial(_pre_kernel, latent=latent, d_model=d),
        grid=(n // tb,),
        in_specs=in_specs, out_specs=out_specs, out_shape=out_shape,
        compiler_params=pltpu.CompilerParams(dimension_semantics=("parallel",), vmem_limit_bytes=VMEM_LIMIT),
        name="pre_lat" if latent else "pre_ctx",
    )(*args)


def _softmax_pv(q, keys, vals):
    s = [_dot_nt(q, k) for k in keys]
    m = s[0].max(axis=-1, keepdims=True)
    for si in s[1:]:
        m = jnp.maximum(m, si.max(axis=-1, keepdims=True))
    den = None
    o = None
    for si, v in zip(s, vals):
        e = jnp.exp(si - m)
        d = e.sum(axis=-1, keepdims=True)
        den = d if den is None else den + d
        pv = _dot(e.astype(_bf16), v)
        o = pv if o is None else o + pv
    return o * (1.0 / den)


def _attn_kernel(*refs, latent):
    if latent:
        (qa_ref, qm_ref, ka_ref, va_ref, kvm_ref, kr_ref, ck_ref, cv_ref, cckv_ref, ckr_ref, wukv_ref,
         oa_ref, om_ref) = refs
    else:
        qa_ref, qm_ref, ka_ref, va_ref, kvm_ref, kr_ref, oa_ref, om_ref = refs
    qb = qa_ref.shape[0]
    nope_w = M_HEADS * M_NOPE

    keys_a = [ka_ref[...]]
    vals_a = [va_ref[...]]
    if latent:
        keys_a = [ck_ref[0].astype(_bf16)] + keys_a
        vals_a = [cv_ref[0].astype(_bf16)] + vals_a
    lane = lax.broadcasted_iota(jnp.int32, (qb, LANES), 1)
    low = lane < HEAD_DIM
    for j in range(QA_W // LANES):
        slab = qa_ref[:, j * LANES:(j + 1) * LANES]
        q2 = jnp.concatenate([jnp.where(low, slab, jnp.zeros_like(slab)),
                              jnp.where(low, jnp.zeros_like(slab), slab)], axis=0)
        o2 = _softmax_pv(q2, keys_a, vals_a)
        oa_ref[:, j * LANES:(j + 1) * LANES] = jnp.where(low, o2[0:qb], o2[qb:]).astype(_bf16)

    kn = [kvm_ref[:, 0:nope_w]]
    vm = [kvm_ref[:, nope_w:]]
    krs = [kr_ref[...]]
    if latent:
        kvc = _dot(cckv_ref[0].astype(_bf16), wukv_ref[...]).astype(_bf16)
        kn = [kvc[:, 0:nope_w]] + kn
        vm = [kvc[:, nope_w:]] + vm
        krs = [ckr_ref[0].astype(_bf16)] + krs
    keys_m = [jnp.concatenate([a, b], axis=1) for a, b in zip(kn, krs)]
    qm = qm_ref[...]
    lane_q = lax.broadcasted_iota(jnp.int32, qm.shape, 1)
    head_q = jnp.where(lane_q < nope_w, lane_q // M_NOPE, (lane_q - nope_w) // M_ROPE)
    lane_o = lax.broadcasted_iota(jnp.int32, (qb, nope_w), 1)
    om = jnp.zeros((qb, nope_w), _f32)
    for p in range(M_HEADS // 2):
        q2 = jnp.concatenate([jnp.where(head_q == 2 * p + t, qm, jnp.zeros_like(qm)) for t in range(2)], axis=0)
        o2 = _softmax_pv(q2, keys_m, vm)
        for t in range(2):
            om = jnp.where(lane_o // M_V == 2 * p + t, o2[t * qb:(t + 1) * qb], om)
    om_ref[...] = om.astype(_bf16)


def _attn_call(pre, seq, cache, wukv, latent):
    qa, ka, va, qm, kvm, kr = pre
    n = qa.shape[0]
    nb = n // seq
    qb = ATT_QB
    nq = seq // qb
    qrow = lambda b, i: (b * nq + i, 0)
    krow = lambda b, i: (b, 0)
    in_specs = [pl.BlockSpec((qb, qa.shape[1]), qrow), pl.BlockSpec((qb, qm.shape[1]), qrow),
                pl.BlockSpec((seq, ka.shape[1]), krow), pl.BlockSpec((seq, va.shape[1]), krow),
                pl.BlockSpec((seq, kvm.shape[1]), krow), pl.BlockSpec((seq, kr.shape[1]), krow)]
    args = [qa, qm, ka, va, kvm, kr]
    if latent:
        for c in cache:
            in_specs.append(pl.BlockSpec((1,) + c.shape[1:], lambda b, i: (b, 0, 0)))
            args.append(c)
        in_specs.append(pl.BlockSpec(wukv.shape, lambda b, i: (0, 0)))
        args.append(wukv)
    return pl.pallas_call(
        functools.partial(_attn_kernel, latent=latent),
        grid=(nb, nq),
        in_specs=in_specs,
        out_specs=[pl.BlockSpec((qb, QA_W), qrow), pl.BlockSpec((qb, M_HEADS * M_V), qrow)],
        out_shape=[jax.ShapeDtypeStruct((n, QA_W), _bf16), jax.ShapeDtypeStruct((n, M_HEADS * M_V), _bf16)],
        compiler_params=pltpu.CompilerParams(dimension_semantics=("parallel", "parallel"),
                                             vmem_limit_bytes=VMEM_LIMIT),
        name="attn_lat" if latent else "attn_ctx",
    )(*args)


def _post_kernel(oa_ref, om_ref, oc_ref, x_ref, mod_ref, wo_ref, g_ref, b_ref, x1_ref, h2_ref, *, alpha, d_model):
    D = d_model
    a_w = oa_ref.shape[1]
    m_w = om_ref.shape[1]
    mix = (_dot(oa_ref[...], wo_ref[0:a_w, :]) + _dot(om_ref[...], wo_ref[a_w:a_w + m_w, :])
           + _dot(oc_ref[...], wo_ref[a_w + m_w:, :]))
    mod = mod_ref[0]
    g1 = mod[:, 2 * D:3 * D]
    sh2 = mod[:, 3 * D:4 * D]
    sc2 = mod[:, 4 * D:5 * D]
    x1 = _ln_rows(alpha * x_ref[...] + g1 * mix) * g_ref[...] + b_ref[...]
    x1_ref[...] = x1
    h2_ref[...] = (_ln_rows(x1) * (1.0 + sc2) + sh2).astype(_bf16)


def _post_call(oa, om, oc, x2, mod, seq, wo, g, b, alpha):
    n, d = x2.shape
    tb = PRE_TB
    nblk_seq = seq // tb
    row = lambda i: (i, 0)
    const = lambda i: (0, 0)
    return pl.pallas_call(
        functools.partial(_post_kernel, alpha=alpha, d_model=d),
        grid=(n // tb,),
        in_specs=[pl.BlockSpec((tb, oa.shape[1]), row), pl.BlockSpec((tb, om.shape[1]), row),
                  pl.BlockSpec((tb, oc.shape[1]), row), pl.BlockSpec((tb, d), row),
                  pl.BlockSpec((1, 1, mod.shape[-1]), lambda i: (i // nblk_seq, 0, 0)),
                  pl.BlockSpec(wo.shape, const), pl.BlockSpec((1, d), const), pl.BlockSpec((1, d), const)],
        out_specs=[pl.BlockSpec((tb, d), row), pl.BlockSpec((tb, d), row)],
        out_shape=[jax.ShapeDtypeStruct((n, d), _f32), jax.ShapeDtypeStruct((n, d), _bf16)],
        compiler_params=pltpu.CompilerParams(dimension_semantics=("parallel",), vmem_limit_bytes=VMEM_LIMIT),
        name="post",
    )(oa, om, oc, x2, mod, wo, g, b)


def _extract_top(s, n):
    tops = []
    for j in range(n):
        m = jnp.max(s, axis=0, keepdims=True)
        tops.append(m)
        if j + 1 < n:
            s = jnp.where(s == m, NEG_BIG, s)
    return tops


def _stack_rows(rows):
    n = len(rows)
    rid = lax.broadcasted_iota(jnp.int32, (n, LANES), 0)
    out = jnp.broadcast_to(rows[0], (n, LANES))
    for j in range(1, n):
        out = jnp.where(rid == j, rows[j], out)
    return out


def _peer_select(s1, s2):
    k = PEER_TOPK
    v1 = _extract_top(s1, k + 1)
    v2 = _extract_top(s2, k + 1)
    col1 = _stack_rows(v1[0:k])
    col2 = _stack_rows(v2[0:k])
    pieces = [v1[0] + col2]
    for a in range(1, 8):
        pieces.append(v1[a] + col2[0:8])
    pieces.append(col1[8:16] + v2[0])
    tail = jnp.where(lax.broadcasted_iota(jnp.int32, (8, LANES), 0) == 0, v1[0] + v2[k],
                     jnp.where(lax.broadcasted_iota(jnp.int32, (8, LANES), 0) == 1, v1[k] + v2[0], NEG_BIG))
    pieces.append(tail)
    cand = jnp.concatenate(pieces, axis=0)
    tops = _extract_top(cand, k + 1)
    z = jnp.ones_like(tops[0])
    for j in range(1, k):
        z = z + jnp.exp(tops[j] - tops[0])
    tau = 0.5 * (tops[k - 1] + tops[k])
    thr = tau - s1
    e1 = jnp.exp(s1 - v1[0]) * (1.0 / z)
    e2 = jnp.exp(s2 - v2[0])
    return thr, e1, e2


def _gelu2(x):
    c = math.sqrt(2.0 / math.pi)
    return x * (1.0 + jnp.tanh(x * (c + (c * 0.044715) * (x * x))))


def _peer_kernel(h2_ref, x1_ref, mod_ref, wq_ref, k1_ref, k2_ref, u_ref, vt_ref, g_ref, b_ref, y_ref,
                 s1_ref, s2_ref, thr_ref, e1_ref, e2_ref, a_ref, p_ref, acc_ref, *, alpha, d_model):
    D = d_model
    e = pl.program_id(1)
    t = h2_ref.shape[0]
    nchunk = t // LANES
    slabs = u_ref.shape[0] // N_KEYS

    @pl.when(e == 0)
    def _select():
        q = _dot(h2_ref[...], wq_ref[...]).astype(_bf16)
        for h in range(PEER_HEADS):
            base = h * 2 * PEER_HALF
            s1_ref[h] = _dot_nt(k1_ref[...], q[:, base:base + PEER_HALF])
            s2_ref[h] = _dot_nt(k2_ref[...], q[:, base + PEER_HALF:base + 2 * PEER_HALF])

        def body(it, carry):
            h = it // nchunk
            c = pl.multiple_of((it % nchunk) * LANES, LANES)
            thr, e1, e2 = _peer_select(s1_ref[h, :, pl.ds(c, LANES)], s2_ref[h, :, pl.ds(c, LANES)])
            thr_ref[h, :, pl.ds(c, LANES)] = thr
            e1_ref[h, :, pl.ds(c, LANES)] = 0.5 * e1
            e2_ref[h, :, pl.ds(c, LANES)] = e2
            return carry

        lax.fori_loop(0, PEER_HEADS * nchunk, body, 0)
        acc_ref[...] = jnp.zeros_like(acc_ref)

    a_ref[...] = _dot_nt(u_ref[...], h2_ref[...])

    def chunk_body(c, carry):
        cols = pl.ds(pl.multiple_of(c * LANES, LANES), LANES)
        for g in range(slabs // 8):
            rows8 = pl.ds(pl.multiple_of(e * slabs + g * 8, 8), 8)
            thr8 = [thr_ref[h, rows8, cols] for h in range(PEER_HEADS)]
            e18 = [e1_ref[h, rows8, cols] for h in range(PEER_HEADS)]
            for k in range(8):
                w = jnp.zeros((N_KEYS, LANES), _f32)
                for h in range(PEER_HEADS):
                    hit = s2_ref[h, :, cols] >= thr8[h][k:k + 1]
                    w = w + jnp.where(hit, e2_ref[h, :, cols], 0.0) * e18[h][k:k + 1]
                rows = slice((g * 8 + k) * N_KEYS, (g * 8 + k + 1) * N_KEYS)
                p_ref[rows, cols] = (_gelu2(a_ref[rows, cols]) * w).astype(_bf16)
        return carry

    lax.fori_loop(0, nchunk, chunk_body, 0)
    acc_ref[...] += _dot(vt_ref[...], p_ref[...])

    @pl.when(e == pl.num_programs(1) - 1)
    def _finish():
        mod = mod_ref[0]
        g2 = mod[:, 5 * D:6 * D]
        ff = acc_ref[...].T
        y_ref[...] = _ln_rows(alpha * x1_ref[...] + g2 * ff) * g_ref[...] + b_ref[...]


def _peer_call(h2, x1, mod, seq, wq, k1, k2, u, vt, g, b, alpha):
    n, d = x1.shape
    t = PEER_T
    eb = PEER_EB
    n_exp = u.shape[0]
    nblk_seq = seq // t
    row = lambda i, e: (i, 0)
    const = lambda i, e: (0, 0)
    sel = pltpu.VMEM((PEER_HEADS, N_KEYS, t), _f32)
    return pl.pallas_call(
        functools.partial(_peer_kernel, alpha=alpha, d_model=d),
        grid=(n // t, n_exp // eb),
        in_specs=[pl.BlockSpec((t, d), row), pl.BlockSpec((t, d), row),
                  pl.BlockSpec((1, 1, mod.shape[-1]), lambda i, e: (i // nblk_seq, 0, 0)),
                  pl.BlockSpec(wq.shape, const), pl.BlockSpec(k1.shape, const), pl.BlockSpec(k2.shape, const),
                  pl.BlockSpec((eb, d), lambda i, e: (e, 0)), pl.BlockSpec((d, eb), lambda i, e: (0, e)),
                  pl.BlockSpec((1, d), const), pl.BlockSpec((1, d), const)],
        out_specs=pl.BlockSpec((t, d), row),
        out_shape=jax.ShapeDtypeStruct((n, d), _f32),
        scratch_shapes=[sel, sel, sel, sel, sel,
                        pltpu.VMEM((eb, t), _f32), pltpu.VMEM((eb, t), _bf16), pltpu.VMEM((d, t), _f32)],
        compiler_params=pltpu.CompilerParams(dimension_semantics=("parallel", "arbitrary"),
                                             vmem_limit_bytes=VMEM_LIMIT),
        name="peer",
    )(h2, x1, mod, wq, k1, k2, u, vt, g, b)


def _rope_tables(seq, head_dim):
    m = head_dim // 4
    lane = np.arange(LANES)
    d = lane % head_dim
    use_col = (d // (2 * m)) == 1
    within = d % (2 * m)
    second = within // m
    freqs = ROPE_BASE ** (-np.arange(m, dtype=np.float64) / m)
    f = freqs[within % m]
    tpos = np.arange(seq)
    pos = np.where(use_col[None, :], (tpos % GRID_W)[:, None], (tpos // GRID_W)[:, None]).astype(np.float64)
    ang = pos * f[None, :]
    sign = np.where(second == 0, -1.0, 1.0)
    return jnp.asarray(np.cos(ang), _f32), jnp.asarray(np.sin(ang) * sign[None, :], _f32)


def _layer_weights(l, w_in, attn_q_norm, attn_k_norm, mla_q_norm, mla_kv_norm, w_uq, w_ukv, gmlp_ws, gmlp_b, w_o):
    offs = np.cumsum([0, QA_W, KA_W, KA_W, M_Q_RANK, M_KV_RANK, M_ROPE, CW, CW])
    qa_perm = np.concatenate([np.arange(HEAD_DIM) + HEAD_DIM * (j + 4 * half)
                              for j in range(A_HEADS // 2) for half in range(2)])
    cols = np.concatenate([qa_perm, np.arange(offs[1], offs[5]),
                           np.tile(np.arange(offs[5], offs[6]), M_HEADS), np.arange(offs[6], offs[8])])
    qd = M_NOPE + M_ROPE
    uq_cols = np.concatenate([np.arange(M_NOPE) + h * qd for h in range(M_HEADS)]
                             + [np.arange(M_ROPE) + h * qd + M_NOPE for h in range(M_HEADS)])
    kd = M_NOPE + M_V
    ukv_cols = np.concatenate([np.arange(M_NOPE) + h * kd for h in range(M_HEADS)]
                              + [np.arange(M_V) + h * kd + M_NOPE for h in range(M_HEADS)])
    wo_rows = np.concatenate([qa_perm, np.arange(QA_W, w_o.shape[1])])
    seg = np.arange(QA_W) // HEAD_DIM
    return {
        'win': w_in[l][:, cols].astype(_bf16),
        'aqn': jnp.tile(attn_q_norm[l], A_HEADS)[None, :],
        'akn': jnp.tile(attn_k_norm[l], A_KV)[None, :],
        'mqn': mla_q_norm[l][None, :],
        'mkvn': mla_kv_norm[l][None, :],
        'wuq': w_uq[l][:, uq_cols].astype(_bf16),
        'wukv': w_ukv[l][:, ukv_cols].astype(_bf16),
        'ws': gmlp_ws[l].astype(_bf16),
        'bsf': jnp.repeat(gmlp_b[l].T, C_DIM, axis=1),
        'bd': jnp.asarray(seg[:, None] == seg[None, :], _bf16),
        'wo': w_o[l][wo_rows, :].astype(_bf16),
    }


def kernel(x_prompt, x_sample, cache_attn_k, cache_attn_v, cache_mla_ckv, cache_mla_krope, c, c_ctx, w_mod, b_mod, w_in, attn_q_norm, attn_k_norm, mla_q_norm, mla_kv_norm, w_uq, w_ukv, gmlp_ws, gmlp_b, w_o, ln1_g, ln1_b, ln2_g, ln2_b, peer_wq, peer_k1, peer_k2, peer_u, peer_v):
    batch, seq, d = x_prompt.shape
    dec_batch, dec_seq, _ = x_sample.shape
    depth = w_in.shape[0]
    past = cache_attn_k.shape[2]
    alpha = (2.0 * depth) ** 0.25

    n_rows = 8 * ((1 + dec_batch + 7) // 8)
    c_all = jnp.zeros((n_rows, d), _f32).at[0].set(c_ctx).at[1:1 + dec_batch].set(c)
    mod_all = _mod_call(c_all, w_mod, b_mod)

    rope = _rope_tables(dec_seq, HEAD_DIM) + _rope_tables(dec_seq, M_ROPE)
    xp = x_prompt.reshape(batch * seq, d)
    xs = x_sample.reshape(dec_batch * dec_seq, d)
    new = []
    for l in range(depth):
        wts = _layer_weights(l, w_in, attn_q_norm, attn_k_norm, mla_q_norm, mla_kv_norm, w_uq, w_ukv,
                             gmlp_ws, gmlp_b, w_o)
        peer_w = (peer_wq[l].astype(_bf16), peer_k1[l].astype(_bf16), peer_k2[l].astype(_bf16),
                  peer_u[l].astype(_bf16), peer_v[l].T.astype(_bf16))
        ln1 = (ln1_g[l][None, :], ln1_b[l][None, :])
        ln2 = (ln2_g[l][None, :], ln2_b[l][None, :])
        mod_ctx = mod_all[l, 0:1][:, None, :]
        mod_lat = mod_all[l, 1:1 + dec_batch][:, None, :]
        cache = (cache_attn_k[:, l].reshape(dec_batch, past, KA_W),
                 cache_attn_v[:, l].reshape(dec_batch, past, KA_W),
                 cache_mla_ckv[:, l],
                 jnp.tile(cache_mla_krope[:, l], (1, 1, M_HEADS)))

        pre = _pre_call(xp, mod_ctx, batch * seq, wts, None, latent=False)
        new.append(pre[7:])
        oa, om = _attn_call(pre[0:6], seq, None, None, latent=False)
        x1, h2 = _post_call(oa, om, pre[6], xp, mod_ctx, batch * seq, wts['wo'], *ln1, alpha)
        xp = _peer_call(h2, x1, mod_ctx, batch * seq, *peer_w, *ln2, alpha)

        pre = _pre_call(xs, mod_lat, dec_seq, wts, rope, latent=True)
        oa, om = _attn_call(pre[0:6], dec_seq, cache, wts['wukv'], latent=True)
        x1, h2 = _post_call(oa, om, pre[6], xs, mod_lat, dec_seq, wts['wo'], *ln1, alpha)
        xs = _peer_call(h2, x1, mod_lat, dec_seq, *peer_w, *ln2, alpha)

    def stack(i, tail):
        return jnp.stack([new[l][i].reshape((batch, seq) + tail) for l in range(depth)], axis=1)

    return (xp.reshape(batch, seq, d), xs.reshape(dec_batch, dec_seq, d),
            stack(0, (A_KV, HEAD_DIM)), stack(1, (A_KV, HEAD_DIM)), stack(2, (M_KV_RANK,)), stack(3, (M_ROPE,)))
```

```python
import functools
import math

import numpy as np
import jax
import jax.numpy as jnp
from jax import lax
from jax.experimental import pallas as pl
from jax.experimental.pallas import tpu as pltpu

A_HEADS = 8
A_KV = 2
HEAD_DIM = 64
M_HEADS = 4
M_Q_RANK = 256
M_KV_RANK = 128
M_NOPE = 64
M_ROPE = 32
M_V = 64
C_GROUPS = 4
C_DIM = 64
C_CHUNK = 128
PEER_HEADS = 8
N_KEYS = 128
PEER_HALF = 128
PEER_TOPK = 16
GRID_W = 64
EPS = 1e-6
ROPE_BASE = 10000.0

LANES = 128
NEG_BIG = -3.0e38

QA_W = A_HEADS * HEAD_DIM
KA_W = A_KV * HEAD_DIM
CQ_OFF = QA_W + 2 * KA_W
CKV_OFF = CQ_OFF + M_Q_RANK
KR_OFF = CKV_OFF + M_KV_RANK
UC_OFF = KR_OFF + M_HEADS * M_ROPE
CW = C_GROUPS * C_DIM
VC_OFF = UC_OFF + CW
IN_WP = VC_OFF + CW

PRE_TB = 256
ATT_QB = 256
PEER_T = 512
PEER_EB = 1024
VMEM_LIMIT = 56 * 1024 * 1024

_bf16 = jnp.bfloat16
_f32 = jnp.float32


def _dot(a, b):
    return jnp.dot(a, b, preferred_element_type=_f32)


def _dot_nt(a, b):
    return lax.dot_general(a, b, (((1,), (1,)), ((), ())), preferred_element_type=_f32)


def _split(x):
    hi = x.astype(_bf16)
    lo = (x - hi.astype(_f32)).astype(_bf16)
    return hi, lo


def _segsum(x, ones_bd):
    hi, lo = _split(x)
    return _dot(hi, ones_bd) + _dot(lo, ones_bd)


def _ln_rows(x):
    mu = jnp.mean(x, axis=-1, keepdims=True)
    d = x - mu
    var = jnp.mean(d * d, axis=-1, keepdims=True)
    return d * lax.rsqrt(var + EPS)


def _rot(x, cos, sin, half):
    lane = lax.broadcasted_iota(jnp.int32, x.shape, 1)
    up = pltpu.roll(x, LANES - half, axis=1)
    dn = pltpu.roll(x, half, axis=1)
    partner = jnp.where((lane & half) == 0, up, dn)
    return x * cos + partner * sin


def _mod_kernel(c_ref, w_ref, b_ref, o_ref):
    c = c_ref[...]
    a = c * (1.0 / (1.0 + jnp.exp(-c)))
    a_hi, a_lo = _split(a)
    w_hi, w_lo = _split(w_ref[0])
    o_ref[0] = _dot(a_hi, w_hi) + _dot(a_lo, w_hi) + _dot(a_hi, w_lo) + b_ref[0]


def _mod_call(c_all, w_mod, b_mod):
    depth, d, d6 = w_mod.shape
    rows = c_all.shape[0]
    tn = 1024
    return pl.pallas_call(
        _mod_kernel,
        grid=(depth, d6 // tn),
        in_specs=[pl.BlockSpec((rows, d), lambda l, j: (0, 0)),
                  pl.BlockSpec((1, d, tn), lambda l, j: (l, 0, j)),
                  pl.BlockSpec((1, 1, tn), lambda l, j: (l, 0, j))],
        out_specs=pl.BlockSpec((1, rows, tn), lambda l, j: (l, 0, j)),
        out_shape=jax.ShapeDtypeStruct((depth, rows, d6), _f32),
        compiler_params=pltpu.CompilerParams(dimension_semantics=("parallel", "parallel"),
                                             vmem_limit_bytes=VMEM_LIMIT),
        name="mod",
    )(c_all, w_mod, b_mod.reshape(depth, 1, d6))


def _pre_kernel(*refs, latent, d_model):
    if latent:
        (x_ref, mod_ref, win_ref, aqn_ref, akn_ref, mqn_ref, mkvn_ref, wuq_ref, wukv_ref, ws_ref, bsf_ref,
         bd_ref, cosa_ref, sina_ref, cosm_ref, sinm_ref,
         qa_ref, ka_ref, va_ref, qm_ref, kvm_ref, kr_ref, oc_ref) = refs
    else:
        (x_ref, mod_ref, win_ref, aqn_ref, akn_ref, mqn_ref, mkvn_ref, wuq_ref, wukv_ref, ws_ref, bsf_ref,
         bd_ref,
         qa_ref, ka_ref, va_ref, qm_ref, kvm_ref, kr_ref, oc_ref,
         nk_ref, nv_ref, nckv_ref, nkr_ref) = refs
    D = d_model
    x = x_ref[...]
    mod = mod_ref[0]
    sh1 = mod[:, 0:D]
    sc1 = mod[:, D:2 * D]
    h = _ln_rows(x) * (1.0 + sc1) + sh1
    proj = _dot(h.astype(_bf16), win_ref[...])

    qa = proj[:, 0:QA_W]
    ms = _segsum(qa * qa, bd_ref[...]) * (1.0 / HEAD_DIM)
    qa = qa * lax.rsqrt(ms + EPS) * aqn_ref[...]
    ka = proj[:, QA_W:QA_W + KA_W]
    ms = _segsum(ka * ka, bd_ref[0:KA_W, 0:KA_W]) * (1.0 / HEAD_DIM)
    ka = ka * lax.rsqrt(ms + EPS) * akn_ref[...]
    va = proj[:, QA_W + KA_W:CQ_OFF]
    if not latent:
        nk_ref[...] = ka
        nv_ref[...] = va
    for j in range(QA_W // LANES):
        slab = qa[:, j * LANES:(j + 1) * LANES]
        if latent:
            slab = _rot(slab, cosa_ref[...], sina_ref[...], HEAD_DIM // 4)
        qa_ref[:, j * LANES:(j + 1) * LANES] = (slab * (1.0 / math.sqrt(HEAD_DIM))).astype(_bf16)
    if latent:
        ka = _rot(ka, cosa_ref[...], sina_ref[...], HEAD_DIM // 4)
    ka_ref[...] = ka.astype(_bf16)
    va_ref[...] = va.astype(_bf16)

    cq = proj[:, CQ_OFF:CKV_OFF]
    cq = cq * lax.rsqrt(jnp.mean(cq * cq, axis=-1, keepdims=True) + EPS) * mqn_ref[...]
    qm = _dot(cq.astype(_bf16), wuq_ref[...])
    nope_w = M_HEADS * M_NOPE
    qscale = 1.0 / math.sqrt(M_NOPE + M_ROPE)
    qm_ref[:, 0:nope_w] = (qm[:, 0:nope_w] * qscale).astype(_bf16)
    qr = qm[:, nope_w:]
    if latent:
        qr = _rot(qr, cosm_ref[...], sinm_ref[...], M_ROPE // 4)
    qm_ref[:, nope_w:] = (qr * qscale).astype(_bf16)
    ckv = proj[:, CKV_OFF:KR_OFF]
    ckv = ckv * lax.rsqrt(jnp.mean(ckv * ckv, axis=-1, keepdims=True) + EPS) * mkvn_ref[...]
    kvm_ref[...] = _dot(ckv.astype(_bf16), wukv_ref[...]).astype(_bf16)
    kr = proj[:, KR_OFF:UC_OFF]
    if not latent:
        nckv_ref[...] = ckv
        nkr_ref[...] = kr[:, 0:M_ROPE]
    else:
        kr = _rot(kr, cosm_ref[...], sinm_ref[...], M_ROPE // 4)
    kr_ref[...] = kr.astype(_bf16)

    uc = proj[:, UC_OFF:VC_OFF]
    vc = proj[:, VC_OFF:IN_WP]
    bd_c = bd_ref[0:CW, 0:CW]
    mu = _segsum(vc, bd_c) * (1.0 / C_DIM)
    dv = vc - mu
    var = _segsum(dv * dv, bd_c) * (1.0 / C_DIM)
    vg = (dv * lax.rsqrt(var + EPS)).astype(_bf16)
    lane_c = lax.broadcasted_iota(jnp.int32, (C_CHUNK, CW), 1)
    for r in range(x.shape[0] // C_CHUNK):
        rows = slice(r * C_CHUNK, (r + 1) * C_CHUNK)
        vgc = vg[rows]
        mixed = bsf_ref[...]
        for g in range(C_GROUPS):
            mg = _dot(ws_ref[g], vgc)
            mixed = mixed + jnp.where(lane_c // C_DIM == g, mg, 0.0)
        oc_ref[rows, :] = (uc[rows] * mixed).astype(_bf16)


def _pre_call(x2, mod, seq, wts, rope, latent):
    n, d = x2.shape
    tb = PRE_TB
    nblk_seq = seq // tb
    row = lambda i: (i, 0)
    const = lambda i: (0, 0)
    in_specs = [pl.BlockSpec((tb, d), row),
                pl.BlockSpec((1, 1, mod.shape[-1]), lambda i: (i // nblk_seq, 0, 0)),
                pl.BlockSpec(wts['win'].shape, const),
                pl.BlockSpec((1, QA_W), const), pl.BlockSpec((1, KA_W), const),
                pl.BlockSpec((1, M_Q_RANK), const), pl.BlockSpec((1, M_KV_RANK), const),
                pl.BlockSpec(wts['wuq'].shape, const), pl.BlockSpec(wts['wukv'].shape, const),
                pl.BlockSpec(wts['ws'].shape, lambda i: (0, 0, 0)),
                pl.BlockSpec((C_CHUNK, CW), const),
                pl.BlockSpec((QA_W, QA_W), const)]
    args = [x2, mod, wts['win'], wts['aqn'], wts['akn'], wts['mqn'], wts['mkvn'], wts['wuq'], wts['wukv'],
            wts['ws'], wts['bsf'], wts['bd']]
    if latent:
        pos = lambda i: (i % nblk_seq, 0)
        in_specs += [pl.BlockSpec((tb, LANES), pos)] * 4
        args += list(rope)
    widths = [QA_W, KA_W, KA_W, M_HEADS * (M_NOPE + M_ROPE), M_HEADS * (M_NOPE + M_V), M_HEADS * M_ROPE, CW]
    out_specs = [pl.BlockSpec((tb, w), row) for w in widths]
    out_shape = [jax.ShapeDtypeStruct((n, w), _bf16) for w in widths]
    if not latent:
        for w in (KA_W, KA_W, M_KV_RANK, M_ROPE):
            out_specs.append(pl.BlockSpec((tb, w), row))
            out_shape.append(jax.ShapeDtypeStruct((n, w), _f32))
    return pl.pallas_call(
        functools.partial(_pre_kernel, latent=latent, d_model=d),
        grid=(n // tb,),
        in_specs=in_specs, out_specs=out_specs, out_shape=out_shape,
        compiler_params=pltpu.CompilerParams(dimension_semantics=("parallel",), vmem_limit_bytes=VMEM_LIMIT),
        name="pre_lat" if latent else "pre_ctx",
    )(*args)


def _softmax_pv(q, keys, vals):
    s = [_dot_nt(q, k) for k in keys]
    m = s[0].max(axis=-1, keepdims=True)
    for si in s[1:]:
        m = jnp.maximum(m, si.max(axis=-1, keepdims=True))
    den = None
    o = None
    for si, v in zip(s, vals):
        e = jnp.exp(si - m)
        d = e.sum(axis=-1, keepdims=True)
        den = d if den is None else den + d
        pv = _dot(e.astype(_bf16), v)
        o = pv if o is None else o + pv
    return o * (1.0 / den)


def _attn_kernel(*refs, latent):
    if latent:
        (qa_ref, qm_ref, ka_ref, va_ref, kvm_ref, kr_ref, ck_ref, cv_ref, cckv_ref, ckr_ref, wukv_ref,
         oa_ref, om_ref) = refs
    else:
        qa_ref, qm_ref, ka_ref, va_ref, kvm_ref, kr_ref, oa_ref, om_ref = refs
    qb = qa_ref.shape[0]
    nope_w = M_HEADS * M_NOPE

    keys_a = [ka_ref[...]]
    vals_a = [va_ref[...]]
    if latent:
        keys_a = [ck_ref[0].astype(_bf16)] + keys_a
        vals_a = [cv_ref[0].astype(_bf16)] + vals_a
    lane = lax.broadcasted_iota(jnp.int32, (qb, LANES), 1)
    low = lane < HEAD_DIM
    for j in range(QA_W // LANES):
        slab = qa_ref[:, j * LANES:(j + 1) * LANES]
        q2 = jnp.concatenate([jnp.where(low, slab, jnp.zeros_like(slab)),
                              jnp.where(low, jnp.zeros_like(slab), slab)], axis=0)
        o2 = _softmax_pv(q2, keys_a, vals_a)
        oa_ref[:, j * LANES:(j + 1) * LANES] = jnp.where(low, o2[0:qb], o2[qb:]).astype(_bf16)

    kn = [kvm_ref[:, 0:nope_w]]
    vm = [kvm_ref[:, nope_w:]]
    krs = [kr_ref[...]]
    if latent:
        kvc = _dot(cckv_ref[0].astype(_bf16), wukv_ref[...]).astype(_bf16)
        kn = [kvc[:, 0:nope_w]] + kn
        vm = [kvc[:, nope_w:]] + vm
        krs = [ckr_ref[0].astype(_bf16)] + krs
    keys_m = [jnp.concatenate([a, b], axis=1) for a, b in zip(kn, krs)]
    qm = qm_ref[...]
    lane_q = lax.broadcasted_iota(jnp.int32, qm.shape, 1)
    head_q = jnp.where(lane_q < nope_w, lane_q // M_NOPE, (lane_q - nope_w) // M_ROPE)
    lane_o = lax.broadcasted_iota(jnp.int32, (qb, nope_w), 1)
    om = jnp.zeros((qb, nope_w), _f32)
    for p in range(M_HEADS // 2):
        q2 = jnp.concatenate([jnp.where(head_q == 2 * p + t, qm, jnp.zeros_like(qm)) for t in range(2)], axis=0)
        o2 = _softmax_pv(q2, keys_m, vm)
        for t in range(2):
            om = jnp.where(lane_o // M_V == 2 * p + t, o2[t * qb:(t + 1) * qb], om)
    om_ref[...] = om.astype(_bf16)


def _attn_call(pre, seq, cache, wukv, latent):
    qa, ka, va, qm, kvm, kr = pre
    n = qa.shape[0]
    nb = n // seq
    qb = ATT_QB
    nq = seq // qb
    qrow = lambda b, i: (b * nq + i, 0)
    krow = lambda b, i: (b, 0)
    in_specs = [pl.BlockSpec((qb, qa.shape[1]), qrow), pl.BlockSpec((qb, qm.shape[1]), qrow),
                pl.BlockSpec((seq, ka.shape[1]), krow), pl.BlockSpec((seq, va.shape[1]), krow),
                pl.BlockSpec((seq, kvm.shape[1]), krow), pl.BlockSpec((seq, kr.shape[1]), krow)]
    args = [qa, qm, ka, va, kvm, kr]
    if latent:
        for c in cache:
            in_specs.append(pl.BlockSpec((1,) + c.shape[1:], lambda b, i: (b, 0, 0)))
            args.append(c)
        in_specs.append(pl.BlockSpec(wukv.shape, lambda b, i: (0, 0)))
        args.append(wukv)
    return pl.pallas_call(
        functools.partial(_attn_kernel, latent=latent),
        grid=(nb, nq),
        in_specs=in_specs,
        out_specs=[pl.BlockSpec((qb, QA_W), qrow), pl.BlockSpec((qb, M_HEADS * M_V), qrow)],
        out_shape=[jax.ShapeDtypeStruct((n, QA_W), _bf16), jax.ShapeDtypeStruct((n, M_HEADS * M_V), _bf16)],
        compiler_params=pltpu.CompilerParams(dimension_semantics=("parallel", "parallel"),
                                             vmem_limit_bytes=VMEM_LIMIT),
        name="attn_lat" if latent else "attn_ctx",
    )(*args)


def _post_kernel(oa_ref, om_ref, oc_ref, x_ref, mod_ref, wo_ref, g_ref, b_ref, x1_ref, h2_ref, *, alpha, d_model):
    D = d_model
    a_w = oa_ref.shape[1]
    m_w = om_ref.shape[1]
    mix = (_dot(oa_ref[...], wo_ref[0:a_w, :]) + _dot(om_ref[...], wo_ref[a_w:a_w + m_w, :])
           + _dot(oc_ref[...], wo_ref[a_w + m_w:, :]))
    mod = mod_ref[0]
    g1 = mod[:, 2 * D:3 * D]
    sh2 = mod[:, 3 * D:4 * D]
    sc2 = mod[:, 4 * D:5 * D]
    x1 = _ln_rows(alpha * x_ref[...] + g1 * mix) * g_ref[...] + b_ref[...]
    x1_ref[...] = x1
    h2_ref[...] = (_ln_rows(x1) * (1.0 + sc2) + sh2).astype(_bf16)


def _post_call(oa, om, oc, x2, mod, seq, wo, g, b, alpha):
    n, d = x2.shape
    tb = PRE_TB
    nblk_seq = seq // tb
    row = lambda i: (i, 0)
    const = lambda i: (0, 0)
    return pl.pallas_call(
        functools.partial(_post_kernel, alpha=alpha, d_model=d),
        grid=(n // tb,),
        in_specs=[pl.BlockSpec((tb, oa.shape[1]), row), pl.BlockSpec((tb, om.shape[1]), row),
                  pl.BlockSpec((tb, oc.shape[1]), row), pl.BlockSpec((tb, d), row),
                  pl.BlockSpec((1, 1, mod.shape[-1]), lambda i: (i // nblk_seq, 0, 0)),
                  pl.BlockSpec(wo.shape, const), pl.BlockSpec((1, d), const), pl.BlockSpec((1, d), const)],
        out_specs=[pl.BlockSpec((tb, d), row), pl.BlockSpec((tb, d), row)],
        out_shape=[jax.ShapeDtypeStruct((n, d), _f32), jax.ShapeDtypeStruct((n, d), _bf16)],
        compiler_params=pltpu.CompilerParams(dimension_semantics=("parallel",), vmem_limit_bytes=VMEM_LIMIT),
        name="post",
    )(oa, om, oc, x2, mod, wo, g, b)


def _extract_top(s, n, ranked=0):
    tops = []
    rank = jnp.full(s.shape, float(ranked), _f32)
    for j in range(n):
        m = jnp.max(s, axis=0, keepdims=True)
        tops.append(m)
        top = s == m
        if j < ranked:
            rank = jnp.where(top, float(j), rank)
        if j + 1 < n:
            s = jnp.where(top, NEG_BIG, s)
    return (tops, rank) if ranked else tops


def _stack_rows(rows):
    n = len(rows)
    rid = lax.broadcasted_iota(jnp.int32, (n, LANES), 0)
    out = jnp.broadcast_to(rows[0], (n, LANES))
    for j in range(1, n):
        out = jnp.where(rid == j, rows[j], out)
    return out


def _peer_select(s1, s2):
    k = PEER_TOPK
    v1 = _extract_top(s1, k + 1)
    v2, rank2 = _extract_top(s2, k + 1, ranked=k)
    col1 = _stack_rows(v1[0:k])
    col2 = _stack_rows(v2[0:k])
    pieces = [v1[0] + col2]
    for a in range(1, 8):
        pieces.append(v1[a] + col2[0:8])
    pieces.append(col1[8:16] + v2[0])
    tail = jnp.where(lax.broadcasted_iota(jnp.int32, (8, LANES), 0) == 0, v1[0] + v2[k],
                     jnp.where(lax.broadcasted_iota(jnp.int32, (8, LANES), 0) == 1, v1[k] + v2[0], NEG_BIG))
    pieces.append(tail)
    cand = jnp.concatenate(pieces, axis=0)
    tops = _extract_top(cand, k + 1)
    z = jnp.ones_like(tops[0])
    for j in range(1, k):
        z = z + jnp.exp(tops[j] - tops[0])
    tau = 0.5 * (tops[k - 1] + tops[k])
    cnt = jnp.zeros_like(s1)
    for b in range(8):
        cnt = cnt + jnp.where(s1 >= tau - v2[b], 1.0, 0.0)
    extra = jnp.zeros_like(tau)
    for b in range(8, k):
        extra = extra + jnp.where(v1[0] >= tau - v2[b], 1.0, 0.0)
    cnt = cnt + jnp.where(s1 >= v1[0], extra, 0.0)
    e1 = jnp.exp(s1 - v1[0]) * (1.0 / z)
    e2 = jnp.exp(s2 - v2[0])
    return rank2, cnt, e1, e2


def _gelu2(x):
    c = math.sqrt(2.0 / math.pi)
    return x * (1.0 + jnp.tanh(x * (c + (c * 0.044715) * (x * x))))


def _peer_kernel(h2_ref, x1_ref, mod_ref, wq_ref, k1_ref, k2_ref, u_ref, vt_ref, g_ref, b_ref, y_ref,
                 s2_ref, rank_ref, e2_ref, cnt_ref, e1_ref, a0_ref, a1_ref, p0_ref, p1_ref, acc_ref,
                 *, alpha, d_model, n_tiles):
    D = d_model
    e = pl.program_id(1)
    t = h2_ref.shape[0]
    nchunk = t // LANES
    half = t // 2
    slabs = u_ref.shape[0] // N_KEYS
    a_refs = (a0_ref, a1_ref)
    p_refs = (p0_ref, p1_ref)

    def gate_chunk(off, a_ref, p_ref):
        cols = pl.ds(off, LANES)
        for g in range(slabs // 8):
            rows8 = pl.ds(pl.multiple_of((e - 1) * slabs + g * 8, 8), 8)
            cnt8 = [cnt_ref[h, rows8, cols] for h in range(PEER_HEADS)]
            e18 = [e1_ref[h, rows8, cols] for h in range(PEER_HEADS)]
            for k in range(8):
                w = jnp.zeros((N_KEYS, LANES), _bf16)
                for h in range(PEER_HEADS):
                    hit = rank_ref[h, :, cols] < cnt8[h][k:k + 1].astype(_bf16)
                    w = w + jnp.where(hit, e2_ref[h, :, cols], jnp.zeros((), _bf16)) * e18[h][k:k + 1].astype(_bf16)
                rows = slice((g * 8 + k) * N_KEYS, (g * 8 + k + 1) * N_KEYS)
                p_ref[rows, cols] = _gelu2(a_ref[rows, cols]).astype(_bf16) * w

    def run(parity, stage1, stage2, stage3):
        cur, prev = parity, 1 - parity

        for hf in range(2):
            off = hf * half
            if stage1:
                a_refs[cur][:, pl.ds(off, half)] = _dot_nt(u_ref[...], h2_ref[pl.ds(off, half), :])
            if stage2:
                for c in range(half // LANES):
                    gate_chunk(off + c * LANES, a_refs[prev], p_refs[prev])
            if stage3:
                acc_ref[:, pl.ds(off, half)] += _dot(vt_ref[...], p_refs[cur][:, pl.ds(off, half)])

    @pl.when(e == 0)
    def _select():
        q = _dot(h2_ref[...], wq_ref[...]).astype(_bf16)
        for h in range(PEER_HEADS):
            base = h * 2 * PEER_HALF
            rows = slice(h * N_KEYS, (h + 1) * N_KEYS)
            a1_ref[rows, :] = _dot_nt(k1_ref[...], q[:, base:base + PEER_HALF])
            s2_ref[h] = _dot_nt(k2_ref[...], q[:, base + PEER_HALF:base + 2 * PEER_HALF])

        def body(it, carry):
            h = it // nchunk
            c = pl.multiple_of((it % nchunk) * LANES, LANES)
            r = pl.multiple_of(h * N_KEYS, N_KEYS)
            rank2, cnt, e1, e2 = _peer_select(a1_ref[pl.ds(r, N_KEYS), pl.ds(c, LANES)],
                                              s2_ref[h, :, pl.ds(c, LANES)])
            rank_ref[h, :, pl.ds(c, LANES)] = rank2.astype(_bf16)
            cnt_ref[h, :, pl.ds(c, LANES)] = cnt
            e1_ref[h, :, pl.ds(c, LANES)] = 0.5 * e1
            e2_ref[h, :, pl.ds(c, LANES)] = e2.astype(_bf16)
            return carry

        lax.fori_loop(0, PEER_HEADS * nchunk, body, 0)
        acc_ref[...] = jnp.zeros_like(acc_ref)
        run(0, True, False, False)

    @pl.when(e == 1)
    def _fill():
        run(1, True, True, False)

    for parity in range(2):
        @pl.when((e >= 2) & (e < n_tiles) & (e % 2 == parity))
        def _steady():
            run(parity, True, True, True)

    @pl.when(e == n_tiles)
    def _drain():
        run(n_tiles % 2, False, True, True)

    @pl.when(e == n_tiles + 1)
    def _finish():
        run((n_tiles + 1) % 2, False, False, True)
        mod = mod_ref[0]
        g2 = mod[:, 5 * D:6 * D]
        ff = acc_ref[...].T
        y_ref[...] = _ln_rows(alpha * x1_ref[...] + g2 * ff) * g_ref[...] + b_ref[...]


def _peer_call(h2, x1, mod, seq, wq, k1, k2, u, vt, g, b, alpha):
    n, d = x1.shape
    t = PEER_T
    eb = PEER_EB
    n_exp = u.shape[0]
    nblk_seq = seq // t
    row = lambda i, e: (i, 0)
    const = lambda i, e: (0, 0)
    n_tiles = n_exp // eb
    assert eb == PEER_HEADS * N_KEYS and n_tiles >= 2
    sel32 = pltpu.VMEM((PEER_HEADS, N_KEYS, t), _f32)
    sel16 = pltpu.VMEM((PEER_HEADS, N_KEYS, t), _bf16)
    return pl.pallas_call(
        functools.partial(_peer_kernel, alpha=alpha, d_model=d, n_tiles=n_tiles),
        grid=(n // t, n_tiles + 2),
        in_specs=[pl.BlockSpec((t, d), row), pl.BlockSpec((t, d), row),
                  pl.BlockSpec((1, 1, mod.shape[-1]), lambda i, e: (i // nblk_seq, 0, 0)),
                  pl.BlockSpec(wq.shape, const), pl.BlockSpec(k1.shape, const), pl.BlockSpec(k2.shape, const),
                  pl.BlockSpec((eb, d), lambda i, e: (jnp.minimum(e, n_tiles - 1), 0)),
                  pl.BlockSpec((d, eb), lambda i, e: (0, jnp.clip(e - 2, 0, n_tiles - 1))),
                  pl.BlockSpec((1, d), const), pl.BlockSpec((1, d), const)],
        out_specs=pl.BlockSpec((t, d), row),
        out_shape=jax.ShapeDtypeStruct((n, d), _f32),
        scratch_shapes=[sel32, sel16, sel16, sel32, sel32,
                        pltpu.VMEM((eb, t), _f32), pltpu.VMEM((eb, t), _f32),
                        pltpu.VMEM((eb, t), _bf16), pltpu.VMEM((eb, t), _bf16), pltpu.VMEM((d, t), _f32)],
        compiler_params=pltpu.CompilerParams(dimension_semantics=("parallel", "arbitrary"),
                                             vmem_limit_bytes=VMEM_LIMIT),
        name="peer",
    )(h2, x1, mod, wq, k1, k2, u, vt, g, b)


def _rope_tables(seq, head_dim):
    m = head_dim // 4
    lane = np.arange(LANES)
    d = lane % head_dim
    use_col = (d // (2 * m)) == 1
    within = d % (2 * m)
    second = within // m
    freqs = ROPE_BASE ** (-np.arange(m, dtype=np.float64) / m)
    f = freqs[within % m]
    tpos = np.arange(seq)
    pos = np.where(use_col[None, :], (tpos % GRID_W)[:, None], (tpos // GRID_W)[:, None]).astype(np.float64)
    ang = pos * f[None, :]
    sign = np.where(second == 0, -1.0, 1.0)
    return jnp.asarray(np.cos(ang), _f32), jnp.asarray(np.sin(ang) * sign[None, :], _f32)


def _layer_weights(l, w_in, attn_q_norm, attn_k_norm, mla_q_norm, mla_kv_norm, w_uq, w_ukv, gmlp_ws, gmlp_b, w_o):
    offs = np.cumsum([0, QA_W, KA_W, KA_W, M_Q_RANK, M_KV_RANK, M_ROPE, CW, CW])
    qa_perm = np.concatenate([np.arange(HEAD_DIM) + HEAD_DIM * (j + 4 * half)
                              for j in range(A_HEADS // 2) for half in range(2)])
    cols = np.concatenate([qa_perm, np.arange(offs[1], offs[5]),
                           np.tile(np.arange(offs[5], offs[6]), M_HEADS), np.arange(offs[6], offs[8])])
    qd = M_NOPE + M_ROPE
    uq_cols = np.concatenate([np.arange(M_NOPE) + h * qd for h in range(M_HEADS)]
                             + [np.arange(M_ROPE) + h * qd + M_NOPE for h in range(M_HEADS)])
    kd = M_NOPE + M_V
    ukv_cols = np.concatenate([np.arange(M_NOPE) + h * kd for h in range(M_HEADS)]
                              + [np.arange(M_V) + h * kd + M_NOPE for h in range(M_HEADS)])
    wo_rows = np.concatenate([qa_perm, np.arange(QA_W, w_o.shape[1])])
    seg = np.arange(QA_W) // HEAD_DIM
    return {
        'win': w_in[l][:, cols].astype(_bf16),
        'aqn': jnp.tile(attn_q_norm[l], A_HEADS)[None, :],
        'akn': jnp.tile(attn_k_norm[l], A_KV)[None, :],
        'mqn': mla_q_norm[l][None, :],
        'mkvn': mla_kv_norm[l][None, :],
        'wuq': w_uq[l][:, uq_cols].astype(_bf16),
        'wukv': w_ukv[l][:, ukv_cols].astype(_bf16),
        'ws': gmlp_ws[l].astype(_bf16),
        'bsf': jnp.repeat(gmlp_b[l].T, C_DIM, axis=1),
        'bd': jnp.asarray(seg[:, None] == seg[None, :], _bf16),
        'wo': w_o[l][wo_rows, :].astype(_bf16),
    }


def kernel(x_prompt, x_sample, cache_attn_k, cache_attn_v, cache_mla_ckv, cache_mla_krope, c, c_ctx, w_mod, b_mod, w_in, attn_q_norm, attn_k_norm, mla_q_norm, mla_kv_norm, w_uq, w_ukv, gmlp_ws, gmlp_b, w_o, ln1_g, ln1_b, ln2_g, ln2_b, peer_wq, peer_k1, peer_k2, peer_u, peer_v):
    batch, seq, d = x_prompt.shape
    dec_batch, dec_seq, _ = x_sample.shape
    depth = w_in.shape[0]
    past = cache_attn_k.shape[2]
    alpha = (2.0 * depth) ** 0.25

    n_rows = 8 * ((1 + dec_batch + 7) // 8)
    c_all = jnp.zeros((n_rows, d), _f32).at[0].set(c_ctx).at[1:1 + dec_batch].set(c)
    mod_all = _mod_call(c_all, w_mod, b_mod)

    rope = _rope_tables(dec_seq, HEAD_DIM) + _rope_tables(dec_seq, M_ROPE)
    xp = x_prompt.reshape(batch * seq, d)
    xs = x_sample.reshape(dec_batch * dec_seq, d)
    new = []
    for l in range(depth):
        wts = _layer_weights(l, w_in, attn_q_norm, attn_k_norm, mla_q_norm, mla_kv_norm, w_uq, w_ukv,
                             gmlp_ws, gmlp_b, w_o)
        peer_w = (peer_wq[l].astype(_bf16), peer_k1[l].astype(_bf16), peer_k2[l].astype(_bf16),
                  peer_u[l].astype(_bf16), peer_v[l].T.astype(_bf16))
        ln1 = (ln1_g[l][None, :], ln1_b[l][None, :])
        ln2 = (ln2_g[l][None, :], ln2_b[l][None, :])
        mod_ctx = mod_all[l, 0:1][:, None, :]
        mod_lat = mod_all[l, 1:1 + dec_batch][:, None, :]
        cache = (cache_attn_k[:, l].reshape(dec_batch, past, KA_W),
                 cache_attn_v[:, l].reshape(dec_batch, past, KA_W),
                 cache_mla_ckv[:, l],
                 jnp.tile(cache_mla_krope[:, l], (1, 1, M_HEADS)))

        pre = _pre_call(xp, mod_ctx, batch * seq, wts, None, latent=False)
        new.append(pre[7:])
        oa, om = _attn_call(pre[0:6], seq, None, None, latent=False)
        x1, h2 = _post_call(oa, om, pre[6], xp, mod_ctx, batch * seq, wts['wo'], *ln1, alpha)
        xp = _peer_call(h2, x1, mod_ctx, batch * seq, *peer_w, *ln2, alpha)

        pre = _pre_call(xs, mod_lat, dec_seq, wts, rope, latent=True)
        oa, om = _attn_call(pre[0:6], dec_seq, cache, wts['wukv'], latent=True)
        x1, h2 = _post_call(oa, om, pre[6], xs, mod_lat, dec_seq, wts['wo'], *ln1, alpha)
        xs = _peer_call(h2, x1, mod_lat, dec_seq, *peer_w, *ln2, alpha)

    def stack(i, tail):
        return jnp.stack([new[l][i].reshape((batch, seq) + tail) for l in range(depth)], axis=1)

    return (xp.reshape(batch, seq, d), xs.reshape(dec_batch, dec_seq, d),
            stack(0, (A_KV, HEAD_DIM)), stack(1, (A_KV, HEAD_DIM)), stack(2, (M_KV_RANK,)), stack(3, (M_ROPE,)))
```

```python
import functools
import math

import numpy as np
import jax
import jax.numpy as jnp
from jax import lax
from jax.experimental import pallas as pl
from jax.experimental.pallas import tpu as pltpu

A_HEADS = 8
A_KV = 2
HEAD_DIM = 64
M_HEADS = 4
M_Q_RANK = 256
M_KV_RANK = 128
M_NOPE = 64
M_ROPE = 32
M_V = 64
C_GROUPS = 4
C_DIM = 64
C_CHUNK = 128
PEER_HEADS = 8
N_KEYS = 128
PEER_HALF = 128
PEER_TOPK = 16
GRID_W = 64
EPS = 1e-6
ROPE_BASE = 10000.0

LANES = 128
ROWS_B = 16
NEG_BIG = -3.0e38

QA_W = A_HEADS * HEAD_DIM
KA_W = A_KV * HEAD_DIM
CQ_OFF = QA_W + 2 * KA_W
CKV_OFF = CQ_OFF + M_Q_RANK
KR_OFF = CKV_OFF + M_KV_RANK
UC_OFF = KR_OFF + M_HEADS * M_ROPE
CW = C_GROUPS * C_DIM
VC_OFF = UC_OFF + CW
IN_WP = VC_OFF + CW

PRE_TB = 256
ATT_QB = 256
PEER_T = 512
PEER_EB = 1024
VMEM_LIMIT = 56 * 1024 * 1024

_bf16 = jnp.bfloat16
_f32 = jnp.float32


def _dot(a, b):
    return jnp.dot(a, b, preferred_element_type=_f32)


def _dot_nt(a, b):
    return lax.dot_general(a, b, (((1,), (1,)), ((), ())), preferred_element_type=_f32)


def _split(x):
    hi = x.astype(_bf16)
    lo = (x - hi.astype(_f32)).astype(_bf16)
    return hi, lo


def _segsum(x, ones_bd):
    hi, lo = _split(x)
    return _dot(hi, ones_bd) + _dot(lo, ones_bd)


def _ln_rows(x):
    mu = jnp.mean(x, axis=-1, keepdims=True)
    d = x - mu
    var = jnp.mean(d * d, axis=-1, keepdims=True)
    return d * lax.rsqrt(var + EPS)


def _rot(x, cos, sin, half):
    lane = lax.broadcasted_iota(jnp.int32, x.shape, 1)
    up = pltpu.roll(x, LANES - half, axis=1)
    dn = pltpu.roll(x, half, axis=1)
    partner = jnp.where((lane & half) == 0, up, dn)
    return x * cos + partner * sin


def _mod_kernel(c_ref, w_ref, b_ref, o_ref):
    c = c_ref[...]
    a = c * (1.0 / (1.0 + jnp.exp(-c)))
    a_hi, a_lo = _split(a)
    w_hi, w_lo = _split(w_ref[0])
    o_ref[0] = _dot(a_hi, w_hi) + _dot(a_lo, w_hi) + _dot(a_hi, w_lo) + b_ref[0]


def _mod_call(c_all, w_mod, b_mod):
    depth, d, d6 = w_mod.shape
    rows = c_all.shape[0]
    tn = 1024
    return pl.pallas_call(
        _mod_kernel,
        grid=(depth, d6 // tn),
        in_specs=[pl.BlockSpec((rows, d), lambda l, j: (0, 0)),
                  pl.BlockSpec((1, d, tn), lambda l, j: (l, 0, j)),
                  pl.BlockSpec((1, 1, tn), lambda l, j: (l, 0, j))],
        out_specs=pl.BlockSpec((1, rows, tn), lambda l, j: (l, 0, j)),
        out_shape=jax.ShapeDtypeStruct((depth, rows, d6), _f32),
        compiler_params=pltpu.CompilerParams(dimension_semantics=("parallel", "parallel"),
                                             vmem_limit_bytes=VMEM_LIMIT),
        name="mod",
    )(c_all, w_mod, b_mod.reshape(depth, 1, d6))


def _pre_kernel(*refs, latent, d_model):
    if latent:
        (x_ref, mod_ref, win_ref, aqn_ref, akn_ref, mqn_ref, mkvn_ref, wuq_ref, wukv_ref, ws_ref, bsf_ref,
         bd_ref, cosa_ref, sina_ref, cosm_ref, sinm_ref,
         qa_ref, ka_ref, va_ref, qm_ref, kvm_ref, kr_ref, oc_ref) = refs
    else:
        (x_ref, mod_ref, win_ref, aqn_ref, akn_ref, mqn_ref, mkvn_ref, wuq_ref, wukv_ref, ws_ref, bsf_ref,
         bd_ref,
         qa_ref, ka_ref, va_ref, qm_ref, kvm_ref, kr_ref, oc_ref,
         nk_ref, nv_ref, nckv_ref, nkr_ref) = refs
    D = d_model
    x = x_ref[...]
    mod = mod_ref[0]
    sh1 = mod[:, 0:D]
    sc1 = mod[:, D:2 * D]
    h = _ln_rows(x) * (1.0 + sc1) + sh1
    proj = _dot(h.astype(_bf16), win_ref[...])

    qa = proj[:, 0:QA_W]
    ms = _segsum(qa * qa, bd_ref[...]) * (1.0 / HEAD_DIM)
    qa = qa * lax.rsqrt(ms + EPS) * aqn_ref[...]
    ka = proj[:, QA_W:QA_W + KA_W]
    ms = _segsum(ka * ka, bd_ref[0:KA_W, 0:KA_W]) * (1.0 / HEAD_DIM)
    ka = ka * lax.rsqrt(ms + EPS) * akn_ref[...]
    va = proj[:, QA_W + KA_W:CQ_OFF]
    if not latent:
        nk_ref[...] = ka
        nv_ref[...] = va
    for j in range(QA_W // LANES):
        slab = qa[:, j * LANES:(j + 1) * LANES]
        if latent:
            slab = _rot(slab, cosa_ref[...], sina_ref[...], HEAD_DIM // 4)
        qa_ref[:, j * LANES:(j + 1) * LANES] = (slab * (1.0 / math.sqrt(HEAD_DIM))).astype(_bf16)
    if latent:
        ka = _rot(ka, cosa_ref[...], sina_ref[...], HEAD_DIM // 4)
    ka_ref[...] = ka.astype(_bf16)
    va_ref[...] = va.astype(_bf16)

    cq = proj[:, CQ_OFF:CKV_OFF]
    cq = cq * lax.rsqrt(jnp.mean(cq * cq, axis=-1, keepdims=True) + EPS) * mqn_ref[...]
    qm = _dot(cq.astype(_bf16), wuq_ref[...])
    nope_w = M_HEADS * M_NOPE
    qscale = 1.0 / math.sqrt(M_NOPE + M_ROPE)
    qm_ref[:, 0:nope_w] = (qm[:, 0:nope_w] * qscale).astype(_bf16)
    qr = qm[:, nope_w:]
    if latent:
        qr = _rot(qr, cosm_ref[...], sinm_ref[...], M_ROPE // 4)
    qm_ref[:, nope_w:] = (qr * qscale).astype(_bf16)
    ckv = proj[:, CKV_OFF:KR_OFF]
    ckv = ckv * lax.rsqrt(jnp.mean(ckv * ckv, axis=-1, keepdims=True) + EPS) * mkvn_ref[...]
    kvm_ref[...] = _dot(ckv.astype(_bf16), wukv_ref[...]).astype(_bf16)
    kr = proj[:, KR_OFF:UC_OFF]
    if not latent:
        nckv_ref[...] = ckv
        nkr_ref[...] = kr[:, 0:M_ROPE]
    else:
        kr = _rot(kr, cosm_ref[...], sinm_ref[...], M_ROPE // 4)
    kr_ref[...] = kr.astype(_bf16)

    uc = proj[:, UC_OFF:VC_OFF]
    vc = proj[:, VC_OFF:IN_WP]
    bd_c = bd_ref[0:CW, 0:CW]
    mu = _segsum(vc, bd_c) * (1.0 / C_DIM)
    dv = vc - mu
    var = _segsum(dv * dv, bd_c) * (1.0 / C_DIM)
    vg = (dv * lax.rsqrt(var + EPS)).astype(_bf16)
    lane_c = lax.broadcasted_iota(jnp.int32, (C_CHUNK, CW), 1)
    for r in range(x.shape[0] // C_CHUNK):
        rows = slice(r * C_CHUNK, (r + 1) * C_CHUNK)
        vgc = vg[rows]
        mixed = bsf_ref[...]
        for g in range(C_GROUPS):
            mg = _dot(ws_ref[g], vgc)
            mixed = mixed + jnp.where(lane_c // C_DIM == g, mg, 0.0)
        oc_ref[rows, :] = (uc[rows] * mixed).astype(_bf16)


def _pre_call(x2, mod, seq, wts, rope, latent):
    n, d = x2.shape
    tb = PRE_TB
    nblk_seq = seq // tb
    row = lambda i: (i, 0)
    const = lambda i: (0, 0)
    in_specs = [pl.BlockSpec((tb, d), row),
                pl.BlockSpec((1, 1, mod.shape[-1]), lambda i: (i // nblk_seq, 0, 0)),
                pl.BlockSpec(wts['win'].shape, const),
                pl.BlockSpec((1, QA_W), const), pl.BlockSpec((1, KA_W), const),
                pl.BlockSpec((1, M_Q_RANK), const), pl.BlockSpec((1, M_KV_RANK), const),
                pl.BlockSpec(wts['wuq'].shape, const), pl.BlockSpec(wts['wukv'].shape, const),
                pl.BlockSpec(wts['ws'].shape, lambda i: (0, 0, 0)),
                pl.BlockSpec((C_CHUNK, CW), const),
                pl.BlockSpec((QA_W, QA_W), const)]
    args = [x2, mod, wts['win'], wts['aqn'], wts['akn'], wts['mqn'], wts['mkvn'], wts['wuq'], wts['wukv'],
            wts['ws'], wts['bsf'], wts['bd']]
    if latent:
        pos = lambda i: (i % nblk_seq, 0)
        in_specs += [pl.BlockSpec((tb, LANES), pos)] * 4
        args += list(rope)
    widths = [QA_W, KA_W, KA_W, M_HEADS * (M_NOPE + M_ROPE), M_HEADS * (M_NOPE + M_V), M_HEADS * M_ROPE, CW]
    out_specs = [pl.BlockSpec((tb, w), row) for w in widths]
    out_shape = [jax.ShapeDtypeStruct((n, w), _bf16) for w in widths]
    if not latent:
        for w in (KA_W, KA_W, M_KV_RANK, M_ROPE):
            out_specs.append(pl.BlockSpec((tb, w), row))
            out_shape.append(jax.ShapeDtypeStruct((n, w), _f32))
    return pl.pallas_call(
        functools.partial(_pre_kernel, latent=latent, d_model=d),
        grid=(n // tb,),
        in_specs=in_specs, out_specs=out_specs, out_shape=out_shape,
        compiler_params=pltpu.CompilerParams(dimension_semantics=("parallel",), vmem_limit_bytes=VMEM_LIMIT),
        name="pre_lat" if latent else "pre_ctx",
    )(*args)


def _softmax_pv(q, keys, vals):
    s = [_dot_nt(q, k) for k in keys]
    m = s[0].max(axis=-1, keepdims=True)
    for si in s[1:]:
        m = jnp.maximum(m, si.max(axis=-1, keepdims=True))
    den = None
    o = None
    for si, v in zip(s, vals):
        e = jnp.exp(si - m)
        d = e.sum(axis=-1, keepdims=True)
        den = d if den is None else den + d
        pv = _dot(e.astype(_bf16), v)
        o = pv if o is None else o + pv
    return o * (1.0 / den)


def _attn_kernel(*refs, latent):
    if latent:
        (qa_ref, qm_ref, ka_ref, va_ref, kvm_ref, kr_ref, ck_ref, cv_ref, cckv_ref, ckr_ref, wukv_ref,
         oa_ref, om_ref) = refs
    else:
        qa_ref, qm_ref, ka_ref, va_ref, kvm_ref, kr_ref, oa_ref, om_ref = refs
    qb = qa_ref.shape[0]
    nope_w = M_HEADS * M_NOPE

    keys_a = [ka_ref[...]]
    vals_a = [va_ref[...]]
    if latent:
        keys_a = [ck_ref[0].astype(_bf16)] + keys_a
        vals_a = [cv_ref[0].astype(_bf16)] + vals_a
    lane = lax.broadcasted_iota(jnp.int32, (qb, LANES), 1)
    low = lane < HEAD_DIM
    for j in range(QA_W // LANES):
        slab = qa_ref[:, j * LANES:(j + 1) * LANES]
        q2 = jnp.concatenate([jnp.where(low, slab, jnp.zeros_like(slab)),
                              jnp.where(low, jnp.zeros_like(slab), slab)], axis=0)
        o2 = _softmax_pv(q2, keys_a, vals_a)
        oa_ref[:, j * LANES:(j + 1) * LANES] = jnp.where(low, o2[0:qb], o2[qb:]).astype(_bf16)

    kn = [kvm_ref[:, 0:nope_w]]
    vm = [kvm_ref[:, nope_w:]]
    krs = [kr_ref[...]]
    if latent:
        kvc = _dot(cckv_ref[0].astype(_bf16), wukv_ref[...]).astype(_bf16)
        kn = [kvc[:, 0:nope_w]] + kn
        vm = [kvc[:, nope_w:]] + vm
        krs = [ckr_ref[0].astype(_bf16)] + krs
    keys_m = [jnp.concatenate([a, b], axis=1) for a, b in zip(kn, krs)]
    qm = qm_ref[...]
    lane_q = lax.broadcasted_iota(jnp.int32, qm.shape, 1)
    head_q = jnp.where(lane_q < nope_w, lane_q // M_NOPE, (lane_q - nope_w) // M_ROPE)
    lane_o = lax.broadcasted_iota(jnp.int32, (qb, nope_w), 1)
    om = jnp.zeros((qb, nope_w), _f32)
    for p in range(M_HEADS // 2):
        q2 = jnp.concatenate([jnp.where(head_q == 2 * p + t, qm, jnp.zeros_like(qm)) for t in range(2)], axis=0)
        o2 = _softmax_pv(q2, keys_m, vm)
        for t in range(2):
            om = jnp.where(lane_o // M_V == 2 * p + t, o2[t * qb:(t + 1) * qb], om)
    om_ref[...] = om.astype(_bf16)


def _attn_call(pre, seq, cache, wukv, latent):
    qa, ka, va, qm, kvm, kr = pre
    n = qa.shape[0]
    nb = n // seq
    qb = ATT_QB
    nq = seq // qb
    qrow = lambda b, i: (b * nq + i, 0)
    krow = lambda b, i: (b, 0)
    in_specs = [pl.BlockSpec((qb, qa.shape[1]), qrow), pl.BlockSpec((qb, qm.shape[1]), qrow),
                pl.BlockSpec((seq, ka.shape[1]), krow), pl.BlockSpec((seq, va.shape[1]), krow),
                pl.BlockSpec((seq, kvm.shape[1]), krow), pl.BlockSpec((seq, kr.shape[1]), krow)]
    args = [qa, qm, ka, va, kvm, kr]
    if latent:
        for c in cache:
            in_specs.append(pl.BlockSpec((1,) + c.shape[1:], lambda b, i: (b, 0, 0)))
            args.append(c)
        in_specs.append(pl.BlockSpec(wukv.shape, lambda b, i: (0, 0)))
        args.append(wukv)
    return pl.pallas_call(
        functools.partial(_attn_kernel, latent=latent),
        grid=(nb, nq),
        in_specs=in_specs,
        out_specs=[pl.BlockSpec((qb, QA_W), qrow), pl.BlockSpec((qb, M_HEADS * M_V), qrow)],
        out_shape=[jax.ShapeDtypeStruct((n, QA_W), _bf16), jax.ShapeDtypeStruct((n, M_HEADS * M_V), _bf16)],
        compiler_params=pltpu.CompilerParams(dimension_semantics=("parallel", "parallel"),
                                             vmem_limit_bytes=VMEM_LIMIT),
        name="attn_lat" if latent else "attn_ctx",
    )(*args)


def _post_kernel(oa_ref, om_ref, oc_ref, x_ref, mod_ref, wo_ref, g_ref, b_ref, x1_ref, h2_ref, *, alpha, d_model):
    D = d_model
    a_w = oa_ref.shape[1]
    m_w = om_ref.shape[1]
    mix = (_dot(oa_ref[...], wo_ref[0:a_w, :]) + _dot(om_ref[...], wo_ref[a_w:a_w + m_w, :])
           + _dot(oc_ref[...], wo_ref[a_w + m_w:, :]))
    mod = mod_ref[0]
    g1 = mod[:, 2 * D:3 * D]
    sh2 = mod[:, 3 * D:4 * D]
    sc2 = mod[:, 4 * D:5 * D]
    x1 = _ln_rows(alpha * x_ref[...] + g1 * mix) * g_ref[...] + b_ref[...]
    x1_ref[...] = x1
    h2_ref[...] = (_ln_rows(x1) * (1.0 + sc2) + sh2).astype(_bf16)


def _post_call(oa, om, oc, x2, mod, seq, wo, g, b, alpha):
    n, d = x2.shape
    tb = PRE_TB
    nblk_seq = seq // tb
    row = lambda i: (i, 0)
    const = lambda i: (0, 0)
    return pl.pallas_call(
        functools.partial(_post_kernel, alpha=alpha, d_model=d),
        grid=(n // tb,),
        in_specs=[pl.BlockSpec((tb, oa.shape[1]), row), pl.BlockSpec((tb, om.shape[1]), row),
                  pl.BlockSpec((tb, oc.shape[1]), row), pl.BlockSpec((tb, d), row),
                  pl.BlockSpec((1, 1, mod.shape[-1]), lambda i: (i // nblk_seq, 0, 0)),
                  pl.BlockSpec(wo.shape, const), pl.BlockSpec((1, d), const), pl.BlockSpec((1, d), const)],
        out_specs=[pl.BlockSpec((tb, d), row), pl.BlockSpec((tb, d), row)],
        out_shape=[jax.ShapeDtypeStruct((n, d), _f32), jax.ShapeDtypeStruct((n, d), _bf16)],
        compiler_params=pltpu.CompilerParams(dimension_semantics=("parallel",), vmem_limit_bytes=VMEM_LIMIT),
        name="post",
    )(oa, om, oc, x2, mod, wo, g, b)


def _extract_top(s, n, ranked=0):
    tops = []
    rank = jnp.full(s.shape, float(ranked), _f32)
    for j in range(n):
        m = jnp.max(s, axis=0, keepdims=True)
        tops.append(m)
        top = s == m
        if j < ranked:
            rank = jnp.where(top, float(j), rank)
        if j + 1 < n:
            s = jnp.where(top, NEG_BIG, s)
    return (tops, rank) if ranked else tops


def _stack_rows(rows):
    n = len(rows)
    rid = lax.broadcasted_iota(jnp.int32, (n, LANES), 0)
    out = jnp.broadcast_to(rows[0], (n, LANES))
    for j in range(1, n):
        out = jnp.where(rid == j, rows[j], out)
    return out


def _peer_select(s1, s2):
    k = PEER_TOPK
    v1 = _extract_top(s1, k + 1)
    v2, rank2 = _extract_top(s2, k + 1, ranked=k)
    col1 = _stack_rows(v1[0:k])
    col2 = _stack_rows(v2[0:k])
    pieces = [v1[0] + col2]
    for a in range(1, 8):
        pieces.append(v1[a] + col2[0:8])
    pieces.append(col1[8:16] + v2[0])
    tail = jnp.where(lax.broadcasted_iota(jnp.int32, (8, LANES), 0) == 0, v1[0] + v2[k],
                     jnp.where(lax.broadcasted_iota(jnp.int32, (8, LANES), 0) == 1, v1[k] + v2[0], NEG_BIG))
    pieces.append(tail)
    cand = jnp.concatenate(pieces, axis=0)
    tops = _extract_top(cand, k + 1)
    z = jnp.ones_like(tops[0])
    for j in range(1, k):
        z = z + jnp.exp(tops[j] - tops[0])
    tau = 0.5 * (tops[k - 1] + tops[k])
    cnt = jnp.zeros_like(s1)
    for b in range(8):
        cnt = cnt + jnp.where(s1 >= tau - v2[b], 1.0, 0.0)
    extra = jnp.zeros_like(tau)
    for b in range(8, k):
        extra = extra + jnp.where(v1[0] >= tau - v2[b], 1.0, 0.0)
    cnt = cnt + jnp.where(s1 >= v1[0], extra, 0.0)
    e1 = jnp.exp(s1 - v1[0]) * (1.0 / z)
    e2 = jnp.exp(s2 - v2[0])
    return rank2, cnt, e1, e2


def _gelu2(x):
    c = math.sqrt(2.0 / math.pi)
    return x * (1.0 + jnp.tanh(x * (c + (c * 0.044715) * (x * x))))


def _peer_kernel(h2_ref, x1_ref, mod_ref, wq_ref, k1_ref, k2_ref, u_ref, vt_ref, g_ref, b_ref, y_ref,
                 s1_ref, s2_ref, rank_ref, e2_ref, cnt_ref, e1_ref, cntb_ref, e1b_ref, p0_ref, p1_ref, acc_ref,
                 *, alpha, d_model, n_tiles):
    D = d_model
    e = pl.program_id(1)
    t = h2_ref.shape[0]
    nchunk = t // LANES
    slabs = u_ref.shape[0] // N_KEYS
    p_refs = (p0_ref, p1_ref)

    def spread_rows():
        rows8 = pl.ds(pl.multiple_of(e * slabs, 8), 8)
        for h in range(PEER_HEADS):
            c8 = cnt_ref[h, rows8, :]
            e8 = e1_ref[h, rows8, :]
            for k in range(slabs):
                cntb_ref[h, k] = jnp.broadcast_to(c8[k:k + 1], (ROWS_B, t)).astype(_bf16)
                e1b_ref[h, k] = jnp.broadcast_to(e8[k:k + 1], (ROWS_B, t)).astype(_bf16)

    def gate_slab(k, a, p_ref):
        rows = slice(k * N_KEYS, (k + 1) * N_KEYS)
        for c in range(nchunk):
            cols = pl.ds(c * LANES, LANES)
            w = jnp.zeros((N_KEYS // ROWS_B, ROWS_B, LANES), _bf16)
            for h in range(PEER_HEADS):
                hit = rank_ref[h, :, :, cols] < cntb_ref[h, k, :, cols]
                w = w + jnp.where(hit, e2_ref[h, :, :, cols], jnp.zeros((), _bf16)) * e1b_ref[h, k, :, cols]
            ac = a[:, c * LANES:(c + 1) * LANES]
            p_ref[rows, cols] = _gelu2(ac).astype(_bf16) * w.reshape(N_KEYS, LANES)

    def run(parity, front, back):
        cur, prev = parity, 1 - parity
        units = 4
        per = slabs // units
        m1 = per * N_KEYS
        m3 = vt_ref.shape[0] // units
        if front:
            spread_rows()
        for j in range(units):
            if front:
                a = _dot_nt(u_ref[j * m1:(j + 1) * m1, :], h2_ref[...])
                for k in range(per):
                    gate_slab(j * per + k, a[k * N_KEYS:(k + 1) * N_KEYS], p_refs[cur])
            if back:
                acc_ref[j * m3:(j + 1) * m3, :] += _dot(vt_ref[j * m3:(j + 1) * m3, :], p_refs[prev][...])

    @pl.when(e == 0)
    def _select():
        q = _dot(h2_ref[...], wq_ref[...]).astype(_bf16)
        for h in range(PEER_HEADS):
            base = h * 2 * PEER_HALF
            s1_ref[h] = _dot_nt(k1_ref[...], q[:, base:base + PEER_HALF])
            s2_ref[h] = _dot_nt(k2_ref[...], q[:, base + PEER_HALF:base + 2 * PEER_HALF])

        def body(it, carry):
            h = it // nchunk
            c = pl.multiple_of((it % nchunk) * LANES, LANES)
            rank2, cnt, e1, e2 = _peer_select(s1_ref[h, :, pl.ds(c, LANES)], s2_ref[h, :, pl.ds(c, LANES)])
            tiles = (N_KEYS // ROWS_B, ROWS_B, LANES)
            rank_ref[h, :, :, pl.ds(c, LANES)] = rank2.astype(_bf16).reshape(tiles)
            cnt_ref[h, :, pl.ds(c, LANES)] = cnt
            e1_ref[h, :, pl.ds(c, LANES)] = 0.5 * e1
            e2_ref[h, :, :, pl.ds(c, LANES)] = e2.astype(_bf16).reshape(tiles)
            return carry

        lax.fori_loop(0, PEER_HEADS * nchunk, body, 0)
        acc_ref[...] = jnp.zeros_like(acc_ref)
        run(0, True, False)

    for parity in range(2):
        @pl.when((e >= 1) & (e < n_tiles) & (e % 2 == parity))
        def _steady():
            run(parity, True, True)

    @pl.when(e == n_tiles)
    def _finish():
        run(n_tiles % 2, False, True)
        mod = mod_ref[0]
        g2 = mod[:, 5 * D:6 * D]
        ff = acc_ref[...].T
        y_ref[...] = _ln_rows(alpha * x1_ref[...] + g2 * ff) * g_ref[...] + b_ref[...]


def _peer_call(h2, x1, mod, seq, wq, k1, k2, u, vt, g, b, alpha):
    n, d = x1.shape
    t = PEER_T
    eb = PEER_EB
    n_exp = u.shape[0]
    nblk_seq = seq // t
    row = lambda i, e: (i, 0)
    const = lambda i, e: (0, 0)
    n_tiles = n_exp // eb
    assert eb == 8 * N_KEYS and n_tiles >= 2
    sel32 = pltpu.VMEM((PEER_HEADS, N_KEYS, t), _f32)
    sel16 = pltpu.VMEM((PEER_HEADS, N_KEYS // ROWS_B, ROWS_B, t), _bf16)
    rowb = pltpu.VMEM((PEER_HEADS, eb // N_KEYS, ROWS_B, t), _bf16)
    return pl.pallas_call(
        functools.partial(_peer_kernel, alpha=alpha, d_model=d, n_tiles=n_tiles),
        grid=(n // t, n_tiles + 1),
        in_specs=[pl.BlockSpec((t, d), row), pl.BlockSpec((t, d), row),
                  pl.BlockSpec((1, 1, mod.shape[-1]), lambda i, e: (i // nblk_seq, 0, 0)),
                  pl.BlockSpec(wq.shape, const), pl.BlockSpec(k1.shape, const), pl.BlockSpec(k2.shape, const),
                  pl.BlockSpec((eb, d), lambda i, e: (jnp.minimum(e, n_tiles - 1), 0)),
                  pl.BlockSpec((d, eb), lambda i, e: (0, jnp.maximum(e - 1, 0))),
                  pl.BlockSpec((1, d), const), pl.BlockSpec((1, d), const)],
        out_specs=pl.BlockSpec((t, d), row),
        out_shape=jax.ShapeDtypeStruct((n, d), _f32),
        scratch_shapes=[sel32, sel32, sel16, sel16, sel32, sel32, rowb, rowb,
                        pltpu.VMEM((eb, t), _bf16), pltpu.VMEM((eb, t), _bf16), pltpu.VMEM((d, t), _f32)],
        compiler_params=pltpu.CompilerParams(dimension_semantics=("parallel", "arbitrary"),
                                             vmem_limit_bytes=VMEM_LIMIT),
        name="peer",
    )(h2, x1, mod, wq, k1, k2, u, vt, g, b)


def _rope_tables(seq, head_dim):
    m = head_dim // 4
    lane = np.arange(LANES)
    d = lane % head_dim
    use_col = (d // (2 * m)) == 1
    within = d % (2 * m)
    second = within // m
    freqs = ROPE_BASE ** (-np.arange(m, dtype=np.float64) / m)
    f = freqs[within % m]
    tpos = np.arange(seq)
    pos = np.where(use_col[None, :], (tpos % GRID_W)[:, None], (tpos // GRID_W)[:, None]).astype(np.float64)
    ang = pos * f[None, :]
    sign = np.where(second == 0, -1.0, 1.0)
    return jnp.asarray(np.cos(ang), _f32), jnp.asarray(np.sin(ang) * sign[None, :], _f32)


def _layer_weights(l, w_in, attn_q_norm, attn_k_norm, mla_q_norm, mla_kv_norm, w_uq, w_ukv, gmlp_ws, gmlp_b, w_o):
    offs = np.cumsum([0, QA_W, KA_W, KA_W, M_Q_RANK, M_KV_RANK, M_ROPE, CW, CW])
    qa_perm = np.concatenate([np.arange(HEAD_DIM) + HEAD_DIM * (j + 4 * half)
                              for j in range(A_HEADS // 2) for half in range(2)])
    cols = np.concatenate([qa_perm, np.arange(offs[1], offs[5]),
                           np.tile(np.arange(offs[5], offs[6]), M_HEADS), np.arange(offs[6], offs[8])])
    qd = M_NOPE + M_ROPE
    uq_cols = np.concatenate([np.arange(M_NOPE) + h * qd for h in range(M_HEADS)]
                             + [np.arange(M_ROPE) + h * qd + M_NOPE for h in range(M_HEADS)])
    kd = M_NOPE + M_V
    ukv_cols = np.concatenate([np.arange(M_NOPE) + h * kd for h in range(M_HEADS)]
                              + [np.arange(M_V) + h * kd + M_NOPE for h in range(M_HEADS)])
    wo_rows = np.concatenate([qa_perm, np.arange(QA_W, w_o.shape[1])])
    seg = np.arange(QA_W) // HEAD_DIM
    return {
        'win': w_in[l][:, cols].astype(_bf16),
        'aqn': jnp.tile(attn_q_norm[l], A_HEADS)[None, :],
        'akn': jnp.tile(attn_k_norm[l], A_KV)[None, :],
        'mqn': mla_q_norm[l][None, :],
        'mkvn': mla_kv_norm[l][None, :],
        'wuq': w_uq[l][:, uq_cols].astype(_bf16),
        'wukv': w_ukv[l][:, ukv_cols].astype(_bf16),
        'ws': gmlp_ws[l].astype(_bf16),
        'bsf': jnp.repeat(gmlp_b[l].T, C_DIM, axis=1),
        'bd': jnp.asarray(seg[:, None] == seg[None, :], _bf16),
        'wo': w_o[l][wo_rows, :].astype(_bf16),
    }


def kernel(x_prompt, x_sample, cache_attn_k, cache_attn_v, cache_mla_ckv, cache_mla_krope, c, c_ctx, w_mod, b_mod, w_in, attn_q_norm, attn_k_norm, mla_q_norm, mla_kv_norm, w_uq, w_ukv, gmlp_ws, gmlp_b, w_o, ln1_g, ln1_b, ln2_g, ln2_b, peer_wq, peer_k1, peer_k2, peer_u, peer_v):
    batch, seq, d = x_prompt.shape
    dec_batch, dec_seq, _ = x_sample.shape
    depth = w_in.shape[0]
    past = cache_attn_k.shape[2]
    alpha = (2.0 * depth) ** 0.25

    n_rows = 8 * ((1 + dec_batch + 7) // 8)
    c_all = jnp.zeros((n_rows, d), _f32).at[0].set(c_ctx).at[1:1 + dec_batch].set(c)
    mod_all = _mod_call(c_all, w_mod, b_mod)

    rope = _rope_tables(dec_seq, HEAD_DIM) + _rope_tables(dec_seq, M_ROPE)
    xp = x_prompt.reshape(batch * seq, d)
    xs = x_sample.reshape(dec_batch * dec_seq, d)
    new = []
    for l in range(depth):
        wts = _layer_weights(l, w_in, attn_q_norm, attn_k_norm, mla_q_norm, mla_kv_norm, w_uq, w_ukv,
                             gmlp_ws, gmlp_b, w_o)
        peer_w = (peer_wq[l].astype(_bf16), peer_k1[l].astype(_bf16), peer_k2[l].astype(_bf16),
                  peer_u[l].astype(_bf16), peer_v[l].T.astype(_bf16))
        ln1 = (ln1_g[l][None, :], ln1_b[l][None, :])
        ln2 = (ln2_g[l][None, :], ln2_b[l][None, :])
        mod_ctx = mod_all[l, 0:1][:, None, :]
        mod_lat = mod_all[l, 1:1 + dec_batch][:, None, :]
        cache = (cache_attn_k[:, l].reshape(dec_batch, past, KA_W),
                 cache_attn_v[:, l].reshape(dec_batch, past, KA_W),
                 cache_mla_ckv[:, l],
                 jnp.tile(cache_mla_krope[:, l], (1, 1, M_HEADS)))

        pre = _pre_call(xp, mod_ctx, batch * seq, wts, None, latent=False)
        new.append(pre[7:])
        oa, om = _attn_call(pre[0:6], seq, None, None, latent=False)
        x1, h2 = _post_call(oa, om, pre[6], xp, mod_ctx, batch * seq, wts['wo'], *ln1, alpha)
        xp = _peer_call(h2, x1, mod_ctx, batch * seq, *peer_w, *ln2, alpha)

        pre = _pre_call(xs, mod_lat, dec_seq, wts, rope, latent=True)
        oa, om = _attn_call(pre[0:6], dec_seq, cache, wts['wukv'], latent=True)
        x1, h2 = _post_call(oa, om, pre[6], xs, mod_lat, dec_seq, wts['wo'], *ln1, alpha)
        xs = _peer_call(h2, x1, mod_lat, dec_seq, *peer_w, *ln2, alpha)

    def stack(i, tail):
        return jnp.stack([new[l][i].reshape((batch, seq) + tail) for l in range(depth)], axis=1)

    return (xp.reshape(batch, seq, d), xs.reshape(dec_batch, dec_seq, d),
            stack(0, (A_KV, HEAD_DIM)), stack(1, (A_KV, HEAD_DIM)), stack(2, (M_KV_RANK,)), stack(3, (M_ROPE,)))
```

```python
import functools
import math

import numpy as np
import jax
import jax.numpy as jnp
from jax import lax
from jax.experimental import pallas as pl
from jax.experimental.pallas import tpu as pltpu

A_HEADS = 8
A_KV = 2
HEAD_DIM = 64
M_HEADS = 4
M_Q_RANK = 256
M_KV_RANK = 128
M_NOPE = 64
M_ROPE = 32
M_V = 64
C_GROUPS = 4
C_DIM = 64
C_CHUNK = 128
PEER_HEADS = 8
N_KEYS = 128
PEER_HALF = 128
PEER_TOPK = 16
GRID_W = 64
EPS = 1e-6
ROPE_BASE = 10000.0

LANES = 128
ROWS_B = 16
NEG_BIG = -3.0e38

QA_W = A_HEADS * HEAD_DIM
KA_W = A_KV * HEAD_DIM
CQ_OFF = QA_W + 2 * KA_W
CKV_OFF = CQ_OFF + M_Q_RANK
KR_OFF = CKV_OFF + M_KV_RANK
UC_OFF = KR_OFF + M_HEADS * M_ROPE
CW = C_GROUPS * C_DIM
VC_OFF = UC_OFF + CW
IN_WP = VC_OFF + CW

PRE_TB = 256
ATT_QB = 256
PEER_T = 512
PEER_EB = 1024
VMEM_LIMIT = 56 * 1024 * 1024

_bf16 = jnp.bfloat16
_f32 = jnp.float32


def _dot(a, b):
    return jnp.dot(a, b, preferred_element_type=_f32)


def _dot_nt(a, b):
    return lax.dot_general(a, b, (((1,), (1,)), ((), ())), preferred_element_type=_f32)


def _split(x):
    hi = x.astype(_bf16)
    lo = (x - hi.astype(_f32)).astype(_bf16)
    return hi, lo


def _segsum(x, ones_bd):
    hi, lo = _split(x)
    return _dot(hi, ones_bd) + _dot(lo, ones_bd)


def _ln_rows(x):
    mu = jnp.mean(x, axis=-1, keepdims=True)
    d = x - mu
    var = jnp.mean(d * d, axis=-1, keepdims=True)
    return d * lax.rsqrt(var + EPS)


def _rot(x, cos, sin, half):
    lane = lax.broadcasted_iota(jnp.int32, x.shape, 1)
    up = pltpu.roll(x, LANES - half, axis=1)
    dn = pltpu.roll(x, half, axis=1)
    partner = jnp.where((lane & half) == 0, up, dn)
    return x * cos + partner * sin


def _mod_kernel(c_ref, w_ref, b_ref, o_ref):
    c = c_ref[...]
    a = c * (1.0 / (1.0 + jnp.exp(-c)))
    a_hi, a_lo = _split(a)
    w_hi, w_lo = _split(w_ref[0])
    o_ref[0] = _dot(a_hi, w_hi) + _dot(a_lo, w_hi) + _dot(a_hi, w_lo) + b_ref[0]


def _mod_call(c_all, w_mod, b_mod):
    depth, d, d6 = w_mod.shape
    rows = c_all.shape[0]
    tn = 1024
    return pl.pallas_call(
        _mod_kernel,
        grid=(depth, d6 // tn),
        in_specs=[pl.BlockSpec((rows, d), lambda l, j: (0, 0)),
                  pl.BlockSpec((1, d, tn), lambda l, j: (l, 0, j)),
                  pl.BlockSpec((1, 1, tn), lambda l, j: (l, 0, j))],
        out_specs=pl.BlockSpec((1, rows, tn), lambda l, j: (l, 0, j)),
        out_shape=jax.ShapeDtypeStruct((depth, rows, d6), _f32),
        compiler_params=pltpu.CompilerParams(dimension_semantics=("parallel", "parallel"),
                                             vmem_limit_bytes=VMEM_LIMIT),
        name="mod",
    )(c_all, w_mod, b_mod.reshape(depth, 1, d6))


def _pre_kernel(*refs, latent, d_model):
    if latent:
        (x_ref, mod_ref, win_ref, aqn_ref, akn_ref, mqn_ref, mkvn_ref, wuq_ref, wukv_ref, ws_ref, bsf_ref,
         bd_ref, cosa_ref, sina_ref, cosm_ref, sinm_ref,
         qa_ref, ka_ref, va_ref, qm_ref, kvm_ref, kr_ref, oc_ref) = refs
    else:
        (x_ref, mod_ref, win_ref, aqn_ref, akn_ref, mqn_ref, mkvn_ref, wuq_ref, wukv_ref, ws_ref, bsf_ref,
         bd_ref,
         qa_ref, ka_ref, va_ref, qm_ref, kvm_ref, kr_ref, oc_ref,
         nk_ref, nv_ref, nckv_ref, nkr_ref) = refs
    D = d_model
    x = x_ref[...]
    mod = mod_ref[0]
    sh1 = mod[:, 0:D]
    sc1 = mod[:, D:2 * D]
    h = _ln_rows(x) * (1.0 + sc1) + sh1
    proj = _dot(h.astype(_bf16), win_ref[...])

    qa = proj[:, 0:QA_W]
    ms = _segsum(qa * qa, bd_ref[...]) * (1.0 / HEAD_DIM)
    qa = qa * lax.rsqrt(ms + EPS) * aqn_ref[...]
    ka = proj[:, QA_W:QA_W + KA_W]
    ms = _segsum(ka * ka, bd_ref[0:KA_W, 0:KA_W]) * (1.0 / HEAD_DIM)
    ka = ka * lax.rsqrt(ms + EPS) * akn_ref[...]
    va = proj[:, QA_W + KA_W:CQ_OFF]
    if not latent:
        nk_ref[...] = ka
        nv_ref[...] = va
    for j in range(QA_W // LANES):
        slab = qa[:, j * LANES:(j + 1) * LANES]
        if latent:
            slab = _rot(slab, cosa_ref[...], sina_ref[...], HEAD_DIM // 4)
        qa_ref[:, j * LANES:(j + 1) * LANES] = (slab * (1.0 / math.sqrt(HEAD_DIM))).astype(_bf16)
    if latent:
        ka = _rot(ka, cosa_ref[...], sina_ref[...], HEAD_DIM // 4)
    ka_ref[...] = ka.astype(_bf16)
    va_ref[...] = va.astype(_bf16)

    cq = proj[:, CQ_OFF:CKV_OFF]
    cq = cq * lax.rsqrt(jnp.mean(cq * cq, axis=-1, keepdims=True) + EPS) * mqn_ref[...]
    qm = _dot(cq.astype(_bf16), wuq_ref[...])
    nope_w = M_HEADS * M_NOPE
    qscale = 1.0 / math.sqrt(M_NOPE + M_ROPE)
    qm_ref[:, 0:nope_w] = (qm[:, 0:nope_w] * qscale).astype(_bf16)
    qr = qm[:, nope_w:]
    if latent:
        qr = _rot(qr, cosm_ref[...], sinm_ref[...], M_ROPE // 4)
    qm_ref[:, nope_w:] = (qr * qscale).astype(_bf16)
    ckv = proj[:, CKV_OFF:KR_OFF]
    ckv = ckv * lax.rsqrt(jnp.mean(ckv * ckv, axis=-1, keepdims=True) + EPS) * mkvn_ref[...]
    kvm_ref[...] = _dot(ckv.astype(_bf16), wukv_ref[...]).astype(_bf16)
    kr = proj[:, KR_OFF:UC_OFF]
    if not latent:
        nckv_ref[...] = ckv
        nkr_ref[...] = kr[:, 0:M_ROPE]
    else:
        kr = _rot(kr, cosm_ref[...], sinm_ref[...], M_ROPE // 4)
    kr_ref[...] = kr.astype(_bf16)

    uc = proj[:, UC_OFF:VC_OFF]
    vc = proj[:, VC_OFF:IN_WP]
    bd_c = bd_ref[0:CW, 0:CW]
    mu = _segsum(vc, bd_c) * (1.0 / C_DIM)
    dv = vc - mu
    var = _segsum(dv * dv, bd_c) * (1.0 / C_DIM)
    vg = (dv * lax.rsqrt(var + EPS)).astype(_bf16)
    lane_c = lax.broadcasted_iota(jnp.int32, (C_CHUNK, CW), 1)
    for r in range(x.shape[0] // C_CHUNK):
        rows = slice(r * C_CHUNK, (r + 1) * C_CHUNK)
        vgc = vg[rows]
        mixed = bsf_ref[...]
        for g in range(C_GROUPS):
            mg = _dot(ws_ref[g], vgc)
            mixed = mixed + jnp.where(lane_c // C_DIM == g, mg, 0.0)
        oc_ref[rows, :] = (uc[rows] * mixed).astype(_bf16)


def _pre_call(x2, mod, seq, wts, rope, latent):
    n, d = x2.shape
    tb = PRE_TB
    nblk_seq = seq // tb
    row = lambda i: (i, 0)
    const = lambda i: (0, 0)
    in_specs = [pl.BlockSpec((tb, d), row),
                pl.BlockSpec((1, 1, mod.shape[-1]), lambda i: (i // nblk_seq, 0, 0)),
                pl.BlockSpec(wts['win'].shape, const),
                pl.BlockSpec((1, QA_W), const), pl.BlockSpec((1, KA_W), const),
                pl.BlockSpec((1, M_Q_RANK), const), pl.BlockSpec((1, M_KV_RANK), const),
                pl.BlockSpec(wts['wuq'].shape, const), pl.BlockSpec(wts['wukv'].shape, const),
                pl.BlockSpec(wts['ws'].shape, lambda i: (0, 0, 0)),
                pl.BlockSpec((C_CHUNK, CW), const),
                pl.BlockSpec((QA_W, QA_W), const)]
    args = [x2, mod, wts['win'], wts['aqn'], wts['akn'], wts['mqn'], wts['mkvn'], wts['wuq'], wts['wukv'],
            wts['ws'], wts['bsf'], wts['bd']]
    if latent:
        pos = lambda i: (i % nblk_seq, 0)
        in_specs += [pl.BlockSpec((tb, LANES), pos)] * 4
        args += list(rope)
    widths = [QA_W, KA_W, KA_W, M_HEADS * (M_NOPE + M_ROPE), M_HEADS * (M_NOPE + M_V), M_HEADS * M_ROPE, CW]
    out_specs = [pl.BlockSpec((tb, w), row) for w in widths]
    out_shape = [jax.ShapeDtypeStruct((n, w), _bf16) for w in widths]
    if not latent:
        for w in (KA_W, KA_W, M_KV_RANK, M_ROPE):
            out_specs.append(pl.BlockSpec((tb, w), row))
            out_shape.append(jax.ShapeDtypeStruct((n, w), _f32))
    return pl.pallas_call(
        functools.partial(_pre_kernel, latent=latent, d_model=d),
        grid=(n // tb,),
        in_specs=in_specs, out_specs=out_specs, out_shape=out_shape,
        compiler_params=pltpu.CompilerParams(dimension_semantics=("parallel",), vmem_limit_bytes=VMEM_LIMIT),
        name="pre_lat" if latent else "pre_ctx",
    )(*args)


def _softmax_pv(q, keys, vals):
    s = [_dot_nt(q, k) for k in keys]
    m = s[0].max(axis=-1, keepdims=True)
    for si in s[1:]:
        m = jnp.maximum(m, si.max(axis=-1, keepdims=True))
    den = None
    o = None
    for si, v in zip(s, vals):
        e = jnp.exp(si - m)
        d = e.sum(axis=-1, keepdims=True)
        den = d if den is None else den + d
        pv = _dot(e.astype(_bf16), v)
        o = pv if o is None else o + pv
    return o * (1.0 / den)


def _attn_kernel(*refs, latent):
    if latent:
        (qa_ref, qm_ref, ka_ref, va_ref, kvm_ref, kr_ref, ck_ref, cv_ref, cckv_ref, ckr_ref, wukv_ref,
         oa_ref, om_ref) = refs
    else:
        qa_ref, qm_ref, ka_ref, va_ref, kvm_ref, kr_ref, oa_ref, om_ref = refs
    qb = qa_ref.shape[0]
    nope_w = M_HEADS * M_NOPE

    keys_a = [ka_ref[...]]
    vals_a = [va_ref[...]]
    if latent:
        keys_a = [ck_ref[0].astype(_bf16)] + keys_a
        vals_a = [cv_ref[0].astype(_bf16)] + vals_a
    lane = lax.broadcasted_iota(jnp.int32, (qb, LANES), 1)
    low = lane < HEAD_DIM
    for j in range(QA_W // LANES):
        slab = qa_ref[:, j * LANES:(j + 1) * LANES]
        q2 = jnp.concatenate([jnp.where(low, slab, jnp.zeros_like(slab)),
                              jnp.where(low, jnp.zeros_like(slab), slab)], axis=0)
        o2 = _softmax_pv(q2, keys_a, vals_a)
        oa_ref[:, j * LANES:(j + 1) * LANES] = jnp.where(low, o2[0:qb], o2[qb:]).astype(_bf16)

    kn = [kvm_ref[:, 0:nope_w]]
    vm = [kvm_ref[:, nope_w:]]
    krs = [kr_ref[...]]
    if latent:
        kvc = _dot(cckv_ref[0].astype(_bf16), wukv_ref[...]).astype(_bf16)
        kn = [kvc[:, 0:nope_w]] + kn
        vm = [kvc[:, nope_w:]] + vm
        krs = [ckr_ref[0].astype(_bf16)] + krs
    keys_m = [jnp.concatenate([a, b], axis=1) for a, b in zip(kn, krs)]
    qm = qm_ref[...]
    lane_q = lax.broadcasted_iota(jnp.int32, qm.shape, 1)
    head_q = jnp.where(lane_q < nope_w, lane_q // M_NOPE, (lane_q - nope_w) // M_ROPE)
    lane_o = lax.broadcasted_iota(jnp.int32, (qb, nope_w), 1)
    om = jnp.zeros((qb, nope_w), _f32)
    for p in range(M_HEADS // 2):
        q2 = jnp.concatenate([jnp.where(head_q == 2 * p + t, qm, jnp.zeros_like(qm)) for t in range(2)], axis=0)
        o2 = _softmax_pv(q2, keys_m, vm)
        for t in range(2):
            om = jnp.where(lane_o // M_V == 2 * p + t, o2[t * qb:(t + 1) * qb], om)
    om_ref[...] = om.astype(_bf16)


def _attn_call(pre, seq, cache, wukv, latent):
    qa, ka, va, qm, kvm, kr = pre
    n = qa.shape[0]
    nb = n // seq
    qb = ATT_QB
    nq = seq // qb
    qrow = lambda b, i: (b * nq + i, 0)
    krow = lambda b, i: (b, 0)
    in_specs = [pl.BlockSpec((qb, qa.shape[1]), qrow), pl.BlockSpec((qb, qm.shape[1]), qrow),
                pl.BlockSpec((seq, ka.shape[1]), krow), pl.BlockSpec((seq, va.shape[1]), krow),
                pl.BlockSpec((seq, kvm.shape[1]), krow), pl.BlockSpec((seq, kr.shape[1]), krow)]
    args = [qa, qm, ka, va, kvm, kr]
    if latent:
        for c in cache:
            in_specs.append(pl.BlockSpec((1,) + c.shape[1:], lambda b, i: (b, 0, 0)))
            args.append(c)
        in_specs.append(pl.BlockSpec(wukv.shape, lambda b, i: (0, 0)))
        args.append(wukv)
    return pl.pallas_call(
        functools.partial(_attn_kernel, latent=latent),
        grid=(nb, nq),
        in_specs=in_specs,
        out_specs=[pl.BlockSpec((qb, QA_W), qrow), pl.BlockSpec((qb, M_HEADS * M_V), qrow)],
        out_shape=[jax.ShapeDtypeStruct((n, QA_W), _bf16), jax.ShapeDtypeStruct((n, M_HEADS * M_V), _bf16)],
        compiler_params=pltpu.CompilerParams(dimension_semantics=("parallel", "parallel"),
                                             vmem_limit_bytes=VMEM_LIMIT),
        name="attn_lat" if latent else "attn_ctx",
    )(*args)


def _post_kernel(oa_ref, om_ref, oc_ref, x_ref, mod_ref, wo_ref, g_ref, b_ref, x1_ref, h2_ref, *, alpha, d_model):
    D = d_model
    a_w = oa_ref.shape[1]
    m_w = om_ref.shape[1]
    mix = (_dot(oa_ref[...], wo_ref[0:a_w, :]) + _dot(om_ref[...], wo_ref[a_w:a_w + m_w, :])
           + _dot(oc_ref[...], wo_ref[a_w + m_w:, :]))
    mod = mod_ref[0]
    g1 = mod[:, 2 * D:3 * D]
    sh2 = mod[:, 3 * D:4 * D]
    sc2 = mod[:, 4 * D:5 * D]
    x1 = _ln_rows(alpha * x_ref[...] + g1 * mix) * g_ref[...] + b_ref[...]
    x1_ref[...] = x1
    h2_ref[...] = (_ln_rows(x1) * (1.0 + sc2) + sh2).astype(_bf16)


def _post_call(oa, om, oc, x2, mod, seq, wo, g, b, alpha):
    n, d = x2.shape
    tb = PRE_TB
    nblk_seq = seq // tb
    row = lambda i: (i, 0)
    const = lambda i: (0, 0)
    return pl.pallas_call(
        functools.partial(_post_kernel, alpha=alpha, d_model=d),
        grid=(n // tb,),
        in_specs=[pl.BlockSpec((tb, oa.shape[1]), row), pl.BlockSpec((tb, om.shape[1]), row),
                  pl.BlockSpec((tb, oc.shape[1]), row), pl.BlockSpec((tb, d), row),
                  pl.BlockSpec((1, 1, mod.shape[-1]), lambda i: (i // nblk_seq, 0, 0)),
                  pl.BlockSpec(wo.shape, const), pl.BlockSpec((1, d), const), pl.BlockSpec((1, d), const)],
        out_specs=[pl.BlockSpec((tb, d), row), pl.BlockSpec((tb, d), row)],
        out_shape=[jax.ShapeDtypeStruct((n, d), _f32), jax.ShapeDtypeStruct((n, d), _bf16)],
        compiler_params=pltpu.CompilerParams(dimension_semantics=("parallel",), vmem_limit_bytes=VMEM_LIMIT),
        name="post",
    )(oa, om, oc, x2, mod, wo, g, b)


def _extract_top(s, n, ranked=0):
    tops = []
    rank = jnp.full(s.shape, float(ranked), _f32)
    for j in range(n):
        m = jnp.max(s, axis=0, keepdims=True)
        tops.append(m)
        top = s == m
        if j < ranked:
            rank = jnp.where(top, float(j), rank)
        if j + 1 < n:
            s = jnp.where(top, NEG_BIG, s)
    return (tops, rank) if ranked else tops


def _stack_rows(rows):
    n = len(rows)
    rid = lax.broadcasted_iota(jnp.int32, (n, LANES), 0)
    out = jnp.broadcast_to(rows[0], (n, LANES))
    for j in range(1, n):
        out = jnp.where(rid == j, rows[j], out)
    return out


def _peer_select(s1, s2):
    k = PEER_TOPK
    v1 = _extract_top(s1, k + 1)
    v2, rank2 = _extract_top(s2, k + 1, ranked=k)
    col1 = _stack_rows(v1[0:k])
    col2 = _stack_rows(v2[0:k])
    pieces = [v1[0] + col2]
    for a in range(1, 8):
        pieces.append(v1[a] + col2[0:8])
    pieces.append(col1[8:16] + v2[0])
    tail = jnp.where(lax.broadcasted_iota(jnp.int32, (8, LANES), 0) == 0, v1[0] + v2[k],
                     jnp.where(lax.broadcasted_iota(jnp.int32, (8, LANES), 0) == 1, v1[k] + v2[0], NEG_BIG))
    pieces.append(tail)
    cand = jnp.concatenate(pieces, axis=0)
    tops = _extract_top(cand, k + 1)
    z = jnp.ones_like(tops[0])
    for j in range(1, k):
        z = z + jnp.exp(tops[j] - tops[0])
    tau = 0.5 * (tops[k - 1] + tops[k])
    cnt = jnp.zeros_like(s1)
    for b in range(8):
        cnt = cnt + jnp.where(s1 >= tau - v2[b], 1.0, 0.0)
    extra = jnp.zeros_like(tau)
    for b in range(8, k):
        extra = extra + jnp.where(v1[0] >= tau - v2[b], 1.0, 0.0)
    cnt = cnt + jnp.where(s1 >= v1[0], extra, 0.0)
    e1 = jnp.exp(s1 - v1[0]) * (1.0 / z)
    e2 = jnp.exp(s2 - v2[0])
    return rank2, cnt, e1, e2


def _gelu2(x):
    c = math.sqrt(2.0 / math.pi)
    return x * (1.0 + jnp.tanh(x * (c + (c * 0.044715) * (x * x))))


def _peer_kernel(h2_ref, x1_ref, mod_ref, wq_ref, k1_ref, k2_ref, u_ref, vt_ref, g_ref, b_ref, y_ref,
                 s1_ref, s2_ref, rank_ref, e2_ref, cnt_ref, e1_ref, cntb_ref, e1b_ref, p0_ref, p1_ref, acc_ref,
                 *, alpha, d_model, n_tiles):
    D = d_model
    e = pl.program_id(1)
    t = h2_ref.shape[0]
    nchunk = t // LANES
    slabs = u_ref.shape[0] // N_KEYS
    p_refs = (p0_ref, p1_ref)

    def spread_rows():
        rows8 = pl.ds(pl.multiple_of(e * slabs, 8), 8)
        for h in range(PEER_HEADS):
            c8 = cnt_ref[h, rows8, :]
            e8 = e1_ref[h, rows8, :]
            for k in range(slabs):
                cntb_ref[h, k] = jnp.broadcast_to(c8[k:k + 1], (ROWS_B, t)).astype(_bf16)
                e1b_ref[h, k] = jnp.broadcast_to(e8[k:k + 1], (ROWS_B, t)).astype(_bf16)

    def gate_chunk(off, a, p_ref):
        cols = pl.ds(off, LANES)
        for k in range(slabs):
            rows = slice(k * N_KEYS, (k + 1) * N_KEYS)
            w = jnp.zeros((N_KEYS // ROWS_B, ROWS_B, LANES), _bf16)
            for h in range(PEER_HEADS):
                hit = rank_ref[h, :, :, cols] < cntb_ref[h, k, :, cols]
                w = w + jnp.where(hit, e2_ref[h, :, :, cols], jnp.zeros((), _bf16)) * e1b_ref[h, k, :, cols]
            p_ref[rows, cols] = _gelu2(a[rows]).astype(_bf16) * w.reshape(N_KEYS, LANES)

    def run(parity, front, back):
        cur, prev = parity, 1 - parity
        half = t // 2
        if front:
            spread_rows()
        for hf in range(2):
            off = hf * half
            if front:
                a = _dot_nt(u_ref[...], h2_ref[pl.ds(off, half), :])
                for c in range(half // LANES):
                    gate_chunk(off + c * LANES, a[:, c * LANES:(c + 1) * LANES], p_refs[cur])
            if back:
                acc_ref[:, pl.ds(off, half)] += _dot(vt_ref[...], p_refs[prev][:, pl.ds(off, half)])

    @pl.when(e == 0)
    def _select():
        q = _dot(h2_ref[...], wq_ref[...]).astype(_bf16)
        for h in range(PEER_HEADS):
            base = h * 2 * PEER_HALF
            s1_ref[h] = _dot_nt(k1_ref[...], q[:, base:base + PEER_HALF])
            s2_ref[h] = _dot_nt(k2_ref[...], q[:, base + PEER_HALF:base + 2 * PEER_HALF])

        def body(it, carry):
            h = it // nchunk
            c = pl.multiple_of((it % nchunk) * LANES, LANES)
            rank2, cnt, e1, e2 = _peer_select(s1_ref[h, :, pl.ds(c, LANES)], s2_ref[h, :, pl.ds(c, LANES)])
            tiles = (N_KEYS // ROWS_B, ROWS_B, LANES)
            rank_ref[h, :, :, pl.ds(c, LANES)] = rank2.astype(_bf16).reshape(tiles)
            cnt_ref[h, :, pl.ds(c, LANES)] = cnt
            e1_ref[h, :, pl.ds(c, LANES)] = 0.5 * e1
            e2_ref[h, :, :, pl.ds(c, LANES)] = e2.astype(_bf16).reshape(tiles)
            return carry

        lax.fori_loop(0, PEER_HEADS * nchunk, body, 0)
        acc_ref[...] = jnp.zeros_like(acc_ref)
        run(0, True, False)

    for parity in range(2):
        @pl.when((e >= 1) & (e < n_tiles) & (e % 2 == parity))
        def _steady():
            run(parity, True, True)

    @pl.when(e == n_tiles)
    def _finish():
        run(n_tiles % 2, False, True)
        mod = mod_ref[0]
        g2 = mod[:, 5 * D:6 * D]
        ff = acc_ref[...].T
        y_ref[...] = _ln_rows(alpha * x1_ref[...] + g2 * ff) * g_ref[...] + b_ref[...]


def _peer_call(h2, x1, mod, seq, wq, k1, k2, u, vt, g, b, alpha):
    n, d = x1.shape
    t = PEER_T
    eb = PEER_EB
    n_exp = u.shape[0]
    nblk_seq = seq // t
    row = lambda i, e: (i, 0)
    const = lambda i, e: (0, 0)
    n_tiles = n_exp // eb
    assert eb == 8 * N_KEYS and n_tiles >= 2
    sel32 = pltpu.VMEM((PEER_HEADS, N_KEYS, t), _f32)
    sel16 = pltpu.VMEM((PEER_HEADS, N_KEYS // ROWS_B, ROWS_B, t), _bf16)
    rowb = pltpu.VMEM((PEER_HEADS, eb // N_KEYS, ROWS_B, t), _bf16)
    return pl.pallas_call(
        functools.partial(_peer_kernel, alpha=alpha, d_model=d, n_tiles=n_tiles),
        grid=(n // t, n_tiles + 1),
        in_specs=[pl.BlockSpec((t, d), row), pl.BlockSpec((t, d), row),
                  pl.BlockSpec((1, 1, mod.shape[-1]), lambda i, e: (i // nblk_seq, 0, 0)),
                  pl.BlockSpec(wq.shape, const), pl.BlockSpec(k1.shape, const), pl.BlockSpec(k2.shape, const),
                  pl.BlockSpec((eb, d), lambda i, e: (jnp.minimum(e, n_tiles - 1), 0)),
                  pl.BlockSpec((d, eb), lambda i, e: (0, jnp.maximum(e - 1, 0))),
                  pl.BlockSpec((1, d), const), pl.BlockSpec((1, d), const)],
        out_specs=pl.BlockSpec((t, d), row),
        out_shape=jax.ShapeDtypeStruct((n, d), _f32),
        scratch_shapes=[sel32, sel32, sel16, sel16, sel32, sel32, rowb, rowb,
                        pltpu.VMEM((eb, t), _bf16), pltpu.VMEM((eb, t), _bf16), pltpu.VMEM((d, t), _f32)],
        compiler_params=pltpu.CompilerParams(dimension_semantics=("parallel", "arbitrary"),
                                             vmem_limit_bytes=VMEM_LIMIT),
        name="peer",
    )(h2, x1, mod, wq, k1, k2, u, vt, g, b)


def _rope_tables(seq, head_dim):
    m = head_dim // 4
    lane = np.arange(LANES)
    d = lane % head_dim
    use_col = (d // (2 * m)) == 1
    within = d % (2 * m)
    second = within // m
    freqs = ROPE_BASE ** (-np.arange(m, dtype=np.float64) / m)
    f = freqs[within % m]
    tpos = np.arange(seq)
    pos = np.where(use_col[None, :], (tpos % GRID_W)[:, None], (tpos // GRID_W)[:, None]).astype(np.float64)
    ang = pos * f[None, :]
    sign = np.where(second == 0, -1.0, 1.0)
    return jnp.asarray(np.cos(ang), _f32), jnp.asarray(np.sin(ang) * sign[None, :], _f32)


def _layer_weights(l, w_in, attn_q_norm, attn_k_norm, mla_q_norm, mla_kv_norm, w_uq, w_ukv, gmlp_ws, gmlp_b, w_o):
    offs = np.cumsum([0, QA_W, KA_W, KA_W, M_Q_RANK, M_KV_RANK, M_ROPE, CW, CW])
    qa_perm = np.concatenate([np.arange(HEAD_DIM) + HEAD_DIM * (j + 4 * half)
                              for j in range(A_HEADS // 2) for half in range(2)])
    cols = np.concatenate([qa_perm, np.arange(offs[1], offs[5]),
                           np.tile(np.arange(offs[5], offs[6]), M_HEADS), np.arange(offs[6], offs[8])])
    qd = M_NOPE + M_ROPE
    uq_cols = np.concatenate([np.arange(M_NOPE) + h * qd for h in range(M_HEADS)]
                             + [np.arange(M_ROPE) + h * qd + M_NOPE for h in range(M_HEADS)])
    kd = M_NOPE + M_V
    ukv_cols = np.concatenate([np.arange(M_NOPE) + h * kd for h in range(M_HEADS)]
                              + [np.arange(M_V) + h * kd + M_NOPE for h in range(M_HEADS)])
    wo_rows = np.concatenate([qa_perm, np.arange(QA_W, w_o.shape[1])])
    seg = np.arange(QA_W) // HEAD_DIM
    return {
        'win': w_in[l][:, cols].astype(_bf16),
        'aqn': jnp.tile(attn_q_norm[l], A_HEADS)[None, :],
        'akn': jnp.tile(attn_k_norm[l], A_KV)[None, :],
        'mqn': mla_q_norm[l][None, :],
        'mkvn': mla_kv_norm[l][None, :],
        'wuq': w_uq[l][:, uq_cols].astype(_bf16),
        'wukv': w_ukv[l][:, ukv_cols].astype(_bf16),
        'ws': gmlp_ws[l].astype(_bf16),
        'bsf': jnp.repeat(gmlp_b[l].T, C_DIM, axis=1),
        'bd': jnp.asarray(seg[:, None] == seg[None, :], _bf16),
        'wo': w_o[l][wo_rows, :].astype(_bf16),
    }


def kernel(x_prompt, x_sample, cache_attn_k, cache_attn_v, cache_mla_ckv, cache_mla_krope, c, c_ctx, w_mod, b_mod, w_in, attn_q_norm, attn_k_norm, mla_q_norm, mla_kv_norm, w_uq, w_ukv, gmlp_ws, gmlp_b, w_o, ln1_g, ln1_b, ln2_g, ln2_b, peer_wq, peer_k1, peer_k2, peer_u, peer_v):
    batch, seq, d = x_prompt.shape
    dec_batch, dec_seq, _ = x_sample.shape
    depth = w_in.shape[0]
    past = cache_attn_k.shape[2]
    alpha = (2.0 * depth) ** 0.25

    n_rows = 8 * ((1 + dec_batch + 7) // 8)
    c_all = jnp.zeros((n_rows, d), _f32).at[0].set(c_ctx).at[1:1 + dec_batch].set(c)
    mod_all = _mod_call(c_all, w_mod, b_mod)

    rope = _rope_tables(dec_seq, HEAD_DIM) + _rope_tables(dec_seq, M_ROPE)
    xp = x_prompt.reshape(batch * seq, d)
    xs = x_sample.reshape(dec_batch * dec_seq, d)
    new = []
    for l in range(depth):
        wts = _layer_weights(l, w_in, attn_q_norm, attn_k_norm, mla_q_norm, mla_kv_norm, w_uq, w_ukv,
                             gmlp_ws, gmlp_b, w_o)
        peer_w = (peer_wq[l].astype(_bf16), peer_k1[l].astype(_bf16), peer_k2[l].astype(_bf16),
                  peer_u[l].astype(_bf16), peer_v[l].T.astype(_bf16))
        ln1 = (ln1_g[l][None, :], ln1_b[l][None, :])
        ln2 = (ln2_g[l][None, :], ln2_b[l][None, :])
        mod_ctx = mod_all[l, 0:1][:, None, :]
        mod_lat = mod_all[l, 1:1 + dec_batch][:, None, :]
        cache = (cache_attn_k[:, l].reshape(dec_batch, past, KA_W),
                 cache_attn_v[:, l].reshape(dec_batch, past, KA_W),
                 cache_mla_ckv[:, l],
                 jnp.tile(cache_mla_krope[:, l], (1, 1, M_HEADS)))

        pre = _pre_call(xp, mod_ctx, batch * seq, wts, None, latent=False)
        new.append(pre[7:])
        oa, om = _attn_call(pre[0:6], seq, None, None, latent=False)
        x1, h2 = _post_call(oa, om, pre[6], xp, mod_ctx, batch * seq, wts['wo'], *ln1, alpha)
        xp = _peer_call(h2, x1, mod_ctx, batch * seq, *peer_w, *ln2, alpha)

        pre = _pre_call(xs, mod_lat, dec_seq, wts, rope, latent=True)
        oa, om = _attn_call(pre[0:6], dec_seq, cache, wts['wukv'], latent=True)
        x1, h2 = _post_call(oa, om, pre[6], xs, mod_lat, dec_seq, wts['wo'], *ln1, alpha)
        xs = _peer_call(h2, x1, mod_lat, dec_seq, *peer_w, *ln2, alpha)

    def stack(i, tail):
        return jnp.stack([new[l][i].reshape((batch, seq) + tail) for l in range(depth)], axis=1)

    return (xp.reshape(batch, seq, d), xs.reshape(dec_batch, dec_seq, d),
            stack(0, (A_KV, HEAD_DIM)), stack(1, (A_KV, HEAD_DIM)), stack(2, (M_KV_RANK,)), stack(3, (M_ROPE,)))
```

```python
import functools
import math

import numpy as np
import jax
import jax.numpy as jnp
from jax import lax
from jax.experimental import pallas as pl
from jax.experimental.pallas import tpu as pltpu

A_HEADS = 8
A_KV = 2
HEAD_DIM = 64
M_HEADS = 4
M_Q_RANK = 256
M_KV_RANK = 128
M_NOPE = 64
M_ROPE = 32
M_V = 64
C_GROUPS = 4
C_DIM = 64
C_CHUNK = 128
PEER_HEADS = 8
N_KEYS = 128
PEER_HALF = 128
PEER_TOPK = 16
GRID_W = 64
EPS = 1e-6
ROPE_BASE = 10000.0

LANES = 128
ROWS_B = 16
NEG_BIG = -3.0e38

QA_W = A_HEADS * HEAD_DIM
KA_W = A_KV * HEAD_DIM
CQ_OFF = QA_W + 2 * KA_W
CKV_OFF = CQ_OFF + M_Q_RANK
KR_OFF = CKV_OFF + M_KV_RANK
UC_OFF = KR_OFF + M_HEADS * M_ROPE
CW = C_GROUPS * C_DIM
VC_OFF = UC_OFF + CW
IN_WP = VC_OFF + CW

PRE_TB = 256
ATT_QB = 256
PEER_T = 512
PEER_EB = 1024
VMEM_LIMIT = 56 * 1024 * 1024

_bf16 = jnp.bfloat16
_f32 = jnp.float32


def _dot(a, b):
    return jnp.dot(a, b, preferred_element_type=_f32)


def _dot_nt(a, b):
    return lax.dot_general(a, b, (((1,), (1,)), ((), ())), preferred_element_type=_f32)


def _split(x):
    hi = x.astype(_bf16)
    lo = (x - hi.astype(_f32)).astype(_bf16)
    return hi, lo


def _segsum(x, ones_bd):
    hi, lo = _split(x)
    return _dot(hi, ones_bd) + _dot(lo, ones_bd)


def _ln_rows(x):
    mu = jnp.mean(x, axis=-1, keepdims=True)
    d = x - mu
    var = jnp.mean(d * d, axis=-1, keepdims=True)
    return d * lax.rsqrt(var + EPS)


def _rot(x, cos, sin, half):
    lane = lax.broadcasted_iota(jnp.int32, x.shape, 1)
    up = pltpu.roll(x, LANES - half, axis=1)
    dn = pltpu.roll(x, half, axis=1)
    partner = jnp.where((lane & half) == 0, up, dn)
    return x * cos + partner * sin


def _mod_kernel(c_ref, w_ref, b_ref, o_ref):
    c = c_ref[...]
    a = c * (1.0 / (1.0 + jnp.exp(-c)))
    a_hi, a_lo = _split(a)
    w_hi, w_lo = _split(w_ref[0])
    o_ref[0] = _dot(a_hi, w_hi) + _dot(a_lo, w_hi) + _dot(a_hi, w_lo) + b_ref[0]


def _mod_call(c_all, w_mod, b_mod):
    depth, d, d6 = w_mod.shape
    rows = c_all.shape[0]
    tn = 1024
    return pl.pallas_call(
        _mod_kernel,
        grid=(depth, d6 // tn),
        in_specs=[pl.BlockSpec((rows, d), lambda l, j: (0, 0)),
                  pl.BlockSpec((1, d, tn), lambda l, j: (l, 0, j)),
                  pl.BlockSpec((1, 1, tn), lambda l, j: (l, 0, j))],
        out_specs=pl.BlockSpec((1, rows, tn), lambda l, j: (l, 0, j)),
        out_shape=jax.ShapeDtypeStruct((depth, rows, d6), _f32),
        compiler_params=pltpu.CompilerParams(dimension_semantics=("parallel", "parallel"),
                                             vmem_limit_bytes=VMEM_LIMIT),
        name="mod",
    )(c_all, w_mod, b_mod.reshape(depth, 1, d6))


def _pre_kernel(*refs, latent, d_model):
    if latent:
        (x_ref, mod_ref, win_ref, aqn_ref, akn_ref, mqn_ref, mkvn_ref, wuq_ref, wukv_ref, ws_ref, bsf_ref,
         bd_ref, cosa_ref, sina_ref, cosm_ref, sinm_ref,
         qa_ref, ka_ref, va_ref, qm_ref, kvm_ref, kr_ref, oc_ref) = refs
    else:
        (x_ref, mod_ref, win_ref, aqn_ref, akn_ref, mqn_ref, mkvn_ref, wuq_ref, wukv_ref, ws_ref, bsf_ref,
         bd_ref,
         qa_ref, ka_ref, va_ref, qm_ref, kvm_ref, kr_ref, oc_ref,
         nk_ref, nv_ref, nckv_ref, nkr_ref) = refs
    D = d_model
    x = x_ref[...]
    mod = mod_ref[0]
    sh1 = mod[:, 0:D]
    sc1 = mod[:, D:2 * D]
    h = _ln_rows(x) * (1.0 + sc1) + sh1
    proj = _dot(h.astype(_bf16), win_ref[...])

    qa = proj[:, 0:QA_W]
    ms = _segsum(qa * qa, bd_ref[...]) * (1.0 / HEAD_DIM)
    qa = qa * lax.rsqrt(ms + EPS) * aqn_ref[...]
    ka = proj[:, QA_W:QA_W + KA_W]
    ms = _segsum(ka * ka, bd_ref[0:KA_W, 0:KA_W]) * (1.0 / HEAD_DIM)
    ka = ka * lax.rsqrt(ms + EPS) * akn_ref[...]
    va = proj[:, QA_W + KA_W:CQ_OFF]
    if not latent:
        nk_ref[...] = ka
        nv_ref[...] = va
    for j in range(QA_W // LANES):
        slab = qa[:, j * LANES:(j + 1) * LANES]
        if latent:
            slab = _rot(slab, cosa_ref[...], sina_ref[...], HEAD_DIM // 4)
        qa_ref[:, j * LANES:(j + 1) * LANES] = (slab * (1.0 / math.sqrt(HEAD_DIM))).astype(_bf16)
    if latent:
        ka = _rot(ka, cosa_ref[...], sina_ref[...], HEAD_DIM // 4)
    ka_ref[...] = ka.astype(_bf16)
    va_ref[...] = va.astype(_bf16)

    cq = proj[:, CQ_OFF:CKV_OFF]
    cq = cq * lax.rsqrt(jnp.mean(cq * cq, axis=-1, keepdims=True) + EPS) * mqn_ref[...]
    qm = _dot(cq.astype(_bf16), wuq_ref[...])
    nope_w = M_HEADS * M_NOPE
    qscale = 1.0 / math.sqrt(M_NOPE + M_ROPE)
    qm_ref[:, 0:nope_w] = (qm[:, 0:nope_w] * qscale).astype(_bf16)
    qr = qm[:, nope_w:]
    if latent:
        qr = _rot(qr, cosm_ref[...], sinm_ref[...], M_ROPE // 4)
    qm_ref[:, nope_w:] = (qr * qscale).astype(_bf16)
    ckv = proj[:, CKV_OFF:KR_OFF]
    ckv = ckv * lax.rsqrt(jnp.mean(ckv * ckv, axis=-1, keepdims=True) + EPS) * mkvn_ref[...]
    kvm_ref[...] = _dot(ckv.astype(_bf16), wukv_ref[...]).astype(_bf16)
    kr = proj[:, KR_OFF:UC_OFF]
    if not latent:
        nckv_ref[...] = ckv
        nkr_ref[...] = kr[:, 0:M_ROPE]
    else:
        kr = _rot(kr, cosm_ref[...], sinm_ref[...], M_ROPE // 4)
    kr_ref[...] = kr.astype(_bf16)

    uc = proj[:, UC_OFF:VC_OFF]
    vc = proj[:, VC_OFF:IN_WP]
    bd_c = bd_ref[0:CW, 0:CW]
    mu = _segsum(vc, bd_c) * (1.0 / C_DIM)
    dv = vc - mu
    var = _segsum(dv * dv, bd_c) * (1.0 / C_DIM)
    vg = (dv * lax.rsqrt(var + EPS)).astype(_bf16)
    lane_c = lax.broadcasted_iota(jnp.int32, (C_CHUNK, CW), 1)
    for r in range(x.shape[0] // C_CHUNK):
        rows = slice(r * C_CHUNK, (r + 1) * C_CHUNK)
        vgc = vg[rows]
        mixed = bsf_ref[...]
        for g in range(C_GROUPS):
            mg = _dot(ws_ref[g], vgc)
            mixed = mixed + jnp.where(lane_c // C_DIM == g, mg, 0.0)
        oc_ref[rows, :] = (uc[rows] * mixed).astype(_bf16)


def _pre_call(x2, mod, seq, wts, rope, latent):
    n, d = x2.shape
    tb = PRE_TB
    nblk_seq = seq // tb
    row = lambda i: (i, 0)
    const = lambda i: (0, 0)
    in_specs = [pl.BlockSpec((tb, d), row),
                pl.BlockSpec((1, 1, mod.shape[-1]), lambda i: (i // nblk_seq, 0, 0)),
                pl.BlockSpec(wts['win'].shape, const),
                pl.BlockSpec((1, QA_W), const), pl.BlockSpec((1, KA_W), const),
                pl.BlockSpec((1, M_Q_RANK), const), pl.BlockSpec((1, M_KV_RANK), const),
                pl.BlockSpec(wts['wuq'].shape, const), pl.BlockSpec(wts['wukv'].shape, const),
                pl.BlockSpec(wts['ws'].shape, lambda i: (0, 0, 0)),
                pl.BlockSpec((C_CHUNK, CW), const),
                pl.BlockSpec((QA_W, QA_W), const)]
    args = [x2, mod, wts['win'], wts['aqn'], wts['akn'], wts['mqn'], wts['mkvn'], wts['wuq'], wts['wukv'],
            wts['ws'], wts['bsf'], wts['bd']]
    if latent:
        pos = lambda i: (i % nblk_seq, 0)
        in_specs += [pl.BlockSpec((tb, LANES), pos)] * 4
        args += list(rope)
    widths = [QA_W, KA_W, KA_W, M_HEADS * (M_NOPE + M_ROPE), M_HEADS * (M_NOPE + M_V), M_HEADS * M_ROPE, CW]
    out_specs = [pl.BlockSpec((tb, w), row) for w in widths]
    out_shape = [jax.ShapeDtypeStruct((n, w), _bf16) for w in widths]
    if not latent:
        for w in (KA_W, KA_W, M_KV_RANK, M_ROPE):
            out_specs.append(pl.BlockSpec((tb, w), row))
            out_shape.append(jax.ShapeDtypeStruct((n, w), _f32))
    return pl.pallas_call(
        functools.partial(_pre_kernel, latent=latent, d_model=d),
        grid=(n // tb,),
        in_specs=in_specs, out_specs=out_specs, out_shape=out_shape,
        compiler_params=pltpu.CompilerParams(dimension_semantics=("parallel",), vmem_limit_bytes=VMEM_LIMIT),
        name="pre_lat" if latent else "pre_ctx",
    )(*args)


def _softmax_pv(q, keys, vals):
    s = [_dot_nt(q, k) for k in keys]
    m = s[0].max(axis=-1, keepdims=True)
    for si in s[1:]:
        m = jnp.maximum(m, si.max(axis=-1, keepdims=True))
    den = None
    o = None
    for si, v in zip(s, vals):
        e = jnp.exp(si - m)
        d = e.sum(axis=-1, keepdims=True)
        den = d if den is None else den + d
        pv = _dot(e.astype(_bf16), v)
        o = pv if o is None else o + pv
    return o * (1.0 / den)


def _attn_kernel(*refs, latent):
    if latent:
        (qa_ref, qm_ref, ka_ref, va_ref, kvm_ref, kr_ref, ck_ref, cv_ref, cckv_ref, ckr_ref, wukv_ref,
         oa_ref, om_ref) = refs
    else:
        qa_ref, qm_ref, ka_ref, va_ref, kvm_ref, kr_ref, oa_ref, om_ref = refs
    qb = qa_ref.shape[0]
    nope_w = M_HEADS * M_NOPE

    keys_a = [ka_ref[...]]
    vals_a = [va_ref[...]]
    if latent:
        keys_a = [ck_ref[0].astype(_bf16)] + keys_a
        vals_a = [cv_ref[0].astype(_bf16)] + vals_a
    lane = lax.broadcasted_iota(jnp.int32, (qb, LANES), 1)
    low = lane < HEAD_DIM
    for j in range(QA_W // LANES):
        slab = qa_ref[:, j * LANES:(j + 1) * LANES]
        q2 = jnp.concatenate([jnp.where(low, slab, jnp.zeros_like(slab)),
                              jnp.where(low, jnp.zeros_like(slab), slab)], axis=0)
        o2 = _softmax_pv(q2, keys_a, vals_a)
        oa_ref[:, j * LANES:(j + 1) * LANES] = jnp.where(low, o2[0:qb], o2[qb:]).astype(_bf16)

    kn = [kvm_ref[:, 0:nope_w]]
    vm = [kvm_ref[:, nope_w:]]
    krs = [kr_ref[...]]
    if latent:
        kvc = _dot(cckv_ref[0].astype(_bf16), wukv_ref[...]).astype(_bf16)
        kn = [kvc[:, 0:nope_w]] + kn
        vm = [kvc[:, nope_w:]] + vm
        krs = [ckr_ref[0].astype(_bf16)] + krs
    keys_m = [jnp.concatenate([a, b], axis=1) for a, b in zip(kn, krs)]
    qm = qm_ref[...]
    lane_q = lax.broadcasted_iota(jnp.int32, qm.shape, 1)
    head_q = jnp.where(lane_q < nope_w, lane_q // M_NOPE, (lane_q - nope_w) // M_ROPE)
    lane_o = lax.broadcasted_iota(jnp.int32, (qb, nope_w), 1)
    om = jnp.zeros((qb, nope_w), _f32)
    for p in range(M_HEADS // 2):
        q2 = jnp.concatenate([jnp.where(head_q == 2 * p + t, qm, jnp.zeros_like(qm)) for t in range(2)], axis=0)
        o2 = _softmax_pv(q2, keys_m, vm)
        for t in range(2):
            om = jnp.where(lane_o // M_V == 2 * p + t, o2[t * qb:(t + 1) * qb], om)
    om_ref[...] = om.astype(_bf16)


def _attn_call(pre, seq, cache, wukv, latent):
    qa, ka, va, qm, kvm, kr = pre
    n = qa.shape[0]
    nb = n // seq
    qb = ATT_QB
    nq = seq // qb
    qrow = lambda b, i: (b * nq + i, 0)
    krow = lambda b, i: (b, 0)
    in_specs = [pl.BlockSpec((qb, qa.shape[1]), qrow), pl.BlockSpec((qb, qm.shape[1]), qrow),
                pl.BlockSpec((seq, ka.shape[1]), krow), pl.BlockSpec((seq, va.shape[1]), krow),
                pl.BlockSpec((seq, kvm.shape[1]), krow), pl.BlockSpec((seq, kr.shape[1]), krow)]
    args = [qa, qm, ka, va, kvm, kr]
    if latent:
        for c in cache:
            in_specs.append(pl.BlockSpec((1,) + c.shape[1:], lambda b, i: (b, 0, 0)))
            args.append(c)
        in_specs.append(pl.BlockSpec(wukv.shape, lambda b, i: (0, 0)))
        args.append(wukv)
    return pl.pallas_call(
        functools.partial(_attn_kernel, latent=latent),
        grid=(nb, nq),
        in_specs=in_specs,
        out_specs=[pl.BlockSpec((qb, QA_W), qrow), pl.BlockSpec((qb, M_HEADS * M_V), qrow)],
        out_shape=[jax.ShapeDtypeStruct((n, QA_W), _bf16), jax.ShapeDtypeStruct((n, M_HEADS * M_V), _bf16)],
        compiler_params=pltpu.CompilerParams(dimension_semantics=("parallel", "parallel"),
                                             vmem_limit_bytes=VMEM_LIMIT),
        name="attn_lat" if latent else "attn_ctx",
    )(*args)


def _post_kernel(oa_ref, om_ref, oc_ref, x_ref, mod_ref, wo_ref, g_ref, b_ref, x1_ref, h2_ref, *, alpha, d_model):
    D = d_model
    a_w = oa_ref.shape[1]
    m_w = om_ref.shape[1]
    mix = (_dot(oa_ref[...], wo_ref[0:a_w, :]) + _dot(om_ref[...], wo_ref[a_w:a_w + m_w, :])
           + _dot(oc_ref[...], wo_ref[a_w + m_w:, :]))
    mod = mod_ref[0]
    g1 = mod[:, 2 * D:3 * D]
    sh2 = mod[:, 3 * D:4 * D]
    sc2 = mod[:, 4 * D:5 * D]
    x1 = _ln_rows(alpha * x_ref[...] + g1 * mix) * g_ref[...] + b_ref[...]
    x1_ref[...] = x1
    h2_ref[...] = (_ln_rows(x1) * (1.0 + sc2) + sh2).astype(_bf16)


def _post_call(oa, om, oc, x2, mod, seq, wo, g, b, alpha):
    n, d = x2.shape
    tb = PRE_TB
    nblk_seq = seq // tb
    row = lambda i: (i, 0)
    const = lambda i: (0, 0)
    return pl.pallas_call(
        functools.partial(_post_kernel, alpha=alpha, d_model=d),
        grid=(n // tb,),
        in_specs=[pl.BlockSpec((tb, oa.shape[1]), row), pl.BlockSpec((tb, om.shape[1]), row),
                  pl.BlockSpec((tb, oc.shape[1]), row), pl.BlockSpec((tb, d), row),
                  pl.BlockSpec((1, 1, mod.shape[-1]), lambda i: (i // nblk_seq, 0, 0)),
                  pl.BlockSpec(wo.shape, const), pl.BlockSpec((1, d), const), pl.BlockSpec((1, d), const)],
        out_specs=[pl.BlockSpec((tb, d), row), pl.BlockSpec((tb, d), row)],
        out_shape=[jax.ShapeDtypeStruct((n, d), _f32), jax.ShapeDtypeStruct((n, d), _bf16)],
        compiler_params=pltpu.CompilerParams(dimension_semantics=("parallel",), vmem_limit_bytes=VMEM_LIMIT),
        name="post",
    )(oa, om, oc, x2, mod, wo, g, b)


def _extract_top(s, n, ranked=0):
    tops = []
    rank = jnp.full(s.shape, float(ranked), _f32)
    for j in range(n):
        m = jnp.max(s, axis=0, keepdims=True)
        tops.append(m)
        top = s == m
        if j < ranked:
            rank = jnp.where(top, float(j), rank)
        if j + 1 < n:
            s = jnp.where(top, NEG_BIG, s)
    return (tops, rank) if ranked else tops


def _stack_rows(rows):
    n = len(rows)
    rid = lax.broadcasted_iota(jnp.int32, (n, LANES), 0)
    out = jnp.broadcast_to(rows[0], (n, LANES))
    for j in range(1, n):
        out = jnp.where(rid == j, rows[j], out)
    return out


def _peer_select(s1, s2):
    k = PEER_TOPK
    v1 = _extract_top(s1, k + 1)
    v2, rank2 = _extract_top(s2, k + 1, ranked=k)
    col1 = _stack_rows(v1[0:k])
    col2 = _stack_rows(v2[0:k])
    pieces = [v1[0] + col2]
    for a in range(1, 8):
        pieces.append(v1[a] + col2[0:8])
    pieces.append(col1[8:16] + v2[0])
    tail = jnp.where(lax.broadcasted_iota(jnp.int32, (8, LANES), 0) == 0, v1[0] + v2[k],
                     jnp.where(lax.broadcasted_iota(jnp.int32, (8, LANES), 0) == 1, v1[k] + v2[0], NEG_BIG))
    pieces.append(tail)
    cand = jnp.concatenate(pieces, axis=0)
    tops = _extract_top(cand, k + 1)
    z = jnp.ones_like(tops[0])
    for j in range(1, k):
        z = z + jnp.exp(tops[j] - tops[0])
    tau = 0.5 * (tops[k - 1] + tops[k])
    cnt = jnp.zeros_like(s1)
    for b in range(8):
        cnt = cnt + jnp.where(s1 >= tau - v2[b], 1.0, 0.0)
    extra = jnp.zeros_like(tau)
    for b in range(8, k):
        extra = extra + jnp.where(v1[0] >= tau - v2[b], 1.0, 0.0)
    cnt = cnt + jnp.where(s1 >= v1[0], extra, 0.0)
    e1 = jnp.exp(s1 - v1[0]) * (1.0 / z)
    e2 = jnp.exp(s2 - v2[0])
    return rank2, cnt, e1, e2


def _gelu2(x):
    c = math.sqrt(2.0 / math.pi)
    return x * (1.0 + jnp.tanh(x * (c + (c * 0.044715) * (x * x))))


def _peer_kernel(h2_ref, x1_ref, mod_ref, wq_ref, k1_ref, k2_ref, u_ref, vt_ref, g_ref, b_ref, y_ref,
                 s2_ref, re_ref, cnt_ref, e1_ref, a0_ref, a1_ref, p0_ref, p1_ref, acc_ref,
                 *, alpha, d_model, n_tiles):
    D = d_model
    e = pl.program_id(1)
    t = h2_ref.shape[0]
    nchunk = t // LANES
    half = t // 2
    slabs = u_ref.shape[0] // N_KEYS
    a_refs = (a0_ref, a1_ref)
    p_refs = (p0_ref, p1_ref)

    def gate_chunk(off, a_ref, p_ref):
        cols = pl.ds(off, LANES)
        rk_cols = pl.ds(2 * off, LANES)
        e2_cols = pl.ds(2 * off + LANES, LANES)
        for g in range(slabs // 8):
            rows8 = pl.ds(pl.multiple_of((e - 1) * slabs + g * 8, 8), 8)
            cnt8 = [cnt_ref[h, rows8, cols] for h in range(PEER_HEADS)]
            e18 = [e1_ref[h, rows8, cols] for h in range(PEER_HEADS)]
            for k in range(8):
                w = jnp.zeros((N_KEYS, LANES), _bf16)
                for h in range(PEER_HEADS):
                    hit = re_ref[h, :, rk_cols] < cnt8[h][k:k + 1].astype(_bf16)
                    w = w + jnp.where(hit, re_ref[h, :, e2_cols], jnp.zeros((), _bf16)) * e18[h][k:k + 1].astype(_bf16)
                rows = slice((g * 8 + k) * N_KEYS, (g * 8 + k + 1) * N_KEYS)
                p_ref[rows, cols] = _gelu2(a_ref[rows, cols]).astype(_bf16) * w

    def run(parity, stage1, stage2, stage3):
        cur, prev = parity, 1 - parity

        for hf in range(2):
            off = hf * half
            if stage1:
                a_refs[cur][:, pl.ds(off, half)] = _dot_nt(u_ref[...], h2_ref[pl.ds(off, half), :])
            if stage2:
                for c in range(half // LANES):
                    gate_chunk(off + c * LANES, a_refs[prev], p_refs[prev])
            if stage3:
                acc_ref[:, pl.ds(off, half)] += _dot(vt_ref[...], p_refs[cur][:, pl.ds(off, half)])

    @pl.when(e == 0)
    def _select():
        q = _dot(h2_ref[...], wq_ref[...]).astype(_bf16)
        for h in range(PEER_HEADS):
            base = h * 2 * PEER_HALF
            rows = slice(h * N_KEYS, (h + 1) * N_KEYS)
            a1_ref[rows, :] = _dot_nt(k1_ref[...], q[:, base:base + PEER_HALF])
            s2_ref[h] = _dot_nt(k2_ref[...], q[:, base + PEER_HALF:base + 2 * PEER_HALF])

        def body(it, carry):
            h = it // nchunk
            c = pl.multiple_of((it % nchunk) * LANES, LANES)
            r = pl.multiple_of(h * N_KEYS, N_KEYS)
            rank2, cnt, e1, e2 = _peer_select(a1_ref[pl.ds(r, N_KEYS), pl.ds(c, LANES)],
                                              s2_ref[h, :, pl.ds(c, LANES)])
            re_ref[h, :, pl.ds(pl.multiple_of(2 * c, LANES), LANES)] = rank2.astype(_bf16)
            cnt_ref[h, :, pl.ds(c, LANES)] = cnt
            e1_ref[h, :, pl.ds(c, LANES)] = 0.5 * e1
            re_ref[h, :, pl.ds(pl.multiple_of(2 * c + LANES, LANES), LANES)] = e2.astype(_bf16)
            return carry

        lax.fori_loop(0, PEER_HEADS * nchunk, body, 0)
        acc_ref[...] = jnp.zeros_like(acc_ref)
        run(0, True, False, False)

    @pl.when(e == 1)
    def _fill():
        run(1, True, True, False)

    for parity in range(2):
        @pl.when((e >= 2) & (e < n_tiles) & (e % 2 == parity))
        def _steady():
            run(parity, True, True, True)

    @pl.when(e == n_tiles)
    def _drain():
        run(n_tiles % 2, False, True, True)

    @pl.when(e == n_tiles + 1)
    def _finish():
        run((n_tiles + 1) % 2, False, False, True)
        mod = mod_ref[0]
        g2 = mod[:, 5 * D:6 * D]
        ff = acc_ref[...].T
        y_ref[...] = _ln_rows(alpha * x1_ref[...] + g2 * ff) * g_ref[...] + b_ref[...]


def _peer_call(h2, x1, mod, seq, wq, k1, k2, u, vt, g, b, alpha):
    n, d = x1.shape
    t = PEER_T
    eb = PEER_EB
    n_exp = u.shape[0]
    nblk_seq = seq // t
    row = lambda i, e: (i, 0)
    const = lambda i, e: (0, 0)
    n_tiles = n_exp // eb
    assert eb == PEER_HEADS * N_KEYS and n_tiles >= 2
    sel32 = pltpu.VMEM((PEER_HEADS, N_KEYS, t), _f32)
    sel16 = pltpu.VMEM((PEER_HEADS, N_KEYS, 2 * t), _bf16)
    return pl.pallas_call(
        functools.partial(_peer_kernel, alpha=alpha, d_model=d, n_tiles=n_tiles),
        grid=(n // t, n_tiles + 2),
        in_specs=[pl.BlockSpec((t, d), row), pl.BlockSpec((t, d), row),
                  pl.BlockSpec((1, 1, mod.shape[-1]), lambda i, e: (i // nblk_seq, 0, 0)),
                  pl.BlockSpec(wq.shape, const), pl.BlockSpec(k1.shape, const), pl.BlockSpec(k2.shape, const),
                  pl.BlockSpec((eb, d), lambda i, e: (jnp.minimum(e, n_tiles - 1), 0)),
                  pl.BlockSpec((d, eb), lambda i, e: (0, jnp.clip(e - 2, 0, n_tiles - 1))),
                  pl.BlockSpec((1, d), const), pl.BlockSpec((1, d), const)],
        out_specs=pl.BlockSpec((t, d), row),
        out_shape=jax.ShapeDtypeStruct((n, d), _f32),
        scratch_shapes=[sel32, sel16, sel32, sel32,
                        pltpu.VMEM((eb, t), _f32), pltpu.VMEM((eb, t), _f32),
                        pltpu.VMEM((eb, t), _bf16), pltpu.VMEM((eb, t), _bf16), pltpu.VMEM((d, t), _f32)],
        compiler_params=pltpu.CompilerParams(dimension_semantics=("parallel", "arbitrary"),
                                             vmem_limit_bytes=VMEM_LIMIT),
        name="peer",
    )(h2, x1, mod, wq, k1, k2, u, vt, g, b)


def _rope_tables(seq, head_dim):
    m = head_dim // 4
    lane = np.arange(LANES)
    d = lane % head_dim
    use_col = (d // (2 * m)) == 1
    within = d % (2 * m)
    second = within // m
    freqs = ROPE_BASE ** (-np.arange(m, dtype=np.float64) / m)
    f = freqs[within % m]
    tpos = np.arange(seq)
    pos = np.where(use_col[None, :], (tpos % GRID_W)[:, None], (tpos // GRID_W)[:, None]).astype(np.float64)
    ang = pos * f[None, :]
    sign = np.where(second == 0, -1.0, 1.0)
    return jnp.asarray(np.cos(ang), _f32), jnp.asarray(np.sin(ang) * sign[None, :], _f32)


def _layer_weights(l, w_in, attn_q_norm, attn_k_norm, mla_q_norm, mla_kv_norm, w_uq, w_ukv, gmlp_ws, gmlp_b, w_o):
    offs = np.cumsum([0, QA_W, KA_W, KA_W, M_Q_RANK, M_KV_RANK, M_ROPE, CW, CW])
    qa_perm = np.concatenate([np.arange(HEAD_DIM) + HEAD_DIM * (j + 4 * half)
                              for j in range(A_HEADS // 2) for half in range(2)])
    cols = np.concatenate([qa_perm, np.arange(offs[1], offs[5]),
                           np.tile(np.arange(offs[5], offs[6]), M_HEADS), np.arange(offs[6], offs[8])])
    qd = M_NOPE + M_ROPE
    uq_cols = np.concatenate([np.arange(M_NOPE) + h * qd for h in range(M_HEADS)]
                             + [np.arange(M_ROPE) + h * qd + M_NOPE for h in range(M_HEADS)])
    kd = M_NOPE + M_V
    ukv_cols = np.concatenate([np.arange(M_NOPE) + h * kd for h in range(M_HEADS)]
                              + [np.arange(M_V) + h * kd + M_NOPE for h in range(M_HEADS)])
    wo_rows = np.concatenate([qa_perm, np.arange(QA_W, w_o.shape[1])])
    seg = np.arange(QA_W) // HEAD_DIM
    return {
        'win': w_in[l][:, cols].astype(_bf16),
        'aqn': jnp.tile(attn_q_norm[l], A_HEADS)[None, :],
        'akn': jnp.tile(attn_k_norm[l], A_KV)[None, :],
        'mqn': mla_q_norm[l][None, :],
        'mkvn': mla_kv_norm[l][None, :],
        'wuq': w_uq[l][:, uq_cols].astype(_bf16),
        'wukv': w_ukv[l][:, ukv_cols].astype(_bf16),
        'ws': gmlp_ws[l].astype(_bf16),
        'bsf': jnp.repeat(gmlp_b[l].T, C_DIM, axis=1),
        'bd': jnp.asarray(seg[:, None] == seg[None, :], _bf16),
        'wo': w_o[l][wo_rows, :].astype(_bf16),
    }


def kernel(x_prompt, x_sample, cache_attn_k, cache_attn_v, cache_mla_ckv, cache_mla_krope, c, c_ctx, w_mod, b_mod, w_in, attn_q_norm, attn_k_norm, mla_q_norm, mla_kv_norm, w_uq, w_ukv, gmlp_ws, gmlp_b, w_o, ln1_g, ln1_b, ln2_g, ln2_b, peer_wq, peer_k1, peer_k2, peer_u, peer_v):
    batch, seq, d = x_prompt.shape
    dec_batch, dec_seq, _ = x_sample.shape
    depth = w_in.shape[0]
    past = cache_attn_k.shape[2]
    alpha = (2.0 * depth) ** 0.25

    n_rows = 8 * ((1 + dec_batch + 7) // 8)
    c_all = jnp.zeros((n_rows, d), _f32).at[0].set(c_ctx).at[1:1 + dec_batch].set(c)
    mod_all = _mod_call(c_all, w_mod, b_mod)

    rope = _rope_tables(dec_seq, HEAD_DIM) + _rope_tables(dec_seq, M_ROPE)
    xp = x_prompt.reshape(batch * seq, d)
    xs = x_sample.reshape(dec_batch * dec_seq, d)
    new = []
    for l in range(depth):
        wts = _layer_weights(l, w_in, attn_q_norm, attn_k_norm, mla_q_norm, mla_kv_norm, w_uq, w_ukv,
                             gmlp_ws, gmlp_b, w_o)
        peer_w = (peer_wq[l].astype(_bf16), peer_k1[l].astype(_bf16), peer_k2[l].astype(_bf16),
                  peer_u[l].astype(_bf16), peer_v[l].T.astype(_bf16))
        ln1 = (ln1_g[l][None, :], ln1_b[l][None, :])
        ln2 = (ln2_g[l][None, :], ln2_b[l][None, :])
        mod_ctx = mod_all[l, 0:1][:, None, :]
        mod_lat = mod_all[l, 1:1 + dec_batch][:, None, :]
        cache = (cache_attn_k[:, l].reshape(dec_batch, past, KA_W),
                 cache_attn_v[:, l].reshape(dec_batch, past, KA_W),
                 cache_mla_ckv[:, l],
                 jnp.tile(cache_mla_krope[:, l], (1, 1, M_HEADS)))

        pre = _pre_call(xp, mod_ctx, batch * seq, wts, None, latent=False)
        new.append(pre[7:])
        oa, om = _attn_call(pre[0:6], seq, None, None, latent=False)
        x1, h2 = _post_call(oa, om, pre[6], xp, mod_ctx, batch * seq, wts['wo'], *ln1, alpha)
        xp = _peer_call(h2, x1, mod_ctx, batch * seq, *peer_w, *ln2, alpha)

        pre = _pre_call(xs, mod_lat, dec_seq, wts, rope, latent=True)
        oa, om = _attn_call(pre[0:6], dec_seq, cache, wts['wukv'], latent=True)
        x1, h2 = _post_call(oa, om, pre[6], xs, mod_lat, dec_seq, wts['wo'], *ln1, alpha)
        xs = _peer_call(h2, x1, mod_lat, dec_seq, *peer_w, *ln2, alpha)

    def stack(i, tail):
        return jnp.stack([new[l][i].reshape((batch, seq) + tail) for l in range(depth)], axis=1)

    return (xp.reshape(batch, seq, d), xs.reshape(dec_batch, dec_seq, d),
            stack(0, (A_KV, HEAD_DIM)), stack(1, (A_KV, HEAD_DIM)), stack(2, (M_KV_RANK,)), stack(3, (M_ROPE,)))
```

```python
import functools
import math

import numpy as np
import jax
import jax.numpy as jnp
from jax import lax
from jax.experimental import pallas as pl
from jax.experimental.pallas import tpu as pltpu

A_HEADS = 8
A_KV = 2
HEAD_DIM = 64
M_HEADS = 4
M_Q_RANK = 256
M_KV_RANK = 128
M_NOPE = 64
M_ROPE = 32
M_V = 64
C_GROUPS = 4
C_DIM = 64
C_CHUNK = 128
PEER_HEADS = 8
N_KEYS = 128
PEER_HALF = 128
PEER_TOPK = 16
GRID_W = 64
EPS = 1e-6
ROPE_BASE = 10000.0

LANES = 128
ROWS_B = 16
NEG_BIG = -3.0e38

QA_W = A_HEADS * HEAD_DIM
KA_W = A_KV * HEAD_DIM
CQ_OFF = QA_W + 2 * KA_W
CKV_OFF = CQ_OFF + M_Q_RANK
KR_OFF = CKV_OFF + M_KV_RANK
UC_OFF = KR_OFF + M_HEADS * M_ROPE
CW = C_GROUPS * C_DIM
VC_OFF = UC_OFF + CW
IN_WP = VC_OFF + CW

PRE_TB = 256
ATT_QB = 256
PEER_T = 512
PEER_EB = 1024
VMEM_LIMIT = 56 * 1024 * 1024

_bf16 = jnp.bfloat16
_f32 = jnp.float32


def _dot(a, b):
    return jnp.dot(a, b, preferred_element_type=_f32)


def _dot_nt(a, b):
    return lax.dot_general(a, b, (((1,), (1,)), ((), ())), preferred_element_type=_f32)


def _split(x):
    hi = x.astype(_bf16)
    lo = (x - hi.astype(_f32)).astype(_bf16)
    return hi, lo


def _segsum(x, ones_bd):
    hi, lo = _split(x)
    return _dot(hi, ones_bd) + _dot(lo, ones_bd)


def _ln_rows(x):
    mu = jnp.mean(x, axis=-1, keepdims=True)
    d = x - mu
    var = jnp.mean(d * d, axis=-1, keepdims=True)
    return d * lax.rsqrt(var + EPS)


def _rot(x, cos, sin, half):
    lane = lax.broadcasted_iota(jnp.int32, x.shape, 1)
    up = pltpu.roll(x, LANES - half, axis=1)
    dn = pltpu.roll(x, half, axis=1)
    partner = jnp.where((lane & half) == 0, up, dn)
    return x * cos + partner * sin


def _mod_kernel(c_ref, w_ref, b_ref, o_ref):
    c = c_ref[...]
    a = c * (1.0 / (1.0 + jnp.exp(-c)))
    a_hi, a_lo = _split(a)
    w_hi, w_lo = _split(w_ref[0])
    o_ref[0] = _dot(a_hi, w_hi) + _dot(a_lo, w_hi) + _dot(a_hi, w_lo) + b_ref[0]


def _mod_call(c_all, w_mod, b_mod):
    depth, d, d6 = w_mod.shape
    rows = c_all.shape[0]
    tn = 1024
    return pl.pallas_call(
        _mod_kernel,
        grid=(depth, d6 // tn),
        in_specs=[pl.BlockSpec((rows, d), lambda l, j: (0, 0)),
                  pl.BlockSpec((1, d, tn), lambda l, j: (l, 0, j)),
                  pl.BlockSpec((1, 1, tn), lambda l, j: (l, 0, j))],
        out_specs=pl.BlockSpec((1, rows, tn), lambda l, j: (l, 0, j)),
        out_shape=jax.ShapeDtypeStruct((depth, rows, d6), _f32),
        compiler_params=pltpu.CompilerParams(dimension_semantics=("parallel", "parallel"),
                                             vmem_limit_bytes=VMEM_LIMIT),
        name="mod",
    )(c_all, w_mod, b_mod.reshape(depth, 1, d6))


def _pre_kernel(*refs, latent, d_model):
    if latent:
        (x_ref, mod_ref, win_ref, aqn_ref, akn_ref, mqn_ref, mkvn_ref, wuq_ref, wukv_ref, ws_ref, bsf_ref,
         bd_ref, cosa_ref, sina_ref, cosm_ref, sinm_ref,
         qa_ref, ka_ref, va_ref, qm_ref, kvm_ref, kr_ref, oc_ref) = refs
    else:
        (x_ref, mod_ref, win_ref, aqn_ref, akn_ref, mqn_ref, mkvn_ref, wuq_ref, wukv_ref, ws_ref, bsf_ref,
         bd_ref,
         qa_ref, ka_ref, va_ref, qm_ref, kvm_ref, kr_ref, oc_ref,
         nk_ref, nv_ref, nckv_ref, nkr_ref) = refs
    D = d_model
    x = x_ref[...]
    mod = mod_ref[0]
    sh1 = mod[:, 0:D]
    sc1 = mod[:, D:2 * D]
    h = _ln_rows(x) * (1.0 + sc1) + sh1
    proj = _dot(h.astype(_bf16), win_ref[...])

    qa = proj[:, 0:QA_W]
    ms = _segsum(qa * qa, bd_ref[...]) * (1.0 / HEAD_DIM)
    qa = qa * lax.rsqrt(ms + EPS) * aqn_ref[...]
    ka = proj[:, QA_W:QA_W + KA_W]
    ms = _segsum(ka * ka, bd_ref[0:KA_W, 0:KA_W]) * (1.0 / HEAD_DIM)
    ka = ka * lax.rsqrt(ms + EPS) * akn_ref[...]
    va = proj[:, QA_W + KA_W:CQ_OFF]
    if not latent:
        nk_ref[...] = ka
        nv_ref[...] = va
    for j in range(QA_W // LANES):
        slab = qa[:, j * LANES:(j + 1) * LANES]
        if latent:
            slab = _rot(slab, cosa_ref[...], sina_ref[...], HEAD_DIM // 4)
        qa_ref[:, j * LANES:(j + 1) * LANES] = (slab * (1.0 / math.sqrt(HEAD_DIM))).astype(_bf16)
    if latent:
        ka = _rot(ka, cosa_ref[...], sina_ref[...], HEAD_DIM // 4)
    ka_ref[...] = ka.astype(_bf16)
    va_ref[...] = va.astype(_bf16)

    cq = proj[:, CQ_OFF:CKV_OFF]
    cq = cq * lax.rsqrt(jnp.mean(cq * cq, axis=-1, keepdims=True) + EPS) * mqn_ref[...]
    qm = _dot(cq.astype(_bf16), wuq_ref[...])
    nope_w = M_HEADS * M_NOPE
    qscale = 1.0 / math.sqrt(M_NOPE + M_ROPE)
    qm_ref[:, 0:nope_w] = (qm[:, 0:nope_w] * qscale).astype(_bf16)
    qr = qm[:, nope_w:]
    if latent:
        qr = _rot(qr, cosm_ref[...], sinm_ref[...], M_ROPE // 4)
    qm_ref[:, nope_w:] = (qr * qscale).astype(_bf16)
    ckv = proj[:, CKV_OFF:KR_OFF]
    ckv = ckv * lax.rsqrt(jnp.mean(ckv * ckv, axis=-1, keepdims=True) + EPS) * mkvn_ref[...]
    kvm_ref[...] = _dot(ckv.astype(_bf16), wukv_ref[...]).astype(_bf16)
    kr = proj[:, KR_OFF:UC_OFF]
    if not latent:
        nckv_ref[...] = ckv
        nkr_ref[...] = kr[:, 0:M_ROPE]
    else:
        kr = _rot(kr, cosm_ref[...], sinm_ref[...], M_ROPE // 4)
    kr_ref[...] = kr.astype(_bf16)

    uc = proj[:, UC_OFF:VC_OFF]
    vc = proj[:, VC_OFF:IN_WP]
    bd_c = bd_ref[0:CW, 0:CW]
    mu = _segsum(vc, bd_c) * (1.0 / C_DIM)
    dv = vc - mu
    var = _segsum(dv * dv, bd_c) * (1.0 / C_DIM)
    vg = (dv * lax.rsqrt(var + EPS)).astype(_bf16)
    lane_c = lax.broadcasted_iota(jnp.int32, (C_CHUNK, CW), 1)
    for r in range(x.shape[0] // C_CHUNK):
        rows = slice(r * C_CHUNK, (r + 1) * C_CHUNK)
        vgc = vg[rows]
        mixed = bsf_ref[...]
        for g in range(C_GROUPS):
            mg = _dot(ws_ref[g], vgc)
            mixed = mixed + jnp.where(lane_c // C_DIM == g, mg, 0.0)
        oc_ref[rows, :] = (uc[rows] * mixed).astype(_bf16)


def _pre_call(x2, mod, seq, wts, rope, latent):
    n, d = x2.shape
    tb = PRE_TB
    nblk_seq = seq // tb
    row = lambda i: (i, 0)
    const = lambda i: (0, 0)
    in_specs = [pl.BlockSpec((tb, d), row),
                pl.BlockSpec((1, 1, mod.shape[-1]), lambda i: (i // nblk_seq, 0, 0)),
                pl.BlockSpec(wts['win'].shape, const),
                pl.BlockSpec((1, QA_W), const), pl.BlockSpec((1, KA_W), const),
                pl.BlockSpec((1, M_Q_RANK), const), pl.BlockSpec((1, M_KV_RANK), const),
                pl.BlockSpec(wts['wuq'].shape, const), pl.BlockSpec(wts['wukv'].shape, const),
                pl.BlockSpec(wts['ws'].shape, lambda i: (0, 0, 0)),
                pl.BlockSpec((C_CHUNK, CW), const),
                pl.BlockSpec((QA_W, QA_W), const)]
    args = [x2, mod, wts['win'], wts['aqn'], wts['akn'], wts['mqn'], wts['mkvn'], wts['wuq'], wts['wukv'],
            wts['ws'], wts['bsf'], wts['bd']]
    if latent:
        pos = lambda i: (i % nblk_seq, 0)
        in_specs += [pl.BlockSpec((tb, LANES), pos)] * 4
        args += list(rope)
    widths = [QA_W, KA_W, KA_W, M_HEADS * (M_NOPE + M_ROPE), M_HEADS * (M_NOPE + M_V), M_HEADS * M_ROPE, CW]
    out_specs = [pl.BlockSpec((tb, w), row) for w in widths]
    out_shape = [jax.ShapeDtypeStruct((n, w), _bf16) for w in widths]
    if not latent:
        for w in (KA_W, KA_W, M_KV_RANK, M_ROPE):
            out_specs.append(pl.BlockSpec((tb, w), row))
            out_shape.append(jax.ShapeDtypeStruct((n, w), _f32))
    return pl.pallas_call(
        functools.partial(_pre_kernel, latent=latent, d_model=d),
        grid=(n // tb,),
        in_specs=in_specs, out_specs=out_specs, out_shape=out_shape,
        compiler_params=pltpu.CompilerParams(dimension_semantics=("parallel",), vmem_limit_bytes=VMEM_LIMIT),
        name="pre_lat" if latent else "pre_ctx",
    )(*args)


def _softmax_pv(q, keys, vals):
    s = [_dot_nt(q, k) for k in keys]
    m = s[0].max(axis=-1, keepdims=True)
    for si in s[1:]:
        m = jnp.maximum(m, si.max(axis=-1, keepdims=True))
    den = None
    o = None
    for si, v in zip(s, vals):
        e = jnp.exp(si - m)
        d = e.sum(axis=-1, keepdims=True)
        den = d if den is None else den + d
        pv = _dot(e.astype(_bf16), v)
        o = pv if o is None else o + pv
    return o * (1.0 / den)


def _attn_kernel(*refs, latent):
    if latent:
        (qa_ref, qm_ref, ka_ref, va_ref, kvm_ref, kr_ref, ck_ref, cv_ref, cckv_ref, ckr_ref, wukv_ref,
         oa_ref, om_ref) = refs
    else:
        qa_ref, qm_ref, ka_ref, va_ref, kvm_ref, kr_ref, oa_ref, om_ref = refs
    qb = qa_ref.shape[0]
    nope_w = M_HEADS * M_NOPE

    keys_a = [ka_ref[...]]
    vals_a = [va_ref[...]]
    if latent:
        keys_a = [ck_ref[0].astype(_bf16)] + keys_a
        vals_a = [cv_ref[0].astype(_bf16)] + vals_a
    lane = lax.broadcasted_iota(jnp.int32, (qb, LANES), 1)
    low = lane < HEAD_DIM
    for j in range(QA_W // LANES):
        slab = qa_ref[:, j * LANES:(j + 1) * LANES]
        q2 = jnp.concatenate([jnp.where(low, slab, jnp.zeros_like(slab)),
                              jnp.where(low, jnp.zeros_like(slab), slab)], axis=0)
        o2 = _softmax_pv(q2, keys_a, vals_a)
        oa_ref[:, j * LANES:(j + 1) * LANES] = jnp.where(low, o2[0:qb], o2[qb:]).astype(_bf16)

    kn = [kvm_ref[:, 0:nope_w]]
    vm = [kvm_ref[:, nope_w:]]
    krs = [kr_ref[...]]
    if latent:
        kvc = _dot(cckv_ref[0].astype(_bf16), wukv_ref[...]).astype(_bf16)
        kn = [kvc[:, 0:nope_w]] + kn
        vm = [kvc[:, nope_w:]] + vm
        krs = [ckr_ref[0].astype(_bf16)] + krs
    keys_m = [jnp.concatenate([a, b], axis=1) for a, b in zip(kn, krs)]
    qm = qm_ref[...]
    lane_q = lax.broadcasted_iota(jnp.int32, qm.shape, 1)
    head_q = jnp.where(lane_q < nope_w, lane_q // M_NOPE, (lane_q - nope_w) // M_ROPE)
    lane_o = lax.broadcasted_iota(jnp.int32, (qb, nope_w), 1)
    om = jnp.zeros((qb, nope_w), _f32)
    for p in range(M_HEADS // 2):
        q2 = jnp.concatenate([jnp.where(head_q == 2 * p + t, qm, jnp.zeros_like(qm)) for t in range(2)], axis=0)
        o2 = _softmax_pv(q2, keys_m, vm)
        for t in range(2):
            om = jnp.where(lane_o // M_V == 2 * p + t, o2[t * qb:(t + 1) * qb], om)
    om_ref[...] = om.astype(_bf16)


def _attn_call(pre, seq, cache, wukv, latent):
    qa, ka, va, qm, kvm, kr = pre
    n = qa.shape[0]
    nb = n // seq
    qb = ATT_QB
    nq = seq // qb
    qrow = lambda b, i: (b * nq + i, 0)
    krow = lambda b, i: (b, 0)
    in_specs = [pl.BlockSpec((qb, qa.shape[1]), qrow), pl.BlockSpec((qb, qm.shape[1]), qrow),
                pl.BlockSpec((seq, ka.shape[1]), krow), pl.BlockSpec((seq, va.shape[1]), krow),
                pl.BlockSpec((seq, kvm.shape[1]), krow), pl.BlockSpec((seq, kr.shape[1]), krow)]
    args = [qa, qm, ka, va, kvm, kr]
    if latent:
        for c in cache:
            in_specs.append(pl.BlockSpec((1,) + c.shape[1:], lambda b, i: (b, 0, 0)))
            args.append(c)
        in_specs.append(pl.BlockSpec(wukv.shape, lambda b, i: (0, 0)))
        args.append(wukv)
    return pl.pallas_call(
        functools.partial(_attn_kernel, latent=latent),
        grid=(nb, nq),
        in_specs=in_specs,
        out_specs=[pl.BlockSpec((qb, QA_W), qrow), pl.BlockSpec((qb, M_HEADS * M_V), qrow)],
        out_shape=[jax.ShapeDtypeStruct((n, QA_W), _bf16), jax.ShapeDtypeStruct((n, M_HEADS * M_V), _bf16)],
        compiler_params=pltpu.CompilerParams(dimension_semantics=("parallel", "parallel"),
                                             vmem_limit_bytes=VMEM_LIMIT),
        name="attn_lat" if latent else "attn_ctx",
    )(*args)


def _post_kernel(oa_ref, om_ref, oc_ref, x_ref, mod_ref, wo_ref, g_ref, b_ref, x1_ref, h2_ref, *, alpha, d_model):
    D = d_model
    a_w = oa_ref.shape[1]
    m_w = om_ref.shape[1]
    mix = (_dot(oa_ref[...], wo_ref[0:a_w, :]) + _dot(om_ref[...], wo_ref[a_w:a_w + m_w, :])
           + _dot(oc_ref[...], wo_ref[a_w + m_w:, :]))
    mod = mod_ref[0]
    g1 = mod[:, 2 * D:3 * D]
    sh2 = mod[:, 3 * D:4 * D]
    sc2 = mod[:, 4 * D:5 * D]
    x1 = _ln_rows(alpha * x_ref[...] + g1 * mix) * g_ref[...] + b_ref[...]
    x1_ref[...] = x1
    h2_ref[...] = (_ln_rows(x1) * (1.0 + sc2) + sh2).astype(_bf16)


def _post_call(oa, om, oc, x2, mod, seq, wo, g, b, alpha):
    n, d = x2.shape
    tb = PRE_TB
    nblk_seq = seq // tb
    row = lambda i: (i, 0)
    const = lambda i: (0, 0)
    return pl.pallas_call(
        functools.partial(_post_kernel, alpha=alpha, d_model=d),
        grid=(n // tb,),
        in_specs=[pl.BlockSpec((tb, oa.shape[1]), row), pl.BlockSpec((tb, om.shape[1]), row),
                  pl.BlockSpec((tb, oc.shape[1]), row), pl.BlockSpec((tb, d), row),
                  pl.BlockSpec((1, 1, mod.shape[-1]), lambda i: (i // nblk_seq, 0, 0)),
                  pl.BlockSpec(wo.shape, const), pl.BlockSpec((1, d), const), pl.BlockSpec((1, d), const)],
        out_specs=[pl.BlockSpec((tb, d), row), pl.BlockSpec((tb, d), row)],
        out_shape=[jax.ShapeDtypeStruct((n, d), _f32), jax.ShapeDtypeStruct((n, d), _bf16)],
        compiler_params=pltpu.CompilerParams(dimension_semantics=("parallel",), vmem_limit_bytes=VMEM_LIMIT),
        name="post",
    )(oa, om, oc, x2, mod, wo, g, b)


def _sorting_network(n):
    def merge(lo, hi, r):
        step = r * 2
        if step < hi - lo:
            yield from merge(lo, hi, step)
            yield from merge(lo + r, hi, step)
            yield from ((i, i + r) for i in range(lo + r, hi - r, step))
        else:
            yield (lo, lo + r)

    def sort(lo, hi):
        if hi - lo >= 1:
            mid = lo + (hi - lo) // 2
            yield from sort(lo, mid)
            yield from sort(mid + 1, hi)
            yield from merge(lo, hi, 1)

    return tuple(sort(0, n - 1))


SUBLANES = 8
_SORT16 = _sorting_network(N_KEYS // SUBLANES)


def _extract_top(s, n):
    wires = [s[SUBLANES * i:SUBLANES * (i + 1)] for i in range(s.shape[0] // SUBLANES)]
    wires += [None] * (N_KEYS // SUBLANES - len(wires))
    for a, b in _SORT16:
        x, y = wires[a], wires[b]
        if y is None:
            continue
        if x is None:
            wires[a], wires[b] = y, None
        else:
            wires[a], wires[b] = jnp.maximum(x, y), jnp.minimum(x, y)
    cols = [w for w in wires if w is not None]
    tops = []
    for j in range(n):
        m = jnp.max(cols[0], axis=0, keepdims=True)
        tops.append(m)
        if j + 1 < n:
            top = cols[0] == m
            keep = min(len(cols), n - 1 - j)
            cols = [jnp.where(top, cols[r + 1] if r + 1 < len(cols) else NEG_BIG, cols[r]) for r in range(keep)]
    return tops


def _stack_rows(rows):
    n = len(rows)
    rid = lax.broadcasted_iota(jnp.int32, (n, LANES), 0)
    out = jnp.broadcast_to(rows[0], (n, LANES))
    for j in range(1, n):
        out = jnp.where(rid == j, rows[j], out)
    return out


def _peer_select(s1, s2):
    k = PEER_TOPK
    v1 = _extract_top(s1, k + 1)
    v2 = _extract_top(s2, k + 1)
    rank2 = jnp.zeros_like(s2)
    for j in range(k):
        rank2 = rank2 + jnp.where(s2 < v2[j], 1.0, 0.0)
    col1 = _stack_rows(v1[0:k])
    col2 = _stack_rows(v2[0:k])
    pieces = [v1[0] + col2]
    for a in range(1, 8):
        pieces.append(v1[a] + col2[0:8])
    pieces.append(col1[8:16] + v2[0])
    tail = jnp.where(lax.broadcasted_iota(jnp.int32, (8, LANES), 0) == 0, v1[0] + v2[k],
                     jnp.where(lax.broadcasted_iota(jnp.int32, (8, LANES), 0) == 1, v1[k] + v2[0], NEG_BIG))
    pieces.append(tail)
    cand = jnp.concatenate(pieces, axis=0)
    tops = _extract_top(cand, k + 1)
    z = jnp.ones_like(tops[0])
    for j in range(1, k):
        z = z + jnp.exp(tops[j] - tops[0])
    tau = 0.5 * (tops[k - 1] + tops[k])
    cnt = jnp.zeros_like(s1)
    for b in range(8):
        cnt = cnt + jnp.where(s1 >= tau - v2[b], 1.0, 0.0)
    extra = jnp.zeros_like(tau)
    for b in range(8, k):
        extra = extra + jnp.where(v1[0] >= tau - v2[b], 1.0, 0.0)
    cnt = cnt + jnp.where(s1 >= v1[0], extra, 0.0)
    e1 = jnp.exp(s1 - v1[0]) * (1.0 / z)
    e2 = jnp.exp(s2 - v2[0])
    return rank2, cnt, e1, e2


def _gelu2(x):
    c = math.sqrt(2.0 / math.pi)
    return x * (1.0 + jnp.tanh(x * (c + (c * 0.044715) * (x * x))))


def _peer_kernel(h2_ref, x1_ref, mod_ref, wq_ref, k1_ref, k2_ref, u_ref, vt_ref, g_ref, b_ref, y_ref,
                 s2_ref, re_ref, cnt_ref, e1_ref, a0_ref, a1_ref, p0_ref, p1_ref, acc_ref,
                 *, alpha, d_model, n_tiles):
    D = d_model
    e = pl.program_id(1)
    t = h2_ref.shape[0]
    nchunk = t // LANES
    half = t // 2
    slabs = u_ref.shape[0] // N_KEYS
    a_refs = (a0_ref, a1_ref)
    p_refs = (p0_ref, p1_ref)

    def gate_chunk(off, a_ref, p_ref):
        cols = pl.ds(off, LANES)
        rk_cols = pl.ds(2 * off, LANES)
        e2_cols = pl.ds(2 * off + LANES, LANES)
        for g in range(slabs // 8):
            rows8 = pl.ds(pl.multiple_of((e - 1) * slabs + g * 8, 8), 8)
            cnt8 = [cnt_ref[h, rows8, cols] for h in range(PEER_HEADS)]
            e18 = [e1_ref[h, rows8, cols] for h in range(PEER_HEADS)]
            for k in range(8):
                w = jnp.zeros((N_KEYS, LANES), _bf16)
                for h in range(PEER_HEADS):
                    hit = re_ref[h, :, rk_cols] < cnt8[h][k:k + 1].astype(_bf16)
                    w = w + jnp.where(hit, re_ref[h, :, e2_cols], jnp.zeros((), _bf16)) * e18[h][k:k + 1].astype(_bf16)
                rows = slice((g * 8 + k) * N_KEYS, (g * 8 + k + 1) * N_KEYS)
                p_ref[rows, cols] = _gelu2(a_ref[rows, cols]).astype(_bf16) * w

    def run(parity, stage1, stage2, stage3):
        cur, prev = parity, 1 - parity

        for hf in range(2):
            off = hf * half
            if stage1:
                a_refs[cur][:, pl.ds(off, half)] = _dot_nt(u_ref[...], h2_ref[pl.ds(off, half), :])
            if stage2:
                for c in range(half // LANES):
                    gate_chunk(off + c * LANES, a_refs[prev], p_refs[prev])
            if stage3:
                acc_ref[:, pl.ds(off, half)] += _dot(vt_ref[...], p_refs[cur][:, pl.ds(off, half)])

    @pl.when(e == 0)
    def _select():
        q = _dot(h2_ref[...], wq_ref[...]).astype(_bf16)
        for h in range(PEER_HEADS):
            base = h * 2 * PEER_HALF
            rows = slice(h * N_KEYS, (h + 1) * N_KEYS)
            a1_ref[rows, :] = _dot_nt(k1_ref[...], q[:, base:base + PEER_HALF])
            s2_ref[h] = _dot_nt(k2_ref[...], q[:, base + PEER_HALF:base + 2 * PEER_HALF])

        def body(it, carry):
            h = it // nchunk
            c = pl.multiple_of((it % nchunk) * LANES, LANES)
            r = pl.multiple_of(h * N_KEYS, N_KEYS)
            rank2, cnt, e1, e2 = _peer_select(a1_ref[pl.ds(r, N_KEYS), pl.ds(c, LANES)],
                                              s2_ref[h, :, pl.ds(c, LANES)])
            re_ref[h, :, pl.ds(pl.multiple_of(2 * c, LANES), LANES)] = rank2.astype(_bf16)
            cnt_ref[h, :, pl.ds(c, LANES)] = cnt
            e1_ref[h, :, pl.ds(c, LANES)] = 0.5 * e1
            re_ref[h, :, pl.ds(pl.multiple_of(2 * c + LANES, LANES), LANES)] = e2.astype(_bf16)
            return carry

        lax.fori_loop(0, PEER_HEADS * nchunk, body, 0)
        acc_ref[...] = jnp.zeros_like(acc_ref)
        run(0, True, False, False)

    @pl.when(e == 1)
    def _fill():
        run(1, True, True, False)

    for parity in range(2):
        @pl.when((e >= 2) & (e < n_tiles) & (e % 2 == parity))
        def _steady():
            run(parity, True, True, True)

    @pl.when(e == n_tiles)
    def _drain():
        run(n_tiles % 2, False, True, True)

    @pl.when(e == n_tiles + 1)
    def _finish():
        run((n_tiles + 1) % 2, False, False, True)
        mod = mod_ref[0]
        g2 = mod[:, 5 * D:6 * D]
        ff = acc_ref[...].T
        y_ref[...] = _ln_rows(alpha * x1_ref[...] + g2 * ff) * g_ref[...] + b_ref[...]


def _peer_call(h2, x1, mod, seq, wq, k1, k2, u, vt, g, b, alpha):
    n, d = x1.shape
    t = PEER_T
    eb = PEER_EB
    n_exp = u.shape[0]
    nblk_seq = seq // t
    row = lambda i, e: (i, 0)
    const = lambda i, e: (0, 0)
    n_tiles = n_exp // eb
    assert eb == PEER_HEADS * N_KEYS and n_tiles >= 2
    sel32 = pltpu.VMEM((PEER_HEADS, N_KEYS, t), _f32)
    sel16 = pltpu.VMEM((PEER_HEADS, N_KEYS, 2 * t), _bf16)
    return pl.pallas_call(
        functools.partial(_peer_kernel, alpha=alpha, d_model=d, n_tiles=n_tiles),
        grid=(n // t, n_tiles + 2),
        in_specs=[pl.BlockSpec((t, d), row), pl.BlockSpec((t, d), row),
                  pl.BlockSpec((1, 1, mod.shape[-1]), lambda i, e: (i // nblk_seq, 0, 0)),
                  pl.BlockSpec(wq.shape, const), pl.BlockSpec(k1.shape, const), pl.BlockSpec(k2.shape, const),
                  pl.BlockSpec((eb, d), lambda i, e: (jnp.minimum(e, n_tiles - 1), 0)),
                  pl.BlockSpec((d, eb), lambda i, e: (0, jnp.clip(e - 2, 0, n_tiles - 1))),
                  pl.BlockSpec((1, d), const), pl.BlockSpec((1, d), const)],
        out_specs=pl.BlockSpec((t, d), row),
        out_shape=jax.ShapeDtypeStruct((n, d), _f32),
        scratch_shapes=[sel32, sel16, sel32, sel32,
                        pltpu.VMEM((eb, t), _f32), pltpu.VMEM((eb, t), _f32),
                        pltpu.VMEM((eb, t), _bf16), pltpu.VMEM((eb, t), _bf16), pltpu.VMEM((d, t), _f32)],
        compiler_params=pltpu.CompilerParams(dimension_semantics=("parallel", "arbitrary"),
                                             vmem_limit_bytes=VMEM_LIMIT),
        name="peer",
    )(h2, x1, mod, wq, k1, k2, u, vt, g, b)


def _rope_tables(seq, head_dim):
    m = head_dim // 4
    lane = np.arange(LANES)
    d = lane % head_dim
    use_col = (d // (2 * m)) == 1
    within = d % (2 * m)
    second = within // m
    freqs = ROPE_BASE ** (-np.arange(m, dtype=np.float64) / m)
    f = freqs[within % m]
    tpos = np.arange(seq)
    pos = np.where(use_col[None, :], (tpos % GRID_W)[:, None], (tpos // GRID_W)[:, None]).astype(np.float64)
    ang = pos * f[None, :]
    sign = np.where(second == 0, -1.0, 1.0)
    return jnp.asarray(np.cos(ang), _f32), jnp.asarray(np.sin(ang) * sign[None, :], _f32)


def _layer_weights(l, w_in, attn_q_norm, attn_k_norm, mla_q_norm, mla_kv_norm, w_uq, w_ukv, gmlp_ws, gmlp_b, w_o):
    offs = np.cumsum([0, QA_W, KA_W, KA_W, M_Q_RANK, M_KV_RANK, M_ROPE, CW, CW])
    qa_perm = np.concatenate([np.arange(HEAD_DIM) + HEAD_DIM * (j + 4 * half)
                              for j in range(A_HEADS // 2) for half in range(2)])
    cols = np.concatenate([qa_perm, np.arange(offs[1], offs[5]),
                           np.tile(np.arange(offs[5], offs[6]), M_HEADS), np.arange(offs[6], offs[8])])
    qd = M_NOPE + M_ROPE
    uq_cols = np.concatenate([np.arange(M_NOPE) + h * qd for h in range(M_HEADS)]
                             + [np.arange(M_ROPE) + h * qd + M_NOPE for h in range(M_HEADS)])
    kd = M_NOPE + M_V
    ukv_cols = np.concatenate([np.arange(M_NOPE) + h * kd for h in range(M_HEADS)]
                              + [np.arange(M_V) + h * kd + M_NOPE for h in range(M_HEADS)])
    wo_rows = np.concatenate([qa_perm, np.arange(QA_W, w_o.shape[1])])
    seg = np.arange(QA_W) // HEAD_DIM
    return {
        'win': w_in[l][:, cols].astype(_bf16),
        'aqn': jnp.tile(attn_q_norm[l], A_HEADS)[None, :],
        'akn': jnp.tile(attn_k_norm[l], A_KV)[None, :],
        'mqn': mla_q_norm[l][None, :],
        'mkvn': mla_kv_norm[l][None, :],
        'wuq': w_uq[l][:, uq_cols].astype(_bf16),
        'wukv': w_ukv[l][:, ukv_cols].astype(_bf16),
        'ws': gmlp_ws[l].astype(_bf16),
        'bsf': jnp.repeat(gmlp_b[l].T, C_DIM, axis=1),
        'bd': jnp.asarray(seg[:, None] == seg[None, :], _bf16),
        'wo': w_o[l][wo_rows, :].astype(_bf16),
    }


def kernel(x_prompt, x_sample, cache_attn_k, cache_attn_v, cache_mla_ckv, cache_mla_krope, c, c_ctx, w_mod, b_mod, w_in, attn_q_norm, attn_k_norm, mla_q_norm, mla_kv_norm, w_uq, w_ukv, gmlp_ws, gmlp_b, w_o, ln1_g, ln1_b, ln2_g, ln2_b, peer_wq, peer_k1, peer_k2, peer_u, peer_v):
    batch, seq, d = x_prompt.shape
    dec_batch, dec_seq, _ = x_sample.shape
    depth = w_in.shape[0]
    past = cache_attn_k.shape[2]
    alpha = (2.0 * depth) ** 0.25

    n_rows = 8 * ((1 + dec_batch + 7) // 8)
    c_all = jnp.zeros((n_rows, d), _f32).at[0].set(c_ctx).at[1:1 + dec_batch].set(c)
    mod_all = _mod_call(c_all, w_mod, b_mod)

    rope = _rope_tables(dec_seq, HEAD_DIM) + _rope_tables(dec_seq, M_ROPE)
    xp = x_prompt.reshape(batch * seq, d)
    xs = x_sample.reshape(dec_batch * dec_seq, d)
    new = []
    for l in range(depth):
        wts = _layer_weights(l, w_in, attn_q_norm, attn_k_norm, mla_q_norm, mla_kv_norm, w_uq, w_ukv,
                             gmlp_ws, gmlp_b, w_o)
        peer_w = (peer_wq[l].astype(_bf16), peer_k1[l].astype(_bf16), peer_k2[l].astype(_bf16),
                  peer_u[l].astype(_bf16), peer_v[l].T.astype(_bf16))
        ln1 = (ln1_g[l][None, :], ln1_b[l][None, :])
        ln2 = (ln2_g[l][None, :], ln2_b[l][None, :])
        mod_ctx = mod_all[l, 0:1][:, None, :]
        mod_lat = mod_all[l, 1:1 + dec_batch][:, None, :]
        cache = (cache_attn_k[:, l].reshape(dec_batch, past, KA_W),
                 cache_attn_v[:, l].reshape(dec_batch, past, KA_W),
                 cache_mla_ckv[:, l],
                 jnp.tile(cache_mla_krope[:, l], (1, 1, M_HEADS)))

        pre = _pre_call(xp, mod_ctx, batch * seq, wts, None, latent=False)
        new.append(pre[7:])
        oa, om = _attn_call(pre[0:6], seq, None, None, latent=False)
        x1, h2 = _post_call(oa, om, pre[6], xp, mod_ctx, batch * seq, wts['wo'], *ln1, alpha)
        xp = _peer_call(h2, x1, mod_ctx, batch * seq, *peer_w, *ln2, alpha)

        pre = _pre_call(xs, mod_lat, dec_seq, wts, rope, latent=True)
        oa, om = _attn_call(pre[0:6], dec_seq, cache, wts['wukv'], latent=True)
        x1, h2 = _post_call(oa, om, pre[6], xs, mod_lat, dec_seq, wts['wo'], *ln1, alpha)
        xs = _peer_call(h2, x1, mod_lat, dec_seq, *peer_w, *ln2, alpha)

    def stack(i, tail):
        return jnp.stack([new[l][i].reshape((batch, seq) + tail) for l in range(depth)], axis=1)

    return (xp.reshape(batch, seq, d), xs.reshape(dec_batch, dec_seq, d),
            stack(0, (A_KV, HEAD_DIM)), stack(1, (A_KV, HEAD_DIM)), stack(2, (M_KV_RANK,)), stack(3, (M_ROPE,)))
```

```python
import functools
import math

import numpy as np
import jax
import jax.numpy as jnp
from jax import lax
from jax.experimental import pallas as pl
from jax.experimental.pallas import tpu as pltpu

A_HEADS = 8
A_KV = 2
HEAD_DIM = 64
M_HEADS = 4
M_Q_RANK = 256
M_KV_RANK = 128
M_NOPE = 64
M_ROPE = 32
M_V = 64
C_GROUPS = 4
C_DIM = 64
C_CHUNK = 128
PEER_HEADS = 8
N_KEYS = 128
PEER_HALF = 128
PEER_TOPK = 16
GRID_W = 64
EPS = 1e-6
ROPE_BASE = 10000.0

LANES = 128
ROWS_B = 16
NEG_BIG = -3.0e38

QA_W = A_HEADS * HEAD_DIM
KA_W = A_KV * HEAD_DIM
CQ_OFF = QA_W + 2 * KA_W
CKV_OFF = CQ_OFF + M_Q_RANK
KR_OFF = CKV_OFF + M_KV_RANK
UC_OFF = KR_OFF + M_HEADS * M_ROPE
CW = C_GROUPS * C_DIM
VC_OFF = UC_OFF + CW
IN_WP = VC_OFF + CW

PRE_TB = 256
ATT_QB = 256
PEER_T = 512
PEER_EB = 1024
VMEM_LIMIT = 56 * 1024 * 1024

_bf16 = jnp.bfloat16
_f32 = jnp.float32


def _dot(a, b):
    return jnp.dot(a, b, preferred_element_type=_f32)


def _dot_nt(a, b):
    return lax.dot_general(a, b, (((1,), (1,)), ((), ())), preferred_element_type=_f32)


def _split(x):
    hi = x.astype(_bf16)
    lo = (x - hi.astype(_f32)).astype(_bf16)
    return hi, lo


def _segsum(x, ones_bd):
    hi, lo = _split(x)
    return _dot(hi, ones_bd) + _dot(lo, ones_bd)


def _ln_rows(x):
    mu = jnp.mean(x, axis=-1, keepdims=True)
    d = x - mu
    var = jnp.mean(d * d, axis=-1, keepdims=True)
    return d * lax.rsqrt(var + EPS)


def _rot(x, cos, sin, half):
    lane = lax.broadcasted_iota(jnp.int32, x.shape, 1)
    up = pltpu.roll(x, LANES - half, axis=1)
    dn = pltpu.roll(x, half, axis=1)
    partner = jnp.where((lane & half) == 0, up, dn)
    return x * cos + partner * sin


def _mod_kernel(c_ref, w_ref, b_ref, o_ref):
    c = c_ref[...]
    a = c * (1.0 / (1.0 + jnp.exp(-c)))
    a_hi, a_lo = _split(a)
    w_hi, w_lo = _split(w_ref[0])
    o_ref[0] = _dot(a_hi, w_hi) + _dot(a_lo, w_hi) + _dot(a_hi, w_lo) + b_ref[0]


def _mod_call(c_all, w_mod, b_mod):
    depth, d, d6 = w_mod.shape
    rows = c_all.shape[0]
    tn = 1024
    return pl.pallas_call(
        _mod_kernel,
        grid=(depth, d6 // tn),
        in_specs=[pl.BlockSpec((rows, d), lambda l, j: (0, 0)),
                  pl.BlockSpec((1, d, tn), lambda l, j: (l, 0, j)),
                  pl.BlockSpec((1, 1, tn), lambda l, j: (l, 0, j))],
        out_specs=pl.BlockSpec((1, rows, tn), lambda l, j: (l, 0, j)),
        out_shape=jax.ShapeDtypeStruct((depth, rows, d6), _f32),
        compiler_params=pltpu.CompilerParams(dimension_semantics=("parallel", "parallel"),
                                             vmem_limit_bytes=VMEM_LIMIT),
        name="mod",
    )(c_all, w_mod, b_mod.reshape(depth, 1, d6))


def _pre_kernel(*refs, latent, d_model):
    if latent:
        (x_ref, mod_ref, win_ref, aqn_ref, akn_ref, mqn_ref, mkvn_ref, wuq_ref, wukv_ref, ws_ref, bsf_ref,
         bd_ref, cosa_ref, sina_ref, cosm_ref, sinm_ref,
         qa_ref, ka_ref, va_ref, qm_ref, kvm_ref, kr_ref, oc_ref) = refs
    else:
        (x_ref, mod_ref, win_ref, aqn_ref, akn_ref, mqn_ref, mkvn_ref, wuq_ref, wukv_ref, ws_ref, bsf_ref,
         bd_ref,
         qa_ref, ka_ref, va_ref, qm_ref, kvm_ref, kr_ref, oc_ref,
         nk_ref, nv_ref, nckv_ref, nkr_ref) = refs
    D = d_model
    x = x_ref[...]
    mod = mod_ref[0]
    sh1 = mod[:, 0:D]
    sc1 = mod[:, D:2 * D]
    h = _ln_rows(x) * (1.0 + sc1) + sh1
    proj = _dot(h.astype(_bf16), win_ref[...])

    qa = proj[:, 0:QA_W]
    ms = _segsum(qa * qa, bd_ref[...]) * (1.0 / HEAD_DIM)
    qa = qa * lax.rsqrt(ms + EPS) * aqn_ref[...]
    ka = proj[:, QA_W:QA_W + KA_W]
    ms = _segsum(ka * ka, bd_ref[0:KA_W, 0:KA_W]) * (1.0 / HEAD_DIM)
    ka = ka * lax.rsqrt(ms + EPS) * akn_ref[...]
    va = proj[:, QA_W + KA_W:CQ_OFF]
    if not latent:
        nk_ref[...] = ka
        nv_ref[...] = va
    for j in range(QA_W // LANES):
        slab = qa[:, j * LANES:(j + 1) * LANES]
        if latent:
            slab = _rot(slab, cosa_ref[...], sina_ref[...], HEAD_DIM // 4)
        qa_ref[:, j * LANES:(j + 1) * LANES] = (slab * (1.0 / math.sqrt(HEAD_DIM))).astype(_bf16)
    if latent:
        ka = _rot(ka, cosa_ref[...], sina_ref[...], HEAD_DIM // 4)
    ka_ref[...] = ka.astype(_bf16)
    va_ref[...] = va.astype(_bf16)

    cq = proj[:, CQ_OFF:CKV_OFF]
    cq = cq * lax.rsqrt(jnp.mean(cq * cq, axis=-1, keepdims=True) + EPS) * mqn_ref[...]
    qm = _dot(cq.astype(_bf16), wuq_ref[...])
    nope_w = M_HEADS * M_NOPE
    qscale = 1.0 / math.sqrt(M_NOPE + M_ROPE)
    qm_ref[:, 0:nope_w] = (qm[:, 0:nope_w] * qscale).astype(_bf16)
    qr = qm[:, nope_w:]
    if latent:
        qr = _rot(qr, cosm_ref[...], sinm_ref[...], M_ROPE // 4)
    qm_ref[:, nope_w:] = (qr * qscale).astype(_bf16)
    ckv = proj[:, CKV_OFF:KR_OFF]
    ckv = ckv * lax.rsqrt(jnp.mean(ckv * ckv, axis=-1, keepdims=True) + EPS) * mkvn_ref[...]
    kvm_ref[...] = _dot(ckv.astype(_bf16), wukv_ref[...]).astype(_bf16)
    kr = proj[:, KR_OFF:UC_OFF]
    if not latent:
        nckv_ref[...] = ckv
        nkr_ref[...] = kr[:, 0:M_ROPE]
    else:
        kr = _rot(kr, cosm_ref[...], sinm_ref[...], M_ROPE // 4)
    kr_ref[...] = kr.astype(_bf16)

    uc = proj[:, UC_OFF:VC_OFF]
    vc = proj[:, VC_OFF:IN_WP]
    bd_c = bd_ref[0:CW, 0:CW]
    mu = _segsum(vc, bd_c) * (1.0 / C_DIM)
    dv = vc - mu
    var = _segsum(dv * dv, bd_c) * (1.0 / C_DIM)
    vg = (dv * lax.rsqrt(var + EPS)).astype(_bf16)
    lane_c = lax.broadcasted_iota(jnp.int32, (C_CHUNK, CW), 1)
    for r in range(x.shape[0] // C_CHUNK):
        rows = slice(r * C_CHUNK, (r + 1) * C_CHUNK)
        vgc = vg[rows]
        mixed = bsf_ref[...]
        for g in range(C_GROUPS):
            mg = _dot(ws_ref[g], vgc)
            mixed = mixed + jnp.where(lane_c // C_DIM == g, mg, 0.0)
        oc_ref[rows, :] = (uc[rows] * mixed).astype(_bf16)


def _pre_call(x2, mod, seq, wts, rope, latent):
    n, d = x2.shape
    tb = PRE_TB
    nblk_seq = seq // tb
    row = lambda i: (i, 0)
    const = lambda i: (0, 0)
    in_specs = [pl.BlockSpec((tb, d), row),
                pl.BlockSpec((1, 1, mod.shape[-1]), lambda i: (i // nblk_seq, 0, 0)),
                pl.BlockSpec(wts['win'].shape, const),
                pl.BlockSpec((1, QA_W), const), pl.BlockSpec((1, KA_W), const),
                pl.BlockSpec((1, M_Q_RANK), const), pl.BlockSpec((1, M_KV_RANK), const),
                pl.BlockSpec(wts['wuq'].shape, const), pl.BlockSpec(wts['wukv'].shape, const),
                pl.BlockSpec(wts['ws'].shape, lambda i: (0, 0, 0)),
                pl.BlockSpec((C_CHUNK, CW), const),
                pl.BlockSpec((QA_W, QA_W), const)]
    args = [x2, mod, wts['win'], wts['aqn'], wts['akn'], wts['mqn'], wts['mkvn'], wts['wuq'], wts['wukv'],
            wts['ws'], wts['bsf'], wts['bd']]
    if latent:
        pos = lambda i: (i % nblk_seq, 0)
        in_specs += [pl.BlockSpec((tb, LANES), pos)] * 4
        args += list(rope)
    widths = [QA_W, KA_W, KA_W, M_HEADS * (M_NOPE + M_ROPE), M_HEADS * (M_NOPE + M_V), M_HEADS * M_ROPE, CW]
    out_specs = [pl.BlockSpec((tb, w), row) for w in widths]
    out_shape = [jax.ShapeDtypeStruct((n, w), _bf16) for w in widths]
    if not latent:
        for w in (KA_W, KA_W, M_KV_RANK, M_ROPE):
            out_specs.append(pl.BlockSpec((tb, w), row))
            out_shape.append(jax.ShapeDtypeStruct((n, w), _f32))
    return pl.pallas_call(
        functools.partial(_pre_kernel, latent=latent, d_model=d),
        grid=(n // tb,),
        in_specs=in_specs, out_specs=out_specs, out_shape=out_shape,
        compiler_params=pltpu.CompilerParams(dimension_semantics=("parallel",), vmem_limit_bytes=VMEM_LIMIT),
        name="pre_lat" if latent else "pre_ctx",
    )(*args)


def _softmax_pv(q, keys, vals):
    s = [_dot_nt(q, k) for k in keys]
    m = s[0].max(axis=-1, keepdims=True)
    for si in s[1:]:
        m = jnp.maximum(m, si.max(axis=-1, keepdims=True))
    den = None
    o = None
    for si, v in zip(s, vals):
        e = jnp.exp(si - m)
        d = e.sum(axis=-1, keepdims=True)
        den = d if den is None else den + d
        pv = _dot(e.astype(_bf16), v)
        o = pv if o is None else o + pv
    return o * (1.0 / den)


def _attn_kernel(*refs, latent):
    if latent:
        (qa_ref, qm_ref, ka_ref, va_ref, kvm_ref, kr_ref, ck_ref, cv_ref, cckv_ref, ckr_ref, wukv_ref,
         oa_ref, om_ref) = refs
    else:
        qa_ref, qm_ref, ka_ref, va_ref, kvm_ref, kr_ref, oa_ref, om_ref = refs
    qb = qa_ref.shape[0]
    nope_w = M_HEADS * M_NOPE

    keys_a = [ka_ref[...]]
    vals_a = [va_ref[...]]
    if latent:
        keys_a = [ck_ref[0].astype(_bf16)] + keys_a
        vals_a = [cv_ref[0].astype(_bf16)] + vals_a
    lane = lax.broadcasted_iota(jnp.int32, (qb, LANES), 1)
    low = lane < HEAD_DIM
    for j in range(QA_W // LANES):
        slab = qa_ref[:, j * LANES:(j + 1) * LANES]
        q2 = jnp.concatenate([jnp.where(low, slab, jnp.zeros_like(slab)),
                              jnp.where(low, jnp.zeros_like(slab), slab)], axis=0)
        o2 = _softmax_pv(q2, keys_a, vals_a)
        oa_ref[:, j * LANES:(j + 1) * LANES] = jnp.where(low, o2[0:qb], o2[qb:]).astype(_bf16)

    kn = [kvm_ref[:, 0:nope_w]]
    vm = [kvm_ref[:, nope_w:]]
    krs = [kr_ref[...]]
    if latent:
        kvc = _dot(cckv_ref[0].astype(_bf16), wukv_ref[...]).astype(_bf16)
        kn = [kvc[:, 0:nope_w]] + kn
        vm = [kvc[:, nope_w:]] + vm
        krs = [ckr_ref[0].astype(_bf16)] + krs
    keys_m = [jnp.concatenate([a, b], axis=1) for a, b in zip(kn, krs)]
    qm = qm_ref[...]
    lane_q = lax.broadcasted_iota(jnp.int32, qm.shape, 1)
    head_q = jnp.where(lane_q < nope_w, lane_q // M_NOPE, (lane_q - nope_w) // M_ROPE)
    lane_o = lax.broadcasted_iota(jnp.int32, (qb, nope_w), 1)
    om = jnp.zeros((qb, nope_w), _f32)
    for p in range(M_HEADS // 2):
        q2 = jnp.concatenate([jnp.where(head_q == 2 * p + t, qm, jnp.zeros_like(qm)) for t in range(2)], axis=0)
        o2 = _softmax_pv(q2, keys_m, vm)
        for t in range(2):
            om = jnp.where(lane_o // M_V == 2 * p + t, o2[t * qb:(t + 1) * qb], om)
    om_ref[...] = om.astype(_bf16)


def _attn_call(pre, seq, cache, wukv, latent):
    qa, ka, va, qm, kvm, kr = pre
    n = qa.shape[0]
    nb = n // seq
    qb = ATT_QB
    nq = seq // qb
    qrow = lambda b, i: (b * nq + i, 0)
    krow = lambda b, i: (b, 0)
    in_specs = [pl.BlockSpec((qb, qa.shape[1]), qrow), pl.BlockSpec((qb, qm.shape[1]), qrow),
                pl.BlockSpec((seq, ka.shape[1]), krow), pl.BlockSpec((seq, va.shape[1]), krow),
                pl.BlockSpec((seq, kvm.shape[1]), krow), pl.BlockSpec((seq, kr.shape[1]), krow)]
    args = [qa, qm, ka, va, kvm, kr]
    if latent:
        for c in cache:
            in_specs.append(pl.BlockSpec((1,) + c.shape[1:], lambda b, i: (b, 0, 0)))
            args.append(c)
        in_specs.append(pl.BlockSpec(wukv.shape, lambda b, i: (0, 0)))
        args.append(wukv)
    return pl.pallas_call(
        functools.partial(_attn_kernel, latent=latent),
        grid=(nb, nq),
        in_specs=in_specs,
        out_specs=[pl.BlockSpec((qb, QA_W), qrow), pl.BlockSpec((qb, M_HEADS * M_V), qrow)],
        out_shape=[jax.ShapeDtypeStruct((n, QA_W), _bf16), jax.ShapeDtypeStruct((n, M_HEADS * M_V), _bf16)],
        compiler_params=pltpu.CompilerParams(dimension_semantics=("parallel", "parallel"),
                                             vmem_limit_bytes=VMEM_LIMIT),
        name="attn_lat" if latent else "attn_ctx",
    )(*args)


def _post_kernel(oa_ref, om_ref, oc_ref, x_ref, mod_ref, wo_ref, g_ref, b_ref, x1_ref, h2_ref, *, alpha, d_model):
    D = d_model
    a_w = oa_ref.shape[1]
    m_w = om_ref.shape[1]
    mix = (_dot(oa_ref[...], wo_ref[0:a_w, :]) + _dot(om_ref[...], wo_ref[a_w:a_w + m_w, :])
           + _dot(oc_ref[...], wo_ref[a_w + m_w:, :]))
    mod = mod_ref[0]
    g1 = mod[:, 2 * D:3 * D]
    sh2 = mod[:, 3 * D:4 * D]
    sc2 = mod[:, 4 * D:5 * D]
    x1 = _ln_rows(alpha * x_ref[...] + g1 * mix) * g_ref[...] + b_ref[...]
    x1_ref[...] = x1
    h2_ref[...] = (_ln_rows(x1) * (1.0 + sc2) + sh2).astype(_bf16)


def _post_call(oa, om, oc, x2, mod, seq, wo, g, b, alpha):
    n, d = x2.shape
    tb = PRE_TB
    nblk_seq = seq // tb
    row = lambda i: (i, 0)
    const = lambda i: (0, 0)
    return pl.pallas_call(
        functools.partial(_post_kernel, alpha=alpha, d_model=d),
        grid=(n // tb,),
        in_specs=[pl.BlockSpec((tb, oa.shape[1]), row), pl.BlockSpec((tb, om.shape[1]), row),
                  pl.BlockSpec((tb, oc.shape[1]), row), pl.BlockSpec((tb, d), row),
                  pl.BlockSpec((1, 1, mod.shape[-1]), lambda i: (i // nblk_seq, 0, 0)),
                  pl.BlockSpec(wo.shape, const), pl.BlockSpec((1, d), const), pl.BlockSpec((1, d), const)],
        out_specs=[pl.BlockSpec((tb, d), row), pl.BlockSpec((tb, d), row)],
        out_shape=[jax.ShapeDtypeStruct((n, d), _f32), jax.ShapeDtypeStruct((n, d), _bf16)],
        compiler_params=pltpu.CompilerParams(dimension_semantics=("parallel",), vmem_limit_bytes=VMEM_LIMIT),
        name="post",
    )(oa, om, oc, x2, mod, wo, g, b)


def _sorting_network(n):
    def merge(lo, hi, r):
        step = r * 2
        if step < hi - lo:
            yield from merge(lo, hi, step)
            yield from merge(lo + r, hi, step)
            yield from ((i, i + r) for i in range(lo + r, hi - r, step))
        else:
            yield (lo, lo + r)

    def sort(lo, hi):
        if hi - lo >= 1:
            mid = lo + (hi - lo) // 2
            yield from sort(lo, mid)
            yield from sort(mid + 1, hi)
            yield from merge(lo, hi, 1)

    return tuple(sort(0, n - 1))


SUBLANES = 8
_SORT16 = _sorting_network(N_KEYS // SUBLANES)


def _extract_top(s, n):
    wires = [s[SUBLANES * i:SUBLANES * (i + 1)] for i in range(s.shape[0] // SUBLANES)]
    wires += [None] * (N_KEYS // SUBLANES - len(wires))
    for a, b in _SORT16:
        x, y = wires[a], wires[b]
        if y is None:
            continue
        if x is None:
            wires[a], wires[b] = y, None
        else:
            wires[a], wires[b] = jnp.maximum(x, y), jnp.minimum(x, y)
    cols = [w for w in wires if w is not None]
    tops = []
    for j in range(n):
        m = jnp.max(cols[0], axis=0, keepdims=True)
        tops.append(m)
        if j + 1 < n:
            top = cols[0] == m
            keep = min(len(cols), n - 1 - j)
            cols = [jnp.where(top, cols[r + 1] if r + 1 < len(cols) else NEG_BIG, cols[r]) for r in range(keep)]
    return tops


def _stack_rows(rows):
    n = len(rows)
    rid = lax.broadcasted_iota(jnp.int32, (n, LANES), 0)
    out = jnp.broadcast_to(rows[0], (n, LANES))
    for j in range(1, n):
        out = jnp.where(rid == j, rows[j], out)
    return out


def _peer_select(s1, s2):
    k = PEER_TOPK
    v1 = _extract_top(s1, k + 1)
    v2 = _extract_top(s2, k + 1)
    rank2 = jnp.zeros_like(s2)
    for j in range(k):
        rank2 = rank2 + jnp.where(s2 < v2[j], 1.0, 0.0)
    col1 = _stack_rows(v1[0:k])
    col2 = _stack_rows(v2[0:k])
    pieces = [v1[0] + col2]
    for a in range(1, 8):
        pieces.append(v1[a] + col2[0:8])
    pieces.append(col1[8:16] + v2[0])
    tail = jnp.where(lax.broadcasted_iota(jnp.int32, (8, LANES), 0) == 0, v1[0] + v2[k],
                     jnp.where(lax.broadcasted_iota(jnp.int32, (8, LANES), 0) == 1, v1[k] + v2[0], NEG_BIG))
    pieces.append(tail)
    cand = jnp.concatenate(pieces, axis=0)
    tops = _extract_top(cand, k + 1)
    z = jnp.ones_like(tops[0])
    for j in range(1, k):
        z = z + jnp.exp(tops[j] - tops[0])
    tau = 0.5 * (tops[k - 1] + tops[k])
    cnt = jnp.zeros_like(s1)
    for b in range(8):
        cnt = cnt + jnp.where(s1 >= tau - v2[b], 1.0, 0.0)
    extra = jnp.zeros_like(tau)
    for b in range(8, k):
        extra = extra + jnp.where(v1[0] >= tau - v2[b], 1.0, 0.0)
    cnt = cnt + jnp.where(s1 >= v1[0], extra, 0.0)
    e1 = jnp.exp(s1 - v1[0]) * (1.0 / z)
    e2 = jnp.exp(s2 - v2[0])
    return rank2, cnt, e1, e2


def _gelu2(x):
    c = math.sqrt(2.0 / math.pi)
    return x * (1.0 + jnp.tanh(x * (c + (c * 0.044715) * (x * x))))


def _peer_kernel(h2_ref, x1_ref, mod_ref, wq_ref, k1_ref, k2_ref, u_ref, vt_ref, g_ref, b_ref, y_ref,
                 s2_ref, re_ref, cnt_ref, e1_ref, a0_ref, a1_ref, p0_ref, p1_ref, acc_ref,
                 *, alpha, d_model, n_tiles):
    D = d_model
    e = pl.program_id(1)
    t = h2_ref.shape[0]
    nchunk = t // LANES
    half = t // 2
    slabs = u_ref.shape[0] // N_KEYS
    a_refs = (a0_ref, a1_ref)
    p_refs = (p0_ref, p1_ref)

    def gate_chunk(off, a_ref, p_ref):
        cols = pl.ds(off, LANES)
        rk_cols = pl.ds(2 * off, LANES)
        e2_cols = pl.ds(2 * off + LANES, LANES)
        for g in range(slabs // 8):
            rows8 = pl.ds(pl.multiple_of(e * slabs + g * 8, 8), 8)
            cnt8 = [cnt_ref[h, rows8, cols] for h in range(PEER_HEADS)]
            e18 = [e1_ref[h, rows8, cols] for h in range(PEER_HEADS)]
            for k in range(8):
                w = jnp.zeros((N_KEYS, LANES), _bf16)
                for h in range(PEER_HEADS):
                    hit = re_ref[h, :, rk_cols] < cnt8[h][k:k + 1].astype(_bf16)
                    w = w + jnp.where(hit, re_ref[h, :, e2_cols], jnp.zeros((), _bf16)) * e18[h][k:k + 1].astype(_bf16)
                rows = slice((g * 8 + k) * N_KEYS, (g * 8 + k + 1) * N_KEYS)
                p_ref[rows, cols] = _gelu2(a_ref[rows, cols]).astype(_bf16) * w

    def run(parity, front, back):
        cur, prev = parity, 1 - parity
        for hf in range(2):
            off = hf * half
            if front:
                a_refs[cur][:, pl.ds(off, half)] = _dot_nt(u_ref[...], h2_ref[pl.ds(off, half), :])
                for c in range(half // LANES):
                    gate_chunk(off + c * LANES, a_refs[cur], p_refs[cur])
            if back:
                acc_ref[:, pl.ds(off, half)] += _dot(vt_ref[...], p_refs[prev][:, pl.ds(off, half)])

    @pl.when(e == 0)
    def _select():
        q = _dot(h2_ref[...], wq_ref[...]).astype(_bf16)
        for h in range(PEER_HEADS):
            base = h * 2 * PEER_HALF
            rows = slice(h * N_KEYS, (h + 1) * N_KEYS)
            a1_ref[rows, :] = _dot_nt(k1_ref[...], q[:, base:base + PEER_HALF])
            s2_ref[h] = _dot_nt(k2_ref[...], q[:, base + PEER_HALF:base + 2 * PEER_HALF])

        def body(it, carry):
            h = it // nchunk
            c = pl.multiple_of((it % nchunk) * LANES, LANES)
            r = pl.multiple_of(h * N_KEYS, N_KEYS)
            rank2, cnt, e1, e2 = _peer_select(a1_ref[pl.ds(r, N_KEYS), pl.ds(c, LANES)],
                                              s2_ref[h, :, pl.ds(c, LANES)])
            re_ref[h, :, pl.ds(pl.multiple_of(2 * c, LANES), LANES)] = rank2.astype(_bf16)
            cnt_ref[h, :, pl.ds(c, LANES)] = cnt
            e1_ref[h, :, pl.ds(c, LANES)] = 0.5 * e1
            re_ref[h, :, pl.ds(pl.multiple_of(2 * c + LANES, LANES), LANES)] = e2.astype(_bf16)
            return carry

        lax.fori_loop(0, PEER_HEADS * nchunk, body, 0)
        acc_ref[...] = jnp.zeros_like(acc_ref)
        run(0, True, False)

    for parity in range(2):
        @pl.when((e >= 1) & (e < n_tiles) & (e % 2 == parity))
        def _steady():
            run(parity, True, True)

    @pl.when(e == n_tiles)
    def _finish():
        run(n_tiles % 2, False, True)
        mod = mod_ref[0]
        g2 = mod[:, 5 * D:6 * D]
        ff = acc_ref[...].T
        y_ref[...] = _ln_rows(alpha * x1_ref[...] + g2 * ff) * g_ref[...] + b_ref[...]


def _peer_call(h2, x1, mod, seq, wq, k1, k2, u, vt, g, b, alpha):
    n, d = x1.shape
    t = PEER_T
    eb = PEER_EB
    n_exp = u.shape[0]
    nblk_seq = seq // t
    row = lambda i, e: (i, 0)
    const = lambda i, e: (0, 0)
    n_tiles = n_exp // eb
    assert eb == PEER_HEADS * N_KEYS and n_tiles >= 2
    sel32 = pltpu.VMEM((PEER_HEADS, N_KEYS, t), _f32)
    sel16 = pltpu.VMEM((PEER_HEADS, N_KEYS, 2 * t), _bf16)
    return pl.pallas_call(
        functools.partial(_peer_kernel, alpha=alpha, d_model=d, n_tiles=n_tiles),
        grid=(n // t, n_tiles + 1),
        in_specs=[pl.BlockSpec((t, d), row), pl.BlockSpec((t, d), row),
                  pl.BlockSpec((1, 1, mod.shape[-1]), lambda i, e: (i // nblk_seq, 0, 0)),
                  pl.BlockSpec(wq.shape, const), pl.BlockSpec(k1.shape, const), pl.BlockSpec(k2.shape, const),
                  pl.BlockSpec((eb, d), lambda i, e: (jnp.minimum(e, n_tiles - 1), 0)),
                  pl.BlockSpec((d, eb), lambda i, e: (0, jnp.maximum(e - 1, 0))),
                  pl.BlockSpec((1, d), const), pl.BlockSpec((1, d), const)],
        out_specs=pl.BlockSpec((t, d), row),
        out_shape=jax.ShapeDtypeStruct((n, d), _f32),
        scratch_shapes=[sel32, sel16, sel32, sel32,
                        pltpu.VMEM((eb, t), _f32), pltpu.VMEM((eb, t), _f32),
                        pltpu.VMEM((eb, t), _bf16), pltpu.VMEM((eb, t), _bf16), pltpu.VMEM((d, t), _f32)],
        compiler_params=pltpu.CompilerParams(dimension_semantics=("parallel", "arbitrary"),
                                             vmem_limit_bytes=VMEM_LIMIT),
        name="peer",
    )(h2, x1, mod, wq, k1, k2, u, vt, g, b)


def _rope_tables(seq, head_dim):
    m = head_dim // 4
    lane = np.arange(LANES)
    d = lane % head_dim
    use_col = (d // (2 * m)) == 1
    within = d % (2 * m)
    second = within // m
    freqs = ROPE_BASE ** (-np.arange(m, dtype=np.float64) / m)
    f = freqs[within % m]
    tpos = np.arange(seq)
    pos = np.where(use_col[None, :], (tpos % GRID_W)[:, None], (tpos // GRID_W)[:, None]).astype(np.float64)
    ang = pos * f[None, :]
    sign = np.where(second == 0, -1.0, 1.0)
    return jnp.asarray(np.cos(ang), _f32), jnp.asarray(np.sin(ang) * sign[None, :], _f32)


def _layer_weights(l, w_in, attn_q_norm, attn_k_norm, mla_q_norm, mla_kv_norm, w_uq, w_ukv, gmlp_ws, gmlp_b, w_o):
    offs = np.cumsum([0, QA_W, KA_W, KA_W, M_Q_RANK, M_KV_RANK, M_ROPE, CW, CW])
    qa_perm = np.concatenate([np.arange(HEAD_DIM) + HEAD_DIM * (j + 4 * half)
                              for j in range(A_HEADS // 2) for half in range(2)])
    cols = np.concatenate([qa_perm, np.arange(offs[1], offs[5]),
                           np.tile(np.arange(offs[5], offs[6]), M_HEADS), np.arange(offs[6], offs[8])])
    qd = M_NOPE + M_ROPE
    uq_cols = np.concatenate([np.arange(M_NOPE) + h * qd for h in range(M_HEADS)]
                             + [np.arange(M_ROPE) + h * qd + M_NOPE for h in range(M_HEADS)])
    kd = M_NOPE + M_V
    ukv_cols = np.concatenate([np.arange(M_NOPE) + h * kd for h in range(M_HEADS)]
                              + [np.arange(M_V) + h * kd + M_NOPE for h in range(M_HEADS)])
    wo_rows = np.concatenate([qa_perm, np.arange(QA_W, w_o.shape[1])])
    seg = np.arange(QA_W) // HEAD_DIM
    return {
        'win': w_in[l][:, cols].astype(_bf16),
        'aqn': jnp.tile(attn_q_norm[l], A_HEADS)[None, :],
        'akn': jnp.tile(attn_k_norm[l], A_KV)[None, :],
        'mqn': mla_q_norm[l][None, :],
        'mkvn': mla_kv_norm[l][None, :],
        'wuq': w_uq[l][:, uq_cols].astype(_bf16),
        'wukv': w_ukv[l][:, ukv_cols].astype(_bf16),
        'ws': gmlp_ws[l].astype(_bf16),
        'bsf': jnp.repeat(gmlp_b[l].T, C_DIM, axis=1),
        'bd': jnp.asarray(seg[:, None] == seg[None, :], _bf16),
        'wo': w_o[l][wo_rows, :].astype(_bf16),
    }


def kernel(x_prompt, x_sample, cache_attn_k, cache_attn_v, cache_mla_ckv, cache_mla_krope, c, c_ctx, w_mod, b_mod, w_in, attn_q_norm, attn_k_norm, mla_q_norm, mla_kv_norm, w_uq, w_ukv, gmlp_ws, gmlp_b, w_o, ln1_g, ln1_b, ln2_g, ln2_b, peer_wq, peer_k1, peer_k2, peer_u, peer_v):
    batch, seq, d = x_prompt.shape
    dec_batch, dec_seq, _ = x_sample.shape
    depth = w_in.shape[0]
    past = cache_attn_k.shape[2]
    alpha = (2.0 * depth) ** 0.25

    n_rows = 8 * ((1 + dec_batch + 7) // 8)
    c_all = jnp.zeros((n_rows, d), _f32).at[0].set(c_ctx).at[1:1 + dec_batch].set(c)
    mod_all = _mod_call(c_all, w_mod, b_mod)

    rope = _rope_tables(dec_seq, HEAD_DIM) + _rope_tables(dec_seq, M_ROPE)
    xp = x_prompt.reshape(batch * seq, d)
    xs = x_sample.reshape(dec_batch * dec_seq, d)
    new = []
    for l in range(depth):
        wts = _layer_weights(l, w_in, attn_q_norm, attn_k_norm, mla_q_norm, mla_kv_norm, w_uq, w_ukv,
                             gmlp_ws, gmlp_b, w_o)
        peer_w = (peer_wq[l].astype(_bf16), peer_k1[l].astype(_bf16), peer_k2[l].astype(_bf16),
                  peer_u[l].astype(_bf16), peer_v[l].T.astype(_bf16))
        ln1 = (ln1_g[l][None, :], ln1_b[l][None, :])
        ln2 = (ln2_g[l][None, :], ln2_b[l][None, :])
        mod_ctx = mod_all[l, 0:1][:, None, :]
        mod_lat = mod_all[l, 1:1 + dec_batch][:, None, :]
        cache = (cache_attn_k[:, l].reshape(dec_batch, past, KA_W),
                 cache_attn_v[:, l].reshape(dec_batch, past, KA_W),
                 cache_mla_ckv[:, l],
                 jnp.tile(cache_mla_krope[:, l], (1, 1, M_HEADS)))

        pre = _pre_call(xp, mod_ctx, batch * seq, wts, None, latent=False)
        new.append(pre[7:])
        oa, om = _attn_call(pre[0:6], seq, None, None, latent=False)
        x1, h2 = _post_call(oa, om, pre[6], xp, mod_ctx, batch * seq, wts['wo'], *ln1, alpha)
        xp = _peer_call(h2, x1, mod_ctx, batch * seq, *peer_w, *ln2, alpha)

        pre = _pre_call(xs, mod_lat, dec_seq, wts, rope, latent=True)
        oa, om = _attn_call(pre[0:6], dec_seq, cache, wts['wukv'], latent=True)
        x1, h2 = _post_call(oa, om, pre[6], xs, mod_lat, dec_seq, wts['wo'], *ln1, alpha)
        xs = _peer_call(h2, x1, mod_lat, dec_seq, *peer_w, *ln2, alpha)

    def stack(i, tail):
        return jnp.stack([new[l][i].reshape((batch, seq) + tail) for l in range(depth)], axis=1)

    return (xp.reshape(batch, seq, d), xs.reshape(dec_batch, dec_seq, d),
            stack(0, (A_KV, HEAD_DIM)), stack(1, (A_KV, HEAD_DIM)), stack(2, (M_KV_RANK,)), stack(3, (M_ROPE,)))
```

```python
import functools
import math

import numpy as np
import jax
import jax.numpy as jnp
from jax import lax
from jax.experimental import pallas as pl
from jax.experimental.pallas import tpu as pltpu

A_HEADS = 8
A_KV = 2
HEAD_DIM = 64
M_HEADS = 4
M_Q_RANK = 256
M_KV_RANK = 128
M_NOPE = 64
M_ROPE = 32
M_V = 64
C_GROUPS = 4
C_DIM = 64
C_CHUNK = 128
PEER_HEADS = 8
N_KEYS = 128
PEER_HALF = 128
PEER_TOPK = 16
GRID_W = 64
EPS = 1e-6
ROPE_BASE = 10000.0

LANES = 128
ROWS_B = 16
NEG_BIG = -3.0e38

QA_W = A_HEADS * HEAD_DIM
KA_W = A_KV * HEAD_DIM
CQ_OFF = QA_W + 2 * KA_W
CKV_OFF = CQ_OFF + M_Q_RANK
KR_OFF = CKV_OFF + M_KV_RANK
UC_OFF = KR_OFF + M_HEADS * M_ROPE
CW = C_GROUPS * C_DIM
VC_OFF = UC_OFF + CW
IN_WP = VC_OFF + CW

PRE_TB = 256
ATT_QB = 256
PEER_T = 512
PEER_EB = 1024
VMEM_LIMIT = 56 * 1024 * 1024

_bf16 = jnp.bfloat16
_f32 = jnp.float32


def _dot(a, b):
    return jnp.dot(a, b, preferred_element_type=_f32)


def _dot_nt(a, b):
    return lax.dot_general(a, b, (((1,), (1,)), ((), ())), preferred_element_type=_f32)


def _split(x):
    hi = x.astype(_bf16)
    lo = (x - hi.astype(_f32)).astype(_bf16)
    return hi, lo


def _segsum(x, ones_bd):
    hi, lo = _split(x)
    return _dot(hi, ones_bd) + _dot(lo, ones_bd)


def _ln_rows(x):
    mu = jnp.mean(x, axis=-1, keepdims=True)
    d = x - mu
    var = jnp.mean(d * d, axis=-1, keepdims=True)
    return d * lax.rsqrt(var + EPS)


def _rot(x, cos, sin, half):
    lane = lax.broadcasted_iota(jnp.int32, x.shape, 1)
    up = pltpu.roll(x, LANES - half, axis=1)
    dn = pltpu.roll(x, half, axis=1)
    partner = jnp.where((lane & half) == 0, up, dn)
    return x * cos + partner * sin


def _mod_kernel(c_ref, w_ref, b_ref, o_ref):
    c = c_ref[...]
    a = c * (1.0 / (1.0 + jnp.exp(-c)))
    a_hi, a_lo = _split(a)
    w_hi, w_lo = _split(w_ref[0])
    o_ref[0] = _dot(a_hi, w_hi) + _dot(a_lo, w_hi) + _dot(a_hi, w_lo) + b_ref[0]


def _mod_call(c_all, w_mod, b_mod):
    depth, d, d6 = w_mod.shape
    rows = c_all.shape[0]
    tn = 1024
    return pl.pallas_call(
        _mod_kernel,
        grid=(depth, d6 // tn),
        in_specs=[pl.BlockSpec((rows, d), lambda l, j: (0, 0)),
                  pl.BlockSpec((1, d, tn), lambda l, j: (l, 0, j)),
                  pl.BlockSpec((1, 1, tn), lambda l, j: (l, 0, j))],
        out_specs=pl.BlockSpec((1, rows, tn), lambda l, j: (l, 0, j)),
        out_shape=jax.ShapeDtypeStruct((depth, rows, d6), _f32),
        compiler_params=pltpu.CompilerParams(dimension_semantics=("parallel", "parallel"),
                                             vmem_limit_bytes=VMEM_LIMIT),
        name="mod",
    )(c_all, w_mod, b_mod.reshape(depth, 1, d6))


def _pre_kernel(*refs, latent, d_model):
    if latent:
        (x_ref, mod_ref, win_ref, aqn_ref, akn_ref, mqn_ref, mkvn_ref, wuq_ref, wukv_ref, ws_ref, bsf_ref,
         bd_ref, cosa_ref, sina_ref, cosm_ref, sinm_ref,
         qa_ref, ka_ref, va_ref, qm_ref, kvm_ref, kr_ref, oc_ref) = refs
    else:
        (x_ref, mod_ref, win_ref, aqn_ref, akn_ref, mqn_ref, mkvn_ref, wuq_ref, wukv_ref, ws_ref, bsf_ref,
         bd_ref,
         qa_ref, ka_ref, va_ref, qm_ref, kvm_ref, kr_ref, oc_ref,
         nk_ref, nv_ref, nckv_ref, nkr_ref) = refs
    D = d_model
    x = x_ref[...]
    mod = mod_ref[0]
    sh1 = mod[:, 0:D]
    sc1 = mod[:, D:2 * D]
    h = _ln_rows(x) * (1.0 + sc1) + sh1
    proj = _dot(h.astype(_bf16), win_ref[...])

    qa = proj[:, 0:QA_W]
    ms = _segsum(qa * qa, bd_ref[...]) * (1.0 / HEAD_DIM)
    qa = qa * lax.rsqrt(ms + EPS) * aqn_ref[...]
    ka = proj[:, QA_W:QA_W + KA_W]
    ms = _segsum(ka * ka, bd_ref[0:KA_W, 0:KA_W]) * (1.0 / HEAD_DIM)
    ka = ka * lax.rsqrt(ms + EPS) * akn_ref[...]
    va = proj[:, QA_W + KA_W:CQ_OFF]
    if not latent:
        nk_ref[...] = ka
        nv_ref[...] = va
    for j in range(QA_W // LANES):
        slab = qa[:, j * LANES:(j + 1) * LANES]
        if latent:
            slab = _rot(slab, cosa_ref[...], sina_ref[...], HEAD_DIM // 4)
        qa_ref[:, j * LANES:(j + 1) * LANES] = (slab * (1.0 / math.sqrt(HEAD_DIM))).astype(_bf16)
    if latent:
        ka = _rot(ka, cosa_ref[...], sina_ref[...], HEAD_DIM // 4)
    ka_ref[...] = ka.astype(_bf16)
    va_ref[...] = va.astype(_bf16)

    cq = proj[:, CQ_OFF:CKV_OFF]
    cq = cq * lax.rsqrt(jnp.mean(cq * cq, axis=-1, keepdims=True) + EPS) * mqn_ref[...]
    qm = _dot(cq.astype(_bf16), wuq_ref[...])
    nope_w = M_HEADS * M_NOPE
    qscale = 1.0 / math.sqrt(M_NOPE + M_ROPE)
    qm_ref[:, 0:nope_w] = (qm[:, 0:nope_w] * qscale).astype(_bf16)
    qr = qm[:, nope_w:]
    if latent:
        qr = _rot(qr, cosm_ref[...], sinm_ref[...], M_ROPE // 4)
    qm_ref[:, nope_w:] = (qr * qscale).astype(_bf16)
    ckv = proj[:, CKV_OFF:KR_OFF]
    ckv = ckv * lax.rsqrt(jnp.mean(ckv * ckv, axis=-1, keepdims=True) + EPS) * mkvn_ref[...]
    kvm_ref[...] = _dot(ckv.astype(_bf16), wukv_ref[...]).astype(_bf16)
    kr = proj[:, KR_OFF:UC_OFF]
    if not latent:
        nckv_ref[...] = ckv
        nkr_ref[...] = kr[:, 0:M_ROPE]
    else:
        kr = _rot(kr, cosm_ref[...], sinm_ref[...], M_ROPE // 4)
    kr_ref[...] = kr.astype(_bf16)

    uc = proj[:, UC_OFF:VC_OFF]
    vc = proj[:, VC_OFF:IN_WP]
    bd_c = bd_ref[0:CW, 0:CW]
    mu = _segsum(vc, bd_c) * (1.0 / C_DIM)
    dv = vc - mu
    var = _segsum(dv * dv, bd_c) * (1.0 / C_DIM)
    vg = (dv * lax.rsqrt(var + EPS)).astype(_bf16)
    lane_c = lax.broadcasted_iota(jnp.int32, (C_CHUNK, CW), 1)
    for r in range(x.shape[0] // C_CHUNK):
        rows = slice(r * C_CHUNK, (r + 1) * C_CHUNK)
        vgc = vg[rows]
        mixed = bsf_ref[...]
        for g in range(C_GROUPS):
            mg = _dot(ws_ref[g], vgc)
            mixed = mixed + jnp.where(lane_c // C_DIM == g, mg, 0.0)
        oc_ref[rows, :] = (uc[rows] * mixed).astype(_bf16)


def _pre_call(x2, mod, seq, wts, rope, latent):
    n, d = x2.shape
    tb = PRE_TB
    nblk_seq = seq // tb
    row = lambda i: (i, 0)
    const = lambda i: (0, 0)
    in_specs = [pl.BlockSpec((tb, d), row),
                pl.BlockSpec((1, 1, mod.shape[-1]), lambda i: (i // nblk_seq, 0, 0)),
                pl.BlockSpec(wts['win'].shape, const),
                pl.BlockSpec((1, QA_W), const), pl.BlockSpec((1, KA_W), const),
                pl.BlockSpec((1, M_Q_RANK), const), pl.BlockSpec((1, M_KV_RANK), const),
                pl.BlockSpec(wts['wuq'].shape, const), pl.BlockSpec(wts['wukv'].shape, const),
                pl.BlockSpec(wts['ws'].shape, lambda i: (0, 0, 0)),
                pl.BlockSpec((C_CHUNK, CW), const),
                pl.BlockSpec((QA_W, QA_W), const)]
    args = [x2, mod, wts['win'], wts['aqn'], wts['akn'], wts['mqn'], wts['mkvn'], wts['wuq'], wts['wukv'],
            wts['ws'], wts['bsf'], wts['bd']]
    if latent:
        pos = lambda i: (i % nblk_seq, 0)
        in_specs += [pl.BlockSpec((tb, LANES), pos)] * 4
        args += list(rope)
    widths = [QA_W, KA_W, KA_W, M_HEADS * (M_NOPE + M_ROPE), M_HEADS * (M_NOPE + M_V), M_HEADS * M_ROPE, CW]
    out_specs = [pl.BlockSpec((tb, w), row) for w in widths]
    out_shape = [jax.ShapeDtypeStruct((n, w), _bf16) for w in widths]
    if not latent:
        for w in (KA_W, KA_W, M_KV_RANK, M_ROPE):
            out_specs.append(pl.BlockSpec((tb, w), row))
            out_shape.append(jax.ShapeDtypeStruct((n, w), _f32))
    return pl.pallas_call(
        functools.partial(_pre_kernel, latent=latent, d_model=d),
        grid=(n // tb,),
        in_specs=in_specs, out_specs=out_specs, out_shape=out_shape,
        compiler_params=pltpu.CompilerParams(dimension_semantics=("parallel",), vmem_limit_bytes=VMEM_LIMIT),
        name="pre_lat" if latent else "pre_ctx",
    )(*args)


def _softmax_pv(q, keys, vals):
    s = [_dot_nt(q, k) for k in keys]
    m = s[0].max(axis=-1, keepdims=True)
    for si in s[1:]:
        m = jnp.maximum(m, si.max(axis=-1, keepdims=True))
    den = None
    o = None
    for si, v in zip(s, vals):
        e = jnp.exp(si - m)
        d = e.sum(axis=-1, keepdims=True)
        den = d if den is None else den + d
        pv = _dot(e.astype(_bf16), v)
        o = pv if o is None else o + pv
    return o * (1.0 / den)


def _attn_kernel(*refs, latent):
    if latent:
        (qa_ref, qm_ref, ka_ref, va_ref, kvm_ref, kr_ref, ck_ref, cv_ref, cckv_ref, ckr_ref, wukv_ref,
         oa_ref, om_ref) = refs
    else:
        qa_ref, qm_ref, ka_ref, va_ref, kvm_ref, kr_ref, oa_ref, om_ref = refs
    qb = qa_ref.shape[0]
    nope_w = M_HEADS * M_NOPE

    keys_a = [ka_ref[...]]
    vals_a = [va_ref[...]]
    if latent:
        keys_a = [ck_ref[0].astype(_bf16)] + keys_a
        vals_a = [cv_ref[0].astype(_bf16)] + vals_a
    lane = lax.broadcasted_iota(jnp.int32, (qb, LANES), 1)
    low = lane < HEAD_DIM
    for j in range(QA_W // LANES):
        slab = qa_ref[:, j * LANES:(j + 1) * LANES]
        q2 = jnp.concatenate([jnp.where(low, slab, jnp.zeros_like(slab)),
                              jnp.where(low, jnp.zeros_like(slab), slab)], axis=0)
        o2 = _softmax_pv(q2, keys_a, vals_a)
        oa_ref[:, j * LANES:(j + 1) * LANES] = jnp.where(low, o2[0:qb], o2[qb:]).astype(_bf16)

    kn = [kvm_ref[:, 0:nope_w]]
    vm = [kvm_ref[:, nope_w:]]
    krs = [kr_ref[...]]
    if latent:
        kvc = _dot(cckv_ref[0].astype(_bf16), wukv_ref[...]).astype(_bf16)
        kn = [kvc[:, 0:nope_w]] + kn
        vm = [kvc[:, nope_w:]] + vm
        krs = [ckr_ref[0].astype(_bf16)] + krs
    keys_m = [jnp.concatenate([a, b], axis=1) for a, b in zip(kn, krs)]
    qm = qm_ref[...]
    lane_q = lax.broadcasted_iota(jnp.int32, qm.shape, 1)
    head_q = jnp.where(lane_q < nope_w, lane_q // M_NOPE, (lane_q - nope_w) // M_ROPE)
    lane_o = lax.broadcasted_iota(jnp.int32, (qb, nope_w), 1)
    om = jnp.zeros((qb, nope_w), _f32)
    for p in range(M_HEADS // 2):
        q2 = jnp.concatenate([jnp.where(head_q == 2 * p + t, qm, jnp.zeros_like(qm)) for t in range(2)], axis=0)
        o2 = _softmax_pv(q2, keys_m, vm)
        for t in range(2):
            om = jnp.where(lane_o // M_V == 2 * p + t, o2[t * qb:(t + 1) * qb], om)
    om_ref[...] = om.astype(_bf16)


def _attn_call(pre, seq, cache, wukv, latent):
    qa, ka, va, qm, kvm, kr = pre
    n = qa.shape[0]
    nb = n // seq
    qb = min(ATT_QB, seq)
    nq = seq // qb
    qrow = lambda b, i: (b * nq + i, 0)
    krow = lambda b, i: (b, 0)
    in_specs = [pl.BlockSpec((qb, qa.shape[1]), qrow), pl.BlockSpec((qb, qm.shape[1]), qrow),
                pl.BlockSpec((seq, ka.shape[1]), krow), pl.BlockSpec((seq, va.shape[1]), krow),
                pl.BlockSpec((seq, kvm.shape[1]), krow), pl.BlockSpec((seq, kr.shape[1]), krow)]
    args = [qa, qm, ka, va, kvm, kr]
    if latent:
        for c in cache:
            in_specs.append(pl.BlockSpec((1,) + c.shape[1:], lambda b, i: (b, 0, 0)))
            args.append(c)
        in_specs.append(pl.BlockSpec(wukv.shape, lambda b, i: (0, 0)))
        args.append(wukv)
    return pl.pallas_call(
        functools.partial(_attn_kernel, latent=latent),
        grid=(nb, nq),
        in_specs=in_specs,
        out_specs=[pl.BlockSpec((qb, QA_W), qrow), pl.BlockSpec((qb, M_HEADS * M_V), qrow)],
        out_shape=[jax.ShapeDtypeStruct((n, QA_W), _bf16), jax.ShapeDtypeStruct((n, M_HEADS * M_V), _bf16)],
        compiler_params=pltpu.CompilerParams(dimension_semantics=("parallel", "parallel"),
                                             vmem_limit_bytes=VMEM_LIMIT),
        name="attn_lat" if latent else "attn_ctx",
    )(*args)


def _post_kernel(oa_ref, om_ref, oc_ref, x_ref, mod_ref, wo_ref, g_ref, b_ref, x1_ref, h2_ref, *, alpha, d_model):
    D = d_model
    a_w = oa_ref.shape[1]
    m_w = om_ref.shape[1]
    mix = (_dot(oa_ref[...], wo_ref[0:a_w, :]) + _dot(om_ref[...], wo_ref[a_w:a_w + m_w, :])
           + _dot(oc_ref[...], wo_ref[a_w + m_w:, :]))
    mod = mod_ref[0]
    g1 = mod[:, 2 * D:3 * D]
    sh2 = mod[:, 3 * D:4 * D]
    sc2 = mod[:, 4 * D:5 * D]
    x1 = _ln_rows(alpha * x_ref[...] + g1 * mix) * g_ref[...] + b_ref[...]
    x1_ref[...] = x1
    h2_ref[...] = (_ln_rows(x1) * (1.0 + sc2) + sh2).astype(_bf16)


def _post_call(oa, om, oc, x2, mod, seq, wo, g, b, alpha):
    n, d = x2.shape
    tb = PRE_TB
    nblk_seq = seq // tb
    row = lambda i: (i, 0)
    const = lambda i: (0, 0)
    return pl.pallas_call(
        functools.partial(_post_kernel, alpha=alpha, d_model=d),
        grid=(n // tb,),
        in_specs=[pl.BlockSpec((tb, oa.shape[1]), row), pl.BlockSpec((tb, om.shape[1]), row),
                  pl.BlockSpec((tb, oc.shape[1]), row), pl.BlockSpec((tb, d), row),
                  pl.BlockSpec((1, 1, mod.shape[-1]), lambda i: (i // nblk_seq, 0, 0)),
                  pl.BlockSpec(wo.shape, const), pl.BlockSpec((1, d), const), pl.BlockSpec((1, d), const)],
        out_specs=[pl.BlockSpec((tb, d), row), pl.BlockSpec((tb, d), row)],
        out_shape=[jax.ShapeDtypeStruct((n, d), _f32), jax.ShapeDtypeStruct((n, d), _bf16)],
        compiler_params=pltpu.CompilerParams(dimension_semantics=("parallel",), vmem_limit_bytes=VMEM_LIMIT),
        name="post",
    )(oa, om, oc, x2, mod, wo, g, b)


def _sorting_network(n):
    def merge(lo, hi, r):
        step = r * 2
        if step < hi - lo:
            yield from merge(lo, hi, step)
            yield from merge(lo + r, hi, step)
            yield from ((i, i + r) for i in range(lo + r, hi - r, step))
        else:
            yield (lo, lo + r)

    def sort(lo, hi):
        if hi - lo >= 1:
            mid = lo + (hi - lo) // 2
            yield from sort(lo, mid)
            yield from sort(mid + 1, hi)
            yield from merge(lo, hi, 1)

    return tuple(sort(0, n - 1))


SUBLANES = 8
_SORT16 = _sorting_network(N_KEYS // SUBLANES)


def _extract_top(s, n):
    wires = [s[SUBLANES * i:SUBLANES * (i + 1)] for i in range(s.shape[0] // SUBLANES)]
    wires += [None] * (N_KEYS // SUBLANES - len(wires))
    for a, b in _SORT16:
        x, y = wires[a], wires[b]
        if y is None:
            continue
        if x is None:
            wires[a], wires[b] = y, None
        else:
            wires[a], wires[b] = jnp.maximum(x, y), jnp.minimum(x, y)
    cols = [w for w in wires if w is not None]
    tops = []
    for j in range(n):
        m = jnp.max(cols[0], axis=0, keepdims=True)
        tops.append(m)
        if j + 1 < n:
            top = cols[0] == m
            keep = min(len(cols), n - 1 - j)
            cols = [jnp.where(top, cols[r + 1] if r + 1 < len(cols) else NEG_BIG, cols[r]) for r in range(keep)]
    return tops


def _stack_rows(rows):
    n = len(rows)
    rid = lax.broadcasted_iota(jnp.int32, (n, LANES), 0)
    out = jnp.broadcast_to(rows[0], (n, LANES))
    for j in range(1, n):
        out = jnp.where(rid == j, rows[j], out)
    return out


def _peer_select(s1, s2):
    k = PEER_TOPK
    v1 = _extract_top(s1, k + 1)
    v2 = _extract_top(s2, k + 1)
    rank2 = jnp.zeros_like(s2)
    for j in range(k):
        rank2 = rank2 + jnp.where(s2 < v2[j], 1.0, 0.0)
    col1 = _stack_rows(v1[0:k])
    col2 = _stack_rows(v2[0:k])
    pieces = [v1[0] + col2]
    for a in range(1, 8):
        pieces.append(v1[a] + col2[0:8])
    pieces.append(col1[8:16] + v2[0])
    tail = jnp.where(lax.broadcasted_iota(jnp.int32, (8, LANES), 0) == 0, v1[0] + v2[k],
                     jnp.where(lax.broadcasted_iota(jnp.int32, (8, LANES), 0) == 1, v1[k] + v2[0], NEG_BIG))
    pieces.append(tail)
    cand = jnp.concatenate(pieces, axis=0)
    tops = _extract_top(cand, k + 1)
    z = jnp.ones_like(tops[0])
    for j in range(1, k):
        z = z + jnp.exp(tops[j] - tops[0])
    tau = 0.5 * (tops[k - 1] + tops[k])
    cnt = jnp.zeros_like(s1)
    for b in range(8):
        cnt = cnt + jnp.where(s1 >= tau - v2[b], 1.0, 0.0)
    extra = jnp.zeros_like(tau)
    for b in range(8, k):
        extra = extra + jnp.where(v1[0] >= tau - v2[b], 1.0, 0.0)
    cnt = cnt + jnp.where(s1 >= v1[0], extra, 0.0)
    e1 = jnp.exp(s1 - v1[0]) * (1.0 / z)
    e2 = jnp.exp(s2 - v2[0])
    return rank2, cnt, e1, e2


def _gelu2(x):
    c = math.sqrt(2.0 / math.pi)
    return x * (1.0 + jnp.tanh(x * (c + (c * 0.044715) * (x * x))))


def _peer_kernel(h2_ref, x1_ref, mod_ref, wq_ref, k1_ref, k2_ref, u_ref, vt_ref, g_ref, b_ref, y_ref,
                 s2_ref, re_ref, cnt_ref, e1_ref, a0_ref, a1_ref, p0_ref, p1_ref, acc_ref,
                 *, alpha, d_model, n_tiles):
    D = d_model
    e = pl.program_id(1)
    t = h2_ref.shape[0]
    nchunk = t // LANES
    half = t // 2
    slabs = u_ref.shape[0] // N_KEYS
    a_refs = (a0_ref, a1_ref)
    p_refs = (p0_ref, p1_ref)

    def gate_chunk(off, a_ref, p_ref):
        cols = pl.ds(off, LANES)
        rk_cols = pl.ds(2 * off, LANES)
        e2_cols = pl.ds(2 * off + LANES, LANES)
        for g in range(slabs // 8):
            rows8 = pl.ds(pl.multiple_of(e * slabs + g * 8, 8), 8)
            cnt8 = [cnt_ref[h, rows8, cols] for h in range(PEER_HEADS)]
            e18 = [e1_ref[h, rows8, cols] for h in range(PEER_HEADS)]
            for k in range(8):
                w = jnp.zeros((N_KEYS, LANES), _bf16)
                for h in range(PEER_HEADS):
                    hit = re_ref[h, :, rk_cols] < cnt8[h][k:k + 1].astype(_bf16)
                    w = w + jnp.where(hit, re_ref[h, :, e2_cols], jnp.zeros((), _bf16)) * e18[h][k:k + 1].astype(_bf16)
                rows = slice((g * 8 + k) * N_KEYS, (g * 8 + k + 1) * N_KEYS)
                p_ref[rows, cols] = _gelu2(a_ref[rows, cols]).astype(_bf16) * w

    def run(parity, front, back):
        cur, prev = parity, 1 - parity
        for hf in range(2):
            off = hf * half
            if front:
                a_refs[cur][:, pl.ds(off, half)] = _dot_nt(u_ref[...], h2_ref[pl.ds(off, half), :])
                for c in range(half // LANES):
                    gate_chunk(off + c * LANES, a_refs[cur], p_refs[cur])
            if back:
                acc_ref[:, pl.ds(off, half)] += _dot(vt_ref[...], p_refs[prev][:, pl.ds(off, half)])

    @pl.when(e == 0)
    def _select():
        q = _dot(h2_ref[...], wq_ref[...]).astype(_bf16)
        for h in range(PEER_HEADS):
            base = h * 2 * PEER_HALF
            rows = slice(h * N_KEYS, (h + 1) * N_KEYS)
            a1_ref[rows, :] = _dot_nt(k1_ref[...], q[:, base:base + PEER_HALF])
            s2_ref[h] = _dot_nt(k2_ref[...], q[:, base + PEER_HALF:base + 2 * PEER_HALF])

        def body(it, carry):
            c = pl.multiple_of((it % nchunk) * LANES, LANES)
            for j in range(2):
                h = 2 * (it // nchunk) + j
                r = pl.multiple_of(h * N_KEYS, N_KEYS)
                rank2, cnt, e1, e2 = _peer_select(a1_ref[pl.ds(r, N_KEYS), pl.ds(c, LANES)],
                                                  s2_ref[h, :, pl.ds(c, LANES)])
                re_ref[h, :, pl.ds(pl.multiple_of(2 * c, LANES), LANES)] = rank2.astype(_bf16)
                cnt_ref[h, :, pl.ds(c, LANES)] = cnt
                e1_ref[h, :, pl.ds(c, LANES)] = 0.5 * e1
                re_ref[h, :, pl.ds(pl.multiple_of(2 * c + LANES, LANES), LANES)] = e2.astype(_bf16)
            return carry

        lax.fori_loop(0, PEER_HEADS // 2 * nchunk, body, 0)
        acc_ref[...] = jnp.zeros_like(acc_ref)
        run(0, True, False)

    for parity in range(2):
        @pl.when((e >= 1) & (e < n_tiles) & (e % 2 == parity))
        def _steady():
            run(parity, True, True)

    @pl.when(e == n_tiles)
    def _finish():
        run(n_tiles % 2, False, True)
        mod = mod_ref[0]
        g2 = mod[:, 5 * D:6 * D]
        ff = acc_ref[...].T
        y_ref[...] = _ln_rows(alpha * x1_ref[...] + g2 * ff) * g_ref[...] + b_ref[...]


def _peer_call(h2, x1, mod, seq, wq, k1, k2, u, vt, g, b, alpha):
    n, d = x1.shape
    t = PEER_T
    eb = PEER_EB
    n_exp = u.shape[0]
    nblk_seq = seq // t
    row = lambda i, e: (i, 0)
    const = lambda i, e: (0, 0)
    n_tiles = n_exp // eb
    assert eb == PEER_HEADS * N_KEYS and n_tiles >= 2
    sel32 = pltpu.VMEM((PEER_HEADS, N_KEYS, t), _f32)
    sel16 = pltpu.VMEM((PEER_HEADS, N_KEYS, 2 * t), _bf16)
    return pl.pallas_call(
        functools.partial(_peer_kernel, alpha=alpha, d_model=d, n_tiles=n_tiles),
        grid=(n // t, n_tiles + 1),
        in_specs=[pl.BlockSpec((t, d), row), pl.BlockSpec((t, d), row),
                  pl.BlockSpec((1, 1, mod.shape[-1]), lambda i, e: (i // nblk_seq, 0, 0)),
                  pl.BlockSpec(wq.shape, const), pl.BlockSpec(k1.shape, const), pl.BlockSpec(k2.shape, const),
                  pl.BlockSpec((eb, d), lambda i, e: (jnp.minimum(e, n_tiles - 1), 0)),
                  pl.BlockSpec((d, eb), lambda i, e: (0, jnp.maximum(e - 1, 0))),
                  pl.BlockSpec((1, d), const), pl.BlockSpec((1, d), const)],
        out_specs=pl.BlockSpec((t, d), row),
        out_shape=jax.ShapeDtypeStruct((n, d), _f32),
        scratch_shapes=[sel32, sel16, sel32, sel32,
                        pltpu.VMEM((eb, t), _f32), pltpu.VMEM((eb, t), _f32),
                        pltpu.VMEM((eb, t), _bf16), pltpu.VMEM((eb, t), _bf16), pltpu.VMEM((d, t), _f32)],
        compiler_params=pltpu.CompilerParams(dimension_semantics=("parallel", "arbitrary"),
                                             vmem_limit_bytes=VMEM_LIMIT),
        name="peer",
    )(h2, x1, mod, wq, k1, k2, u, vt, g, b)


def _rope_tables(seq, head_dim):
    m = head_dim // 4
    lane = np.arange(LANES)
    d = lane % head_dim
    use_col = (d // (2 * m)) == 1
    within = d % (2 * m)
    second = within // m
    freqs = ROPE_BASE ** (-np.arange(m, dtype=np.float64) / m)
    f = freqs[within % m]
    tpos = np.arange(seq)
    pos = np.where(use_col[None, :], (tpos % GRID_W)[:, None], (tpos // GRID_W)[:, None]).astype(np.float64)
    ang = pos * f[None, :]
    sign = np.where(second == 0, -1.0, 1.0)
    return jnp.asarray(np.cos(ang), _f32), jnp.asarray(np.sin(ang) * sign[None, :], _f32)


def _layer_weights(l, w_in, attn_q_norm, attn_k_norm, mla_q_norm, mla_kv_norm, w_uq, w_ukv, gmlp_ws, gmlp_b, w_o):
    offs = np.cumsum([0, QA_W, KA_W, KA_W, M_Q_RANK, M_KV_RANK, M_ROPE, CW, CW])
    qa_perm = np.concatenate([np.arange(HEAD_DIM) + HEAD_DIM * (j + 4 * half)
                              for j in range(A_HEADS // 2) for half in range(2)])
    cols = np.concatenate([qa_perm, np.arange(offs[1], offs[5]),
                           np.tile(np.arange(offs[5], offs[6]), M_HEADS), np.arange(offs[6], offs[8])])
    qd = M_NOPE + M_ROPE
    uq_cols = np.concatenate([np.arange(M_NOPE) + h * qd for h in range(M_HEADS)]
                             + [np.arange(M_ROPE) + h * qd + M_NOPE for h in range(M_HEADS)])
    kd = M_NOPE + M_V
    ukv_cols = np.concatenate([np.arange(M_NOPE) + h * kd for h in range(M_HEADS)]
                              + [np.arange(M_V) + h * kd + M_NOPE for h in range(M_HEADS)])
    wo_rows = np.concatenate([qa_perm, np.arange(QA_W, w_o.shape[1])])
    seg = np.arange(QA_W) // HEAD_DIM
    return {
        'win': w_in[l][:, cols].astype(_bf16),
        'aqn': jnp.tile(attn_q_norm[l], A_HEADS)[None, :],
        'akn': jnp.tile(attn_k_norm[l], A_KV)[None, :],
        'mqn': mla_q_norm[l][None, :],
        'mkvn': mla_kv_norm[l][None, :],
        'wuq': w_uq[l][:, uq_cols].astype(_bf16),
        'wukv': w_ukv[l][:, ukv_cols].astype(_bf16),
        'ws': gmlp_ws[l].astype(_bf16),
        'bsf': jnp.repeat(gmlp_b[l].T, C_DIM, axis=1),
        'bd': jnp.asarray(seg[:, None] == seg[None, :], _bf16),
        'wo': w_o[l][wo_rows, :].astype(_bf16),
    }


def kernel(x_prompt, x_sample, cache_attn_k, cache_attn_v, cache_mla_ckv, cache_mla_krope, c, c_ctx, w_mod, b_mod, w_in, attn_q_norm, attn_k_norm, mla_q_norm, mla_kv_norm, w_uq, w_ukv, gmlp_ws, gmlp_b, w_o, ln1_g, ln1_b, ln2_g, ln2_b, peer_wq, peer_k1, peer_k2, peer_u, peer_v):
    batch, seq, d = x_prompt.shape
    dec_batch, dec_seq, _ = x_sample.shape
    depth = w_in.shape[0]
    past = cache_attn_k.shape[2]
    alpha = (2.0 * depth) ** 0.25

    n_rows = 8 * ((1 + dec_batch + 7) // 8)
    c_all = jnp.zeros((n_rows, d), _f32).at[0].set(c_ctx).at[1:1 + dec_batch].set(c)
    mod_all = _mod_call(c_all, w_mod, b_mod)

    rope = _rope_tables(dec_seq, HEAD_DIM) + _rope_tables(dec_seq, M_ROPE)
    xp = x_prompt.reshape(batch * seq, d)
    xs = x_sample.reshape(dec_batch * dec_seq, d)
    new = []
    for l in range(depth):
        wts = _layer_weights(l, w_in, attn_q_norm, attn_k_norm, mla_q_norm, mla_kv_norm, w_uq, w_ukv,
                             gmlp_ws, gmlp_b, w_o)
        peer_w = (peer_wq[l].astype(_bf16), peer_k1[l].astype(_bf16), peer_k2[l].astype(_bf16),
                  peer_u[l].astype(_bf16), peer_v[l].T.astype(_bf16))
        ln1 = (ln1_g[l][None, :], ln1_b[l][None, :])
        ln2 = (ln2_g[l][None, :], ln2_b[l][None, :])
        mod_ctx = mod_all[l, 0:1][:, None, :]
        mod_lat = mod_all[l, 1:1 + dec_batch][:, None, :]
        cache = (cache_attn_k[:, l].reshape(dec_batch, past, KA_W),
                 cache_attn_v[:, l].reshape(dec_batch, past, KA_W),
                 cache_mla_ckv[:, l],
                 jnp.tile(cache_mla_krope[:, l], (1, 1, M_HEADS)))

        pre = _pre_call(xp, mod_ctx, batch * seq, wts, None, latent=False)
        new.append(pre[7:])
        oa, om = _attn_call(pre[0:6], seq, None, None, latent=False)
        x1, h2 = _post_call(oa, om, pre[6], xp, mod_ctx, batch * seq, wts['wo'], *ln1, alpha)
        xp = _peer_call(h2, x1, mod_ctx, batch * seq, *peer_w, *ln2, alpha)

        pre = _pre_call(xs, mod_lat, dec_seq, wts, rope, latent=True)
        oa, om = _attn_call(pre[0:6], dec_seq, cache, wts['wukv'], latent=True)
        x1, h2 = _post_call(oa, om, pre[6], xs, mod_lat, dec_seq, wts['wo'], *ln1, alpha)
        xs = _peer_call(h2, x1, mod_lat, dec_seq, *peer_w, *ln2, alpha)

    def stack(i, tail):
        return jnp.stack([new[l][i].reshape((batch, seq) + tail) for l in range(depth)], axis=1)

    return (xp.reshape(batch, seq, d), xs.reshape(dec_batch, dec_seq, d),
            stack(0, (A_KV, HEAD_DIM)), stack(1, (A_KV, HEAD_DIM)), stack(2, (M_KV_RANK,)), stack(3, (M_ROPE,)))
```

```python
import functools
import math

import numpy as np
import jax
import jax.numpy as jnp
from jax import lax
from jax.experimental import pallas as pl
from jax.experimental.pallas import tpu as pltpu

A_HEADS = 8
A_KV = 2
HEAD_DIM = 64
M_HEADS = 4
M_Q_RANK = 256
M_KV_RANK = 128
M_NOPE = 64
M_ROPE = 32
M_V = 64
C_GROUPS = 4
C_DIM = 64
C_CHUNK = 128
PEER_HEADS = 8
N_KEYS = 128
PEER_HALF = 128
PEER_TOPK = 16
GRID_W = 64
EPS = 1e-6
ROPE_BASE = 10000.0

LANES = 128
ROWS_B = 16
NEG_BIG = -3.0e38

QA_W = A_HEADS * HEAD_DIM
KA_W = A_KV * HEAD_DIM
CQ_OFF = QA_W + 2 * KA_W
CKV_OFF = CQ_OFF + M_Q_RANK
KR_OFF = CKV_OFF + M_KV_RANK
UC_OFF = KR_OFF + M_HEADS * M_ROPE
CW = C_GROUPS * C_DIM
VC_OFF = UC_OFF + CW
IN_WP = VC_OFF + CW

PRE_TB = 256
ATT_QB = 256
PEER_T = 512
PEER_EB = 1024
VMEM_LIMIT = 56 * 1024 * 1024

_bf16 = jnp.bfloat16
_f32 = jnp.float32


def _dot(a, b):
    return jnp.dot(a, b, preferred_element_type=_f32)


def _dot_nt(a, b):
    return lax.dot_general(a, b, (((1,), (1,)), ((), ())), preferred_element_type=_f32)


def _split(x):
    hi = x.astype(_bf16)
    lo = (x - hi.astype(_f32)).astype(_bf16)
    return hi, lo


def _segsum(x, ones_bd):
    hi, lo = _split(x)
    return _dot(hi, ones_bd) + _dot(lo, ones_bd)


def _ln_rows(x):
    mu = jnp.mean(x, axis=-1, keepdims=True)
    d = x - mu
    var = jnp.mean(d * d, axis=-1, keepdims=True)
    return d * lax.rsqrt(var + EPS)


def _rot(x, cos, sin, half):
    lane = lax.broadcasted_iota(jnp.int32, x.shape, 1)
    up = pltpu.roll(x, LANES - half, axis=1)
    dn = pltpu.roll(x, half, axis=1)
    partner = jnp.where((lane & half) == 0, up, dn)
    return x * cos + partner * sin


def _mod_kernel(c_ref, w_ref, b_ref, o_ref):
    c = c_ref[...]
    a = c * (1.0 / (1.0 + jnp.exp(-c)))
    a_hi, a_lo = _split(a)
    w_hi, w_lo = _split(w_ref[0])
    o_ref[0] = _dot(a_hi, w_hi) + _dot(a_lo, w_hi) + _dot(a_hi, w_lo) + b_ref[0]


def _mod_call(c_all, w_mod, b_mod):
    depth, d, d6 = w_mod.shape
    rows = c_all.shape[0]
    tn = 1024
    return pl.pallas_call(
        _mod_kernel,
        grid=(depth, d6 // tn),
        in_specs=[pl.BlockSpec((rows, d), lambda l, j: (0, 0)),
                  pl.BlockSpec((1, d, tn), lambda l, j: (l, 0, j)),
                  pl.BlockSpec((1, 1, tn), lambda l, j: (l, 0, j))],
        out_specs=pl.BlockSpec((1, rows, tn), lambda l, j: (l, 0, j)),
        out_shape=jax.ShapeDtypeStruct((depth, rows, d6), _f32),
        compiler_params=pltpu.CompilerParams(dimension_semantics=("parallel", "parallel"),
                                             vmem_limit_bytes=VMEM_LIMIT),
        name="mod",
    )(c_all, w_mod, b_mod.reshape(depth, 1, d6))


def _pre_kernel(*refs, latent, d_model):
    if latent:
        (x_ref, mod_ref, win_ref, aqn_ref, akn_ref, mqn_ref, mkvn_ref, wuq_ref, wukv_ref, ws_ref, bsf_ref,
         bd_ref, cosa_ref, sina_ref, cosm_ref, sinm_ref,
         qa_ref, ka_ref, va_ref, qm_ref, kvm_ref, kr_ref, oc_ref) = refs
    else:
        (x_ref, mod_ref, win_ref, aqn_ref, akn_ref, mqn_ref, mkvn_ref, wuq_ref, wukv_ref, ws_ref, bsf_ref,
         bd_ref,
         qa_ref, ka_ref, va_ref, qm_ref, kvm_ref, kr_ref, oc_ref,
         nk_ref, nv_ref, nckv_ref, nkr_ref) = refs
    D = d_model
    x = x_ref[...]
    mod = mod_ref[0]
    sh1 = mod[:, 0:D]
    sc1 = mod[:, D:2 * D]
    h = _ln_rows(x) * (1.0 + sc1) + sh1
    proj = _dot(h.astype(_bf16), win_ref[...])

    qa = proj[:, 0:QA_W]
    ms = _segsum(qa * qa, bd_ref[...]) * (1.0 / HEAD_DIM)
    qa = qa * lax.rsqrt(ms + EPS) * aqn_ref[...]
    ka = proj[:, QA_W:QA_W + KA_W]
    ms = _segsum(ka * ka, bd_ref[0:KA_W, 0:KA_W]) * (1.0 / HEAD_DIM)
    ka = ka * lax.rsqrt(ms + EPS) * akn_ref[...]
    va = proj[:, QA_W + KA_W:CQ_OFF]
    if not latent:
        nk_ref[...] = ka
        nv_ref[...] = va
    for j in range(QA_W // LANES):
        slab = qa[:, j * LANES:(j + 1) * LANES]
        if latent:
            slab = _rot(slab, cosa_ref[...], sina_ref[...], HEAD_DIM // 4)
        qa_ref[:, j * LANES:(j + 1) * LANES] = (slab * (1.0 / math.sqrt(HEAD_DIM))).astype(_bf16)
    if latent:
        ka = _rot(ka, cosa_ref[...], sina_ref[...], HEAD_DIM // 4)
    ka_ref[...] = ka.astype(_bf16)
    va_ref[...] = va.astype(_bf16)

    cq = proj[:, CQ_OFF:CKV_OFF]
    cq = cq * lax.rsqrt(jnp.mean(cq * cq, axis=-1, keepdims=True) + EPS) * mqn_ref[...]
    qm = _dot(cq.astype(_bf16), wuq_ref[...])
    nope_w = M_HEADS * M_NOPE
    qscale = 1.0 / math.sqrt(M_NOPE + M_ROPE)
    qm_ref[:, 0:nope_w] = (qm[:, 0:nope_w] * qscale).astype(_bf16)
    qr = qm[:, nope_w:]
    if latent:
        qr = _rot(qr, cosm_ref[...], sinm_ref[...], M_ROPE // 4)
    qm_ref[:, nope_w:] = (qr * qscale).astype(_bf16)
    ckv = proj[:, CKV_OFF:KR_OFF]
    ckv = ckv * lax.rsqrt(jnp.mean(ckv * ckv, axis=-1, keepdims=True) + EPS) * mkvn_ref[...]
    kvm_ref[...] = _dot(ckv.astype(_bf16), wukv_ref[...]).astype(_bf16)
    kr = proj[:, KR_OFF:UC_OFF]
    if not latent:
        nckv_ref[...] = ckv
        nkr_ref[...] = kr[:, 0:M_ROPE]
    else:
        kr = _rot(kr, cosm_ref[...], sinm_ref[...], M_ROPE // 4)
    kr_ref[...] = kr.astype(_bf16)

    uc = proj[:, UC_OFF:VC_OFF]
    vc = proj[:, VC_OFF:IN_WP]
    bd_c = bd_ref[0:CW, 0:CW]
    mu = _segsum(vc, bd_c) * (1.0 / C_DIM)
    dv = vc - mu
    var = _segsum(dv * dv, bd_c) * (1.0 / C_DIM)
    vg = (dv * lax.rsqrt(var + EPS)).astype(_bf16)
    lane_c = lax.broadcasted_iota(jnp.int32, (C_CHUNK, CW), 1)
    for r in range(x.shape[0] // C_CHUNK):
        rows = slice(r * C_CHUNK, (r + 1) * C_CHUNK)
        vgc = vg[rows]
        mixed = bsf_ref[...]
        for g in range(C_GROUPS):
            mg = _dot(ws_ref[g], vgc)
            mixed = mixed + jnp.where(lane_c // C_DIM == g, mg, 0.0)
        oc_ref[rows, :] = (uc[rows] * mixed).astype(_bf16)


def _pre_call(x2, mod, seq, wts, rope, latent):
    n, d = x2.shape
    tb = PRE_TB
    nblk_seq = seq // tb
    row = lambda i: (i, 0)
    const = lambda i: (0, 0)
    in_specs = [pl.BlockSpec((tb, d), row),
                pl.BlockSpec((1, 1, mod.shape[-1]), lambda i: (i // nblk_seq, 0, 0)),
                pl.BlockSpec(wts['win'].shape, const),
                pl.BlockSpec((1, QA_W), const), pl.BlockSpec((1, KA_W), const),
                pl.BlockSpec((1, M_Q_RANK), const), pl.BlockSpec((1, M_KV_RANK), const),
                pl.BlockSpec(wts['wuq'].shape, const), pl.BlockSpec(wts['wukv'].shape, const),
                pl.BlockSpec(wts['ws'].shape, lambda i: (0, 0, 0)),
                pl.BlockSpec((C_CHUNK, CW), const),
                pl.BlockSpec((QA_W, QA_W), const)]
    args = [x2, mod, wts['win'], wts['aqn'], wts['akn'], wts['mqn'], wts['mkvn'], wts['wuq'], wts['wukv'],
            wts['ws'], wts['bsf'], wts['bd']]
    if latent:
        pos = lambda i: (i % nblk_seq, 0)
        in_specs += [pl.BlockSpec((tb, LANES), pos)] * 4
        args += list(rope)
    widths = [QA_W, KA_W, KA_W, M_HEADS * (M_NOPE + M_ROPE), M_HEADS * (M_NOPE + M_V), M_HEADS * M_ROPE, CW]
    out_specs = [pl.BlockSpec((tb, w), row) for w in widths]
    out_shape = [jax.ShapeDtypeStruct((n, w), _bf16) for w in widths]
    if not latent:
        for w in (KA_W, KA_W, M_KV_RANK, M_ROPE):
            out_specs.append(pl.BlockSpec((tb, w), row))
            out_shape.append(jax.ShapeDtypeStruct((n, w), _f32))
    return pl.pallas_call(
        functools.partial(_pre_kernel, latent=latent, d_model=d),
        grid=(n // tb,),
        in_specs=in_specs, out_specs=out_specs, out_shape=out_shape,
        compiler_params=pltpu.CompilerParams(dimension_semantics=("parallel",), vmem_limit_bytes=VMEM_LIMIT),
        name="pre_lat" if latent else "pre_ctx",
    )(*args)


def _softmax_pv(q, keys, vals):
    s = [_dot_nt(q, k) for k in keys]
    m = s[0].max(axis=-1, keepdims=True)
    for si in s[1:]:
        m = jnp.maximum(m, si.max(axis=-1, keepdims=True))
    den = None
    o = None
    for si, v in zip(s, vals):
        e = jnp.exp(si - m)
        d = e.sum(axis=-1, keepdims=True)
        den = d if den is None else den + d
        pv = _dot(e.astype(_bf16), v)
        o = pv if o is None else o + pv
    return o * (1.0 / den)


def _attn_kernel(*refs, latent):
    if latent:
        (qa_ref, qm_ref, ka_ref, va_ref, kvm_ref, kr_ref, ck_ref, cv_ref, cckv_ref, ckr_ref, wukv_ref,
         oa_ref, om_ref) = refs
    else:
        qa_ref, qm_ref, ka_ref, va_ref, kvm_ref, kr_ref, oa_ref, om_ref = refs
    qb = qa_ref.shape[0]
    nope_w = M_HEADS * M_NOPE

    keys_a = [ka_ref[...]]
    vals_a = [va_ref[...]]
    if latent:
        keys_a = [ck_ref[0].astype(_bf16)] + keys_a
        vals_a = [cv_ref[0].astype(_bf16)] + vals_a
    lane = lax.broadcasted_iota(jnp.int32, (qb, LANES), 1)
    low = lane < HEAD_DIM
    for j in range(QA_W // LANES):
        slab = qa_ref[:, j * LANES:(j + 1) * LANES]
        q2 = jnp.concatenate([jnp.where(low, slab, jnp.zeros_like(slab)),
                              jnp.where(low, jnp.zeros_like(slab), slab)], axis=0)
        o2 = _softmax_pv(q2, keys_a, vals_a)
        oa_ref[:, j * LANES:(j + 1) * LANES] = jnp.where(low, o2[0:qb], o2[qb:]).astype(_bf16)

    kn = [kvm_ref[:, 0:nope_w]]
    vm = [kvm_ref[:, nope_w:]]
    krs = [kr_ref[...]]
    if latent:
        kvc = _dot(cckv_ref[0].astype(_bf16), wukv_ref[...]).astype(_bf16)
        kn = [kvc[:, 0:nope_w]] + kn
        vm = [kvc[:, nope_w:]] + vm
        krs = [ckr_ref[0].astype(_bf16)] + krs
    keys_m = [jnp.concatenate([a, b], axis=1) for a, b in zip(kn, krs)]
    qm = qm_ref[...]
    lane_q = lax.broadcasted_iota(jnp.int32, qm.shape, 1)
    head_q = jnp.where(lane_q < nope_w, lane_q // M_NOPE, (lane_q - nope_w) // M_ROPE)
    lane_o = lax.broadcasted_iota(jnp.int32, (qb, nope_w), 1)
    om = jnp.zeros((qb, nope_w), _f32)
    for p in range(M_HEADS // 2):
        q2 = jnp.concatenate([jnp.where(head_q == 2 * p + t, qm, jnp.zeros_like(qm)) for t in range(2)], axis=0)
        o2 = _softmax_pv(q2, keys_m, vm)
        for t in range(2):
            om = jnp.where(lane_o // M_V == 2 * p + t, o2[t * qb:(t + 1) * qb], om)
    om_ref[...] = om.astype(_bf16)


def _attn_call(pre, seq, cache, wukv, latent):
    qa, ka, va, qm, kvm, kr = pre
    n = qa.shape[0]
    nb = n // seq
    qb = min(ATT_QB, seq)
    nq = seq // qb
    qrow = lambda b, i: (b * nq + i, 0)
    krow = lambda b, i: (b, 0)
    in_specs = [pl.BlockSpec((qb, qa.shape[1]), qrow), pl.BlockSpec((qb, qm.shape[1]), qrow),
                pl.BlockSpec((seq, ka.shape[1]), krow), pl.BlockSpec((seq, va.shape[1]), krow),
                pl.BlockSpec((seq, kvm.shape[1]), krow), pl.BlockSpec((seq, kr.shape[1]), krow)]
    args = [qa, qm, ka, va, kvm, kr]
    if latent:
        for c in cache:
            in_specs.append(pl.BlockSpec((1,) + c.shape[1:], lambda b, i: (b, 0, 0)))
            args.append(c)
        in_specs.append(pl.BlockSpec(wukv.shape, lambda b, i: (0, 0)))
        args.append(wukv)
    return pl.pallas_call(
        functools.partial(_attn_kernel, latent=latent),
        grid=(nb, nq),
        in_specs=in_specs,
        out_specs=[pl.BlockSpec((qb, QA_W), qrow), pl.BlockSpec((qb, M_HEADS * M_V), qrow)],
        out_shape=[jax.ShapeDtypeStruct((n, QA_W), _bf16), jax.ShapeDtypeStruct((n, M_HEADS * M_V), _bf16)],
        compiler_params=pltpu.CompilerParams(dimension_semantics=("parallel", "parallel"),
                                             vmem_limit_bytes=VMEM_LIMIT),
        name="attn_lat" if latent else "attn_ctx",
    )(*args)


def _post_kernel(oa_ref, om_ref, oc_ref, x_ref, mod_ref, wo_ref, g_ref, b_ref, x1_ref, h2_ref, *, alpha, d_model):
    D = d_model
    a_w = oa_ref.shape[1]
    m_w = om_ref.shape[1]
    mix = (_dot(oa_ref[...], wo_ref[0:a_w, :]) + _dot(om_ref[...], wo_ref[a_w:a_w + m_w, :])
           + _dot(oc_ref[...], wo_ref[a_w + m_w:, :]))
    mod = mod_ref[0]
    g1 = mod[:, 2 * D:3 * D]
    sh2 = mod[:, 3 * D:4 * D]
    sc2 = mod[:, 4 * D:5 * D]
    x1 = _ln_rows(alpha * x_ref[...] + g1 * mix) * g_ref[...] + b_ref[...]
    x1_ref[...] = x1
    h2_ref[...] = (_ln_rows(x1) * (1.0 + sc2) + sh2).astype(_bf16)


def _post_call(oa, om, oc, x2, mod, seq, wo, g, b, alpha):
    n, d = x2.shape
    tb = PRE_TB
    nblk_seq = seq // tb
    row = lambda i: (i, 0)
    const = lambda i: (0, 0)
    return pl.pallas_call(
        functools.partial(_post_kernel, alpha=alpha, d_model=d),
        grid=(n // tb,),
        in_specs=[pl.BlockSpec((tb, oa.shape[1]), row), pl.BlockSpec((tb, om.shape[1]), row),
                  pl.BlockSpec((tb, oc.shape[1]), row), pl.BlockSpec((tb, d), row),
                  pl.BlockSpec((1, 1, mod.shape[-1]), lambda i: (i // nblk_seq, 0, 0)),
                  pl.BlockSpec(wo.shape, const), pl.BlockSpec((1, d), const), pl.BlockSpec((1, d), const)],
        out_specs=[pl.BlockSpec((tb, d), row), pl.BlockSpec((tb, d), row)],
        out_shape=[jax.ShapeDtypeStruct((n, d), _f32), jax.ShapeDtypeStruct((n, d), _bf16)],
        compiler_params=pltpu.CompilerParams(dimension_semantics=("parallel",), vmem_limit_bytes=VMEM_LIMIT),
        name="post",
    )(oa, om, oc, x2, mod, wo, g, b)


def _sorting_network(n):
    def merge(lo, hi, r):
        step = r * 2
        if step < hi - lo:
            yield from merge(lo, hi, step)
            yield from merge(lo + r, hi, step)
            yield from ((i, i + r) for i in range(lo + r, hi - r, step))
        else:
            yield (lo, lo + r)

    def sort(lo, hi):
        if hi - lo >= 1:
            mid = lo + (hi - lo) // 2
            yield from sort(lo, mid)
            yield from sort(mid + 1, hi)
            yield from merge(lo, hi, 1)

    return tuple(sort(0, n - 1))


SUBLANES = 8
_SORT16 = _sorting_network(N_KEYS // SUBLANES)


def _extract_top(s, n):
    wires = [s[SUBLANES * i:SUBLANES * (i + 1)] for i in range(s.shape[0] // SUBLANES)]
    wires += [None] * (N_KEYS // SUBLANES - len(wires))
    for a, b in _SORT16:
        x, y = wires[a], wires[b]
        if y is None:
            continue
        if x is None:
            wires[a], wires[b] = y, None
        else:
            wires[a], wires[b] = jnp.maximum(x, y), jnp.minimum(x, y)
    cols = [w for w in wires if w is not None]
    tops = []
    for j in range(n):
        m = jnp.max(cols[0], axis=0, keepdims=True)
        tops.append(m)
        if j + 1 < n:
            top = cols[0] == m
            keep = min(len(cols), n - 1 - j)
            cols = [jnp.where(top, cols[r + 1] if r + 1 < len(cols) else NEG_BIG, cols[r]) for r in range(keep)]
    return tops


def _stack_rows(rows):
    n = len(rows)
    rid = lax.broadcasted_iota(jnp.int32, (n, LANES), 0)
    out = jnp.broadcast_to(rows[0], (n, LANES))
    for j in range(1, n):
        out = jnp.where(rid == j, rows[j], out)
    return out


def _peer_select(s1, s2):
    k = PEER_TOPK
    v1 = _extract_top(s1, k + 1)
    v2 = _extract_top(s2, k + 1)
    rank2 = jnp.full(s2.shape, float(k), _f32)
    for j in reversed(range(k)):
        rank2 = jnp.where(s2 >= v2[j], float(j), rank2)
    col1 = _stack_rows(v1[0:k])
    col2 = _stack_rows(v2[0:k])
    pieces = [v1[0] + col2]
    for a in range(1, 8):
        pieces.append(v1[a] + col2[0:8])
    pieces.append(col1[8:16] + v2[0])
    tail = jnp.where(lax.broadcasted_iota(jnp.int32, (8, LANES), 0) == 0, v1[0] + v2[k],
                     jnp.where(lax.broadcasted_iota(jnp.int32, (8, LANES), 0) == 1, v1[k] + v2[0], NEG_BIG))
    pieces.append(tail)
    cand = jnp.concatenate(pieces, axis=0)
    tops = _extract_top(cand, k + 1)
    z = jnp.ones_like(tops[0])
    for j in range(1, k):
        z = z + jnp.exp(tops[j] - tops[0])
    tau = 0.5 * (tops[k - 1] + tops[k])
    cnt = jnp.zeros_like(s1)
    for b in range(8):
        cnt = jnp.where(s1 >= tau - v2[b], float(b + 1), cnt)
    extra = jnp.zeros_like(tau)
    for b in range(8, k):
        extra = extra + jnp.where(v1[0] >= tau - v2[b], 1.0, 0.0)
    cnt = cnt + jnp.where(s1 >= v1[0], extra, 0.0)
    e1 = jnp.exp(s1 - v1[0]) * (1.0 / z)
    e2 = jnp.exp(s2 - v2[0])
    return rank2, cnt, e1, e2


def _gelu2(x):
    c = math.sqrt(2.0 / math.pi)
    return x * (1.0 + jnp.tanh(x * (c + (c * 0.044715) * (x * x))))


def _peer_kernel(h2_ref, x1_ref, mod_ref, wq_ref, k1_ref, k2_ref, u_ref, vt_ref, g_ref, b_ref, y_ref,
                 s2_ref, re_ref, cnt_ref, e1_ref, a0_ref, a1_ref, p0_ref, p1_ref, acc_ref,
                 *, alpha, d_model, n_tiles):
    D = d_model
    e = pl.program_id(1)
    t = h2_ref.shape[0]
    nchunk = t // LANES
    half = t // 2
    slabs = u_ref.shape[0] // N_KEYS
    a_refs = (a0_ref, a1_ref)
    p_refs = (p0_ref, p1_ref)

    def gate_chunk(off, a_ref, p_ref):
        cols = pl.ds(off, LANES)
        rk_cols = pl.ds(2 * off, LANES)
        e2_cols = pl.ds(2 * off + LANES, LANES)
        for g in range(slabs // 8):
            rows8 = pl.ds(pl.multiple_of(e * slabs + g * 8, 8), 8)
            cnt8 = [cnt_ref[h, rows8, cols] for h in range(PEER_HEADS)]
            e18 = [e1_ref[h, rows8, cols] for h in range(PEER_HEADS)]
            for k in range(8):
                w = jnp.zeros((N_KEYS, LANES), _bf16)
                for h in range(PEER_HEADS):
                    hit = re_ref[h, :, rk_cols] < cnt8[h][k:k + 1].astype(_bf16)
                    w = w + jnp.where(hit, re_ref[h, :, e2_cols], jnp.zeros((), _bf16)) * e18[h][k:k + 1].astype(_bf16)
                rows = slice((g * 8 + k) * N_KEYS, (g * 8 + k + 1) * N_KEYS)
                p_ref[rows, cols] = _gelu2(a_ref[rows, cols]).astype(_bf16) * w

    def run(parity, front, back):
        cur, prev = parity, 1 - parity
        for hf in range(2):
            off = hf * half
            if front:
                a_refs[cur][:, pl.ds(off, half)] = _dot_nt(u_ref[...], h2_ref[pl.ds(off, half), :])
                for c in range(half // LANES):
                    gate_chunk(off + c * LANES, a_refs[cur], p_refs[cur])
            if back:
                acc_ref[:, pl.ds(off, half)] += _dot(vt_ref[...], p_refs[prev][:, pl.ds(off, half)])

    @pl.when(e == 0)
    def _select():
        q = _dot(h2_ref[...], wq_ref[...]).astype(_bf16)
        for h in range(PEER_HEADS):
            base = h * 2 * PEER_HALF
            rows = slice(h * N_KEYS, (h + 1) * N_KEYS)
            a1_ref[rows, :] = _dot_nt(k1_ref[...], q[:, base:base + PEER_HALF])
            s2_ref[h] = _dot_nt(k2_ref[...], q[:, base + PEER_HALF:base + 2 * PEER_HALF])

        def body(it, carry):
            c = pl.multiple_of((it % nchunk) * LANES, LANES)
            for j in range(2):
                h = 2 * (it // nchunk) + j
                r = pl.multiple_of(h * N_KEYS, N_KEYS)
                rank2, cnt, e1, e2 = _peer_select(a1_ref[pl.ds(r, N_KEYS), pl.ds(c, LANES)],
                                                  s2_ref[h, :, pl.ds(c, LANES)])
                re_ref[h, :, pl.ds(pl.multiple_of(2 * c, LANES), LANES)] = rank2.astype(_bf16)
                cnt_ref[h, :, pl.ds(c, LANES)] = cnt
                e1_ref[h, :, pl.ds(c, LANES)] = 0.5 * e1
                re_ref[h, :, pl.ds(pl.multiple_of(2 * c + LANES, LANES), LANES)] = e2.astype(_bf16)
            return carry

        lax.fori_loop(0, PEER_HEADS // 2 * nchunk, body, 0)
        acc_ref[...] = jnp.zeros_like(acc_ref)
        run(0, True, False)

    for parity in range(2):
        @pl.when((e >= 1) & (e < n_tiles) & (e % 2 == parity))
        def _steady():
            run(parity, True, True)

    @pl.when(e == n_tiles)
    def _finish():
        run(n_tiles % 2, False, True)
        mod = mod_ref[0]
        g2 = mod[:, 5 * D:6 * D]
        ff = acc_ref[...].T
        y_ref[...] = _ln_rows(alpha * x1_ref[...] + g2 * ff) * g_ref[...] + b_ref[...]


def _peer_call(h2, x1, mod, seq, wq, k1, k2, u, vt, g, b, alpha):
    n, d = x1.shape
    t = PEER_T
    eb = PEER_EB
    n_exp = u.shape[0]
    nblk_seq = seq // t
    row = lambda i, e: (i, 0)
    const = lambda i, e: (0, 0)
    n_tiles = n_exp // eb
    assert eb == PEER_HEADS * N_KEYS and n_tiles >= 2
    sel32 = pltpu.VMEM((PEER_HEADS, N_KEYS, t), _f32)
    sel16 = pltpu.VMEM((PEER_HEADS, N_KEYS, 2 * t), _bf16)
    return pl.pallas_call(
        functools.partial(_peer_kernel, alpha=alpha, d_model=d, n_tiles=n_tiles),
        grid=(n // t, n_tiles + 1),
        in_specs=[pl.BlockSpec((t, d), row), pl.BlockSpec((t, d), row),
                  pl.BlockSpec((1, 1, mod.shape[-1]), lambda i, e: (i // nblk_seq, 0, 0)),
                  pl.BlockSpec(wq.shape, const), pl.BlockSpec(k1.shape, const), pl.BlockSpec(k2.shape, const),
                  pl.BlockSpec((eb, d), lambda i, e: (jnp.minimum(e, n_tiles - 1), 0)),
                  pl.BlockSpec((d, eb), lambda i, e: (0, jnp.maximum(e - 1, 0))),
                  pl.BlockSpec((1, d), const), pl.BlockSpec((1, d), const)],
        out_specs=pl.BlockSpec((t, d), row),
        out_shape=jax.ShapeDtypeStruct((n, d), _f32),
        scratch_shapes=[sel32, sel16, sel32, sel32,
                        pltpu.VMEM((eb, t), _f32), pltpu.VMEM((eb, t), _f32),
                        pltpu.VMEM((eb, t), _bf16), pltpu.VMEM((eb, t), _bf16), pltpu.VMEM((d, t), _f32)],
        compiler_params=pltpu.CompilerParams(dimension_semantics=("parallel", "arbitrary"),
                                             vmem_limit_bytes=VMEM_LIMIT),
        name="peer",
    )(h2, x1, mod, wq, k1, k2, u, vt, g, b)


def _rope_tables(seq, head_dim):
    m = head_dim // 4
    lane = np.arange(LANES)
    d = lane % head_dim
    use_col = (d // (2 * m)) == 1
    within = d % (2 * m)
    second = within // m
    freqs = ROPE_BASE ** (-np.arange(m, dtype=np.float64) / m)
    f = freqs[within % m]
    tpos = np.arange(seq)
    pos = np.where(use_col[None, :], (tpos % GRID_W)[:, None], (tpos // GRID_W)[:, None]).astype(np.float64)
    ang = pos * f[None, :]
    sign = np.where(second == 0, -1.0, 1.0)
    return jnp.asarray(np.cos(ang), _f32), jnp.asarray(np.sin(ang) * sign[None, :], _f32)


def _layer_weights(l, w_in, attn_q_norm, attn_k_norm, mla_q_norm, mla_kv_norm, w_uq, w_ukv, gmlp_ws, gmlp_b, w_o):
    offs = np.cumsum([0, QA_W, KA_W, KA_W, M_Q_RANK, M_KV_RANK, M_ROPE, CW, CW])
    qa_perm = np.concatenate([np.arange(HEAD_DIM) + HEAD_DIM * (j + 4 * half)
                              for j in range(A_HEADS // 2) for half in range(2)])
    cols = np.concatenate([qa_perm, np.arange(offs[1], offs[5]),
                           np.tile(np.arange(offs[5], offs[6]), M_HEADS), np.arange(offs[6], offs[8])])
    qd = M_NOPE + M_ROPE
    uq_cols = np.concatenate([np.arange(M_NOPE) + h * qd for h in range(M_HEADS)]
                             + [np.arange(M_ROPE) + h * qd + M_NOPE for h in range(M_HEADS)])
    kd = M_NOPE + M_V
    ukv_cols = np.concatenate([np.arange(M_NOPE) + h * kd for h in range(M_HEADS)]
                              + [np.arange(M_V) + h * kd + M_NOPE for h in range(M_HEADS)])
    wo_rows = np.concatenate([qa_perm, np.arange(QA_W, w_o.shape[1])])
    seg = np.arange(QA_W) // HEAD_DIM
    return {
        'win': w_in[l][:, cols].astype(_bf16),
        'aqn': jnp.tile(attn_q_norm[l], A_HEADS)[None, :],
        'akn': jnp.tile(attn_k_norm[l], A_KV)[None, :],
        'mqn': mla_q_norm[l][None, :],
        'mkvn': mla_kv_norm[l][None, :],
        'wuq': w_uq[l][:, uq_cols].astype(_bf16),
        'wukv': w_ukv[l][:, ukv_cols].astype(_bf16),
        'ws': gmlp_ws[l].astype(_bf16),
        'bsf': jnp.repeat(gmlp_b[l].T, C_DIM, axis=1),
        'bd': jnp.asarray(seg[:, None] == seg[None, :], _bf16),
        'wo': w_o[l][wo_rows, :].astype(_bf16),
    }


def kernel(x_prompt, x_sample, cache_attn_k, cache_attn_v, cache_mla_ckv, cache_mla_krope, c, c_ctx, w_mod, b_mod, w_in, attn_q_norm, attn_k_norm, mla_q_norm, mla_kv_norm, w_uq, w_ukv, gmlp_ws, gmlp_b, w_o, ln1_g, ln1_b, ln2_g, ln2_b, peer_wq, peer_k1, peer_k2, peer_u, peer_v):
    batch, seq, d = x_prompt.shape
    dec_batch, dec_seq, _ = x_sample.shape
    depth = w_in.shape[0]
    past = cache_attn_k.shape[2]
    alpha = (2.0 * depth) ** 0.25

    n_rows = 8 * ((1 + dec_batch + 7) // 8)
    c_all = jnp.zeros((n_rows, d), _f32).at[0].set(c_ctx).at[1:1 + dec_batch].set(c)
    mod_all = _mod_call(c_all, w_mod, b_mod)

    rope = _rope_tables(dec_seq, HEAD_DIM) + _rope_tables(dec_seq, M_ROPE)
    xp = x_prompt.reshape(batch * seq, d)
    xs = x_sample.reshape(dec_batch * dec_seq, d)
    new = []
    for l in range(depth):
        wts = _layer_weights(l, w_in, attn_q_norm, attn_k_norm, mla_q_norm, mla_kv_norm, w_uq, w_ukv,
                             gmlp_ws, gmlp_b, w_o)
        peer_w = (peer_wq[l].astype(_bf16), peer_k1[l].astype(_bf16), peer_k2[l].astype(_bf16),
                  peer_u[l].astype(_bf16), peer_v[l].T.astype(_bf16))
        ln1 = (ln1_g[l][None, :], ln1_b[l][None, :])
        ln2 = (ln2_g[l][None, :], ln2_b[l][None, :])
        mod_ctx = mod_all[l, 0:1][:, None, :]
        mod_lat = mod_all[l, 1:1 + dec_batch][:, None, :]
        cache = (cache_attn_k[:, l].reshape(dec_batch, past, KA_W),
                 cache_attn_v[:, l].reshape(dec_batch, past, KA_W),
                 cache_mla_ckv[:, l],
                 jnp.tile(cache_mla_krope[:, l], (1, 1, M_HEADS)))

        pre = _pre_call(xp, mod_ctx, batch * seq, wts, None, latent=False)
        new.append(pre[7:])
        oa, om = _attn_call(pre[0:6], seq, None, None, latent=False)
        x1, h2 = _post_call(oa, om, pre[6], xp, mod_ctx, batch * seq, wts['wo'], *ln1, alpha)
        xp = _peer_call(h2, x1, mod_ctx, batch * seq, *peer_w, *ln2, alpha)

        pre = _pre_call(xs, mod_lat, dec_seq, wts, rope, latent=True)
        oa, om = _attn_call(pre[0:6], dec_seq, cache, wts['wukv'], latent=True)
        x1, h2 = _post_call(oa, om, pre[6], xs, mod_lat, dec_seq, wts['wo'], *ln1, alpha)
        xs = _peer_call(h2, x1, mod_lat, dec_seq, *peer_w, *ln2, alpha)

    def stack(i, tail):
        return jnp.stack([new[l][i].reshape((batch, seq) + tail) for l in range(depth)], axis=1)

    return (xp.reshape(batch, seq, d), xs.reshape(dec_batch, dec_seq, d),
            stack(0, (A_KV, HEAD_DIM)), stack(1, (A_KV, HEAD_DIM)), stack(2, (M_KV_RANK,)), stack(3, (M_ROPE,)))
```

```python
import functools
import math

import numpy as np
import jax
import jax.numpy as jnp
from jax import lax
from jax.experimental import pallas as pl
from jax.experimental.pallas import tpu as pltpu

A_HEADS = 8
A_KV = 2
HEAD_DIM = 64
M_HEADS = 4
M_Q_RANK = 256
M_KV_RANK = 128
M_NOPE = 64
M_ROPE = 32
M_V = 64
C_GROUPS = 4
C_DIM = 64
C_CHUNK = 128
PEER_HEADS = 8
N_KEYS = 128
PEER_HALF = 128
PEER_TOPK = 16
GRID_W = 64
EPS = 1e-6
ROPE_BASE = 10000.0

LANES = 128
ROWS_B = 16
NEG_BIG = -3.0e38

QA_W = A_HEADS * HEAD_DIM
KA_W = A_KV * HEAD_DIM
CQ_OFF = QA_W + 2 * KA_W
CKV_OFF = CQ_OFF + M_Q_RANK
KR_OFF = CKV_OFF + M_KV_RANK
UC_OFF = KR_OFF + M_HEADS * M_ROPE
CW = C_GROUPS * C_DIM
VC_OFF = UC_OFF + CW
IN_WP = VC_OFF + CW

PRE_TB = 1024
ATT_QB = 256
PEER_T = 512
PEER_EB = 1024
VMEM_LIMIT = 56 * 1024 * 1024

_bf16 = jnp.bfloat16
_f32 = jnp.float32


def _dot(a, b):
    return jnp.dot(a, b, preferred_element_type=_f32)


def _dot_nt(a, b):
    return lax.dot_general(a, b, (((1,), (1,)), ((), ())), preferred_element_type=_f32)


def _split(x):
    hi = x.astype(_bf16)
    lo = (x - hi.astype(_f32)).astype(_bf16)
    return hi, lo


def _segsum(x, ones_bd):
    hi, lo = _split(x)
    return _dot(hi, ones_bd) + _dot(lo, ones_bd)


def _ln_rows(x):
    mu = jnp.mean(x, axis=-1, keepdims=True)
    d = x - mu
    var = jnp.mean(d * d, axis=-1, keepdims=True)
    return d * lax.rsqrt(var + EPS)


def _rot(x, cos, sin, half):
    lane = lax.broadcasted_iota(jnp.int32, x.shape, 1)
    up = pltpu.roll(x, LANES - half, axis=1)
    dn = pltpu.roll(x, half, axis=1)
    partner = jnp.where((lane & half) == 0, up, dn)
    return x * cos + partner * sin


def _mod_kernel(c_ref, w_ref, b_ref, o_ref):
    c = c_ref[...]
    a = c * (1.0 / (1.0 + jnp.exp(-c)))
    a_hi, a_lo = _split(a)
    w_hi, w_lo = _split(w_ref[0])
    o_ref[0] = _dot(a_hi, w_hi) + _dot(a_lo, w_hi) + _dot(a_hi, w_lo) + b_ref[0]


def _mod_call(c_all, w_mod, b_mod):
    depth, d, d6 = w_mod.shape
    rows = c_all.shape[0]
    tn = 1024
    return pl.pallas_call(
        _mod_kernel,
        grid=(depth, d6 // tn),
        in_specs=[pl.BlockSpec((rows, d), lambda l, j: (0, 0)),
                  pl.BlockSpec((1, d, tn), lambda l, j: (l, 0, j)),
                  pl.BlockSpec((1, 1, tn), lambda l, j: (l, 0, j))],
        out_specs=pl.BlockSpec((1, rows, tn), lambda l, j: (l, 0, j)),
        out_shape=jax.ShapeDtypeStruct((depth, rows, d6), _f32),
        compiler_params=pltpu.CompilerParams(dimension_semantics=("parallel", "parallel"),
                                             vmem_limit_bytes=VMEM_LIMIT),
        name="mod",
    )(c_all, w_mod, b_mod.reshape(depth, 1, d6))


def _pre_kernel(*refs, latent, d_model):
    if latent:
        (x_ref, mod_ref, win_ref, aqn_ref, akn_ref, mqn_ref, mkvn_ref, wuq_ref, wukv_ref, ws_ref, bsf_ref,
         bd_ref, cosa_ref, sina_ref, cosm_ref, sinm_ref,
         qa_ref, ka_ref, va_ref, qm_ref, kvm_ref, kr_ref, oc_ref) = refs
    else:
        (x_ref, mod_ref, win_ref, aqn_ref, akn_ref, mqn_ref, mkvn_ref, wuq_ref, wukv_ref, ws_ref, bsf_ref,
         bd_ref,
         qa_ref, ka_ref, va_ref, qm_ref, kvm_ref, kr_ref, oc_ref,
         nk_ref, nv_ref, nckv_ref, nkr_ref) = refs
    D = d_model
    x = x_ref[...]
    mod = mod_ref[0]
    sh1 = mod[:, 0:D]
    sc1 = mod[:, D:2 * D]
    h = _ln_rows(x) * (1.0 + sc1) + sh1
    proj = _dot(h.astype(_bf16), win_ref[...])

    qa = proj[:, 0:QA_W]
    ms = _segsum(qa * qa, bd_ref[...]) * (1.0 / HEAD_DIM)
    qa = qa * lax.rsqrt(ms + EPS) * aqn_ref[...]
    ka = proj[:, QA_W:QA_W + KA_W]
    ms = _segsum(ka * ka, bd_ref[0:KA_W, 0:KA_W]) * (1.0 / HEAD_DIM)
    ka = ka * lax.rsqrt(ms + EPS) * akn_ref[...]
    va = proj[:, QA_W + KA_W:CQ_OFF]
    if not latent:
        nk_ref[...] = ka
        nv_ref[...] = va
    for j in range(QA_W // LANES):
        slab = qa[:, j * LANES:(j + 1) * LANES]
        if latent:
            slab = _rot(slab, cosa_ref[...], sina_ref[...], HEAD_DIM // 4)
        qa_ref[:, j * LANES:(j + 1) * LANES] = (slab * (1.0 / math.sqrt(HEAD_DIM))).astype(_bf16)
    if latent:
        ka = _rot(ka, cosa_ref[...], sina_ref[...], HEAD_DIM // 4)
    ka_ref[...] = ka.astype(_bf16)
    va_ref[...] = va.astype(_bf16)

    cq = proj[:, CQ_OFF:CKV_OFF]
    cq = cq * lax.rsqrt(jnp.mean(cq * cq, axis=-1, keepdims=True) + EPS) * mqn_ref[...]
    qm = _dot(cq.astype(_bf16), wuq_ref[...])
    nope_w = M_HEADS * M_NOPE
    qscale = 1.0 / math.sqrt(M_NOPE + M_ROPE)
    qm_ref[:, 0:nope_w] = (qm[:, 0:nope_w] * qscale).astype(_bf16)
    qr = qm[:, nope_w:]
    if latent:
        qr = _rot(qr, cosm_ref[...], sinm_ref[...], M_ROPE // 4)
    qm_ref[:, nope_w:] = (qr * qscale).astype(_bf16)
    ckv = proj[:, CKV_OFF:KR_OFF]
    ckv = ckv * lax.rsqrt(jnp.mean(ckv * ckv, axis=-1, keepdims=True) + EPS) * mkvn_ref[...]
    kvm_ref[...] = _dot(ckv.astype(_bf16), wukv_ref[...]).astype(_bf16)
    kr = proj[:, KR_OFF:UC_OFF]
    if not latent:
        nckv_ref[...] = ckv
        nkr_ref[...] = kr[:, 0:M_ROPE]
    else:
        kr = _rot(kr, cosm_ref[...], sinm_ref[...], M_ROPE // 4)
    kr_ref[...] = kr.astype(_bf16)

    uc = proj[:, UC_OFF:VC_OFF]
    vc = proj[:, VC_OFF:IN_WP]
    bd_c = bd_ref[0:CW, 0:CW]
    mu = _segsum(vc, bd_c) * (1.0 / C_DIM)
    dv = vc - mu
    var = _segsum(dv * dv, bd_c) * (1.0 / C_DIM)
    vg = (dv * lax.rsqrt(var + EPS)).astype(_bf16)
    lane_c = lax.broadcasted_iota(jnp.int32, (C_CHUNK, CW), 1)
    for r in range(x.shape[0] // C_CHUNK):
        rows = slice(r * C_CHUNK, (r + 1) * C_CHUNK)
        vgc = vg[rows]
        mixed = bsf_ref[...]
        for g in range(C_GROUPS):
            mg = _dot(ws_ref[g], vgc)
            mixed = mixed + jnp.where(lane_c // C_DIM == g, mg, 0.0)
        oc_ref[rows, :] = (uc[rows] * mixed).astype(_bf16)


def _pre_call(x2, mod, seq, wts, rope, latent):
    n, d = x2.shape
    tb = min(PRE_TB, seq)
    nblk_seq = seq // tb
    row = lambda i: (i, 0)
    const = lambda i: (0, 0)
    in_specs = [pl.BlockSpec((tb, d), row),
                pl.BlockSpec((1, 1, mod.shape[-1]), lambda i: (i // nblk_seq, 0, 0)),
                pl.BlockSpec(wts['win'].shape, const),
                pl.BlockSpec((1, QA_W), const), pl.BlockSpec((1, KA_W), const),
                pl.BlockSpec((1, M_Q_RANK), const), pl.BlockSpec((1, M_KV_RANK), const),
                pl.BlockSpec(wts['wuq'].shape, const), pl.BlockSpec(wts['wukv'].shape, const),
                pl.BlockSpec(wts['ws'].shape, lambda i: (0, 0, 0)),
                pl.BlockSpec((C_CHUNK, CW), const),
                pl.BlockSpec((QA_W, QA_W), const)]
    args = [x2, mod, wts['win'], wts['aqn'], wts['akn'], wts['mqn'], wts['mkvn'], wts['wuq'], wts['wukv'],
            wts['ws'], wts['bsf'], wts['bd']]
    if latent:
        pos = lambda i: (i % nblk_seq, 0)
        in_specs += [pl.BlockSpec((tb, LANES), pos)] * 4
        args += list(rope)
    widths = [QA_W, KA_W, KA_W, M_HEADS * (M_NOPE + M_ROPE), M_HEADS * (M_NOPE + M_V), M_HEADS * M_ROPE, CW]
    out_specs = [pl.BlockSpec((tb, w), row) for w in widths]
    out_shape = [jax.ShapeDtypeStruct((n, w), _bf16) for w in widths]
    if not latent:
        for w in (KA_W, KA_W, M_KV_RANK, M_ROPE):
            out_specs.append(pl.BlockSpec((tb, w), row))
            out_shape.append(jax.ShapeDtypeStruct((n, w), _f32))
    return pl.pallas_call(
        functools.partial(_pre_kernel, latent=latent, d_model=d),
        grid=(n // tb,),
        in_specs=in_specs, out_specs=out_specs, out_shape=out_shape,
        compiler_params=pltpu.CompilerParams(dimension_semantics=("parallel",), vmem_limit_bytes=VMEM_LIMIT),
        name="pre_lat" if latent else "pre_ctx",
    )(*args)


def _softmax_pv(q, keys, vals):
    s = [_dot_nt(q, k) for k in keys]
    m = s[0].max(axis=-1, keepdims=True)
    for si in s[1:]:
        m = jnp.maximum(m, si.max(axis=-1, keepdims=True))
    den = None
    o = None
    for si, v in zip(s, vals):
        e = jnp.exp(si - m)
        d = e.sum(axis=-1, keepdims=True)
        den = d if den is None else den + d
        pv = _dot(e.astype(_bf16), v)
        o = pv if o is None else o + pv
    return o * (1.0 / den)


def _attn_kernel(*refs, latent):
    if latent:
        (qa_ref, qm_ref, ka_ref, va_ref, kvm_ref, kr_ref, ck_ref, cv_ref, cckv_ref, ckr_ref, wukv_ref,
         oa_ref, om_ref) = refs
    else:
        qa_ref, qm_ref, ka_ref, va_ref, kvm_ref, kr_ref, oa_ref, om_ref = refs
    qb = qa_ref.shape[0]
    nope_w = M_HEADS * M_NOPE

    keys_a = [ka_ref[...]]
    vals_a = [va_ref[...]]
    if latent:
        keys_a = [ck_ref[0].astype(_bf16)] + keys_a
        vals_a = [cv_ref[0].astype(_bf16)] + vals_a
    lane = lax.broadcasted_iota(jnp.int32, (qb, LANES), 1)
    low = lane < HEAD_DIM
    for j in range(QA_W // LANES):
        slab = qa_ref[:, j * LANES:(j + 1) * LANES]
        q2 = jnp.concatenate([jnp.where(low, slab, jnp.zeros_like(slab)),
                              jnp.where(low, jnp.zeros_like(slab), slab)], axis=0)
        o2 = _softmax_pv(q2, keys_a, vals_a)
        oa_ref[:, j * LANES:(j + 1) * LANES] = jnp.where(low, o2[0:qb], o2[qb:]).astype(_bf16)

    kn = [kvm_ref[:, 0:nope_w]]
    vm = [kvm_ref[:, nope_w:]]
    krs = [kr_ref[...]]
    if latent:
        kvc = _dot(cckv_ref[0].astype(_bf16), wukv_ref[...]).astype(_bf16)
        kn = [kvc[:, 0:nope_w]] + kn
        vm = [kvc[:, nope_w:]] + vm
        krs = [ckr_ref[0].astype(_bf16)] + krs
    keys_m = [jnp.concatenate([a, b], axis=1) for a, b in zip(kn, krs)]
    qm = qm_ref[...]
    lane_q = lax.broadcasted_iota(jnp.int32, qm.shape, 1)
    head_q = jnp.where(lane_q < nope_w, lane_q // M_NOPE, (lane_q - nope_w) // M_ROPE)
    lane_o = lax.broadcasted_iota(jnp.int32, (qb, nope_w), 1)
    om = jnp.zeros((qb, nope_w), _f32)
    for p in range(M_HEADS // 2):
        q2 = jnp.concatenate([jnp.where(head_q == 2 * p + t, qm, jnp.zeros_like(qm)) for t in range(2)], axis=0)
        o2 = _softmax_pv(q2, keys_m, vm)
        for t in range(2):
            om = jnp.where(lane_o // M_V == 2 * p + t, o2[t * qb:(t + 1) * qb], om)
    om_ref[...] = om.astype(_bf16)


def _attn_call(pre, seq, cache, wukv, latent):
    qa, ka, va, qm, kvm, kr = pre
    n = qa.shape[0]
    nb = n // seq
    qb = min(ATT_QB, seq)
    nq = seq // qb
    qrow = lambda b, i: (b * nq + i, 0)
    krow = lambda b, i: (b, 0)
    in_specs = [pl.BlockSpec((qb, qa.shape[1]), qrow), pl.BlockSpec((qb, qm.shape[1]), qrow),
                pl.BlockSpec((seq, ka.shape[1]), krow), pl.BlockSpec((seq, va.shape[1]), krow),
                pl.BlockSpec((seq, kvm.shape[1]), krow), pl.BlockSpec((seq, kr.shape[1]), krow)]
    args = [qa, qm, ka, va, kvm, kr]
    if latent:
        for c in cache:
            in_specs.append(pl.BlockSpec((1,) + c.shape[1:], lambda b, i: (b, 0, 0)))
            args.append(c)
        in_specs.append(pl.BlockSpec(wukv.shape, lambda b, i: (0, 0)))
        args.append(wukv)
    return pl.pallas_call(
        functools.partial(_attn_kernel, latent=latent),
        grid=(nb, nq),
        in_specs=in_specs,
        out_specs=[pl.BlockSpec((qb, QA_W), qrow), pl.BlockSpec((qb, M_HEADS * M_V), qrow)],
        out_shape=[jax.ShapeDtypeStruct((n, QA_W), _bf16), jax.ShapeDtypeStruct((n, M_HEADS * M_V), _bf16)],
        compiler_params=pltpu.CompilerParams(dimension_semantics=("parallel", "parallel"),
                                             vmem_limit_bytes=VMEM_LIMIT),
        name="attn_lat" if latent else "attn_ctx",
    )(*args)


def _post_kernel(oa_ref, om_ref, oc_ref, x_ref, mod_ref, wo_ref, g_ref, b_ref, x1_ref, h2_ref, *, alpha, d_model):
    D = d_model
    a_w = oa_ref.shape[1]
    m_w = om_ref.shape[1]
    mix = (_dot(oa_ref[...], wo_ref[0:a_w, :]) + _dot(om_ref[...], wo_ref[a_w:a_w + m_w, :])
           + _dot(oc_ref[...], wo_ref[a_w + m_w:, :]))
    mod = mod_ref[0]
    g1 = mod[:, 2 * D:3 * D]
    sh2 = mod[:, 3 * D:4 * D]
    sc2 = mod[:, 4 * D:5 * D]
    x1 = _ln_rows(alpha * x_ref[...] + g1 * mix) * g_ref[...] + b_ref[...]
    x1_ref[...] = x1
    h2_ref[...] = (_ln_rows(x1) * (1.0 + sc2) + sh2).astype(_bf16)


def _post_call(oa, om, oc, x2, mod, seq, wo, g, b, alpha):
    n, d = x2.shape
    tb = min(PRE_TB, seq)
    nblk_seq = seq // tb
    row = lambda i: (i, 0)
    const = lambda i: (0, 0)
    return pl.pallas_call(
        functools.partial(_post_kernel, alpha=alpha, d_model=d),
        grid=(n // tb,),
        in_specs=[pl.BlockSpec((tb, oa.shape[1]), row), pl.BlockSpec((tb, om.shape[1]), row),
                  pl.BlockSpec((tb, oc.shape[1]), row), pl.BlockSpec((tb, d), row),
                  pl.BlockSpec((1, 1, mod.shape[-1]), lambda i: (i // nblk_seq, 0, 0)),
                  pl.BlockSpec(wo.shape, const), pl.BlockSpec((1, d), const), pl.BlockSpec((1, d), const)],
        out_specs=[pl.BlockSpec((tb, d), row), pl.BlockSpec((tb, d), row)],
        out_shape=[jax.ShapeDtypeStruct((n, d), _f32), jax.ShapeDtypeStruct((n, d), _bf16)],
        compiler_params=pltpu.CompilerParams(dimension_semantics=("parallel",), vmem_limit_bytes=VMEM_LIMIT),
        name="post",
    )(oa, om, oc, x2, mod, wo, g, b)


def _sorting_network(n):
    def merge(lo, hi, r):
        step = r * 2
        if step < hi - lo:
            yield from merge(lo, hi, step)
            yield from merge(lo + r, hi, step)
            yield from ((i, i + r) for i in range(lo + r, hi - r, step))
        else:
            yield (lo, lo + r)

    def sort(lo, hi):
        if hi - lo >= 1:
            mid = lo + (hi - lo) // 2
            yield from sort(lo, mid)
            yield from sort(mid + 1, hi)
            yield from merge(lo, hi, 1)

    return tuple(sort(0, n - 1))


SUBLANES = 8
_SORT16 = _sorting_network(N_KEYS // SUBLANES)


def _extract_top(s, n):
    wires = [s[SUBLANES * i:SUBLANES * (i + 1)] for i in range(s.shape[0] // SUBLANES)]
    wires += [None] * (N_KEYS // SUBLANES - len(wires))
    for a, b in _SORT16:
        x, y = wires[a], wires[b]
        if y is None:
            continue
        if x is None:
            wires[a], wires[b] = y, None
        else:
            wires[a], wires[b] = jnp.maximum(x, y), jnp.minimum(x, y)
    cols = [w for w in wires if w is not None]
    tops = []
    for j in range(n):
        m = jnp.max(cols[0], axis=0, keepdims=True)
        tops.append(m)
        if j + 1 < n:
            top = cols[0] == m
            keep = min(len(cols), n - 1 - j)
            cols = [jnp.where(top, cols[r + 1] if r + 1 < len(cols) else NEG_BIG, cols[r]) for r in range(keep)]
    return tops


def _stack_rows(rows):
    n = len(rows)
    rid = lax.broadcasted_iota(jnp.int32, (n, LANES), 0)
    out = jnp.broadcast_to(rows[0], (n, LANES))
    for j in range(1, n):
        out = jnp.where(rid == j, rows[j], out)
    return out


def _peer_select(s1, s2):
    k = PEER_TOPK
    v1 = _extract_top(s1, k + 1)
    v2 = _extract_top(s2, k + 1)
    rank2 = jnp.full(s2.shape, float(k), _f32)
    for j in reversed(range(k)):
        rank2 = jnp.where(s2 >= v2[j], float(j), rank2)
    col1 = _stack_rows(v1[0:k])
    col2 = _stack_rows(v2[0:k])
    pieces = [v1[0] + col2]
    for a in range(1, 8):
        pieces.append(v1[a] + col2[0:8])
    pieces.append(col1[8:16] + v2[0])
    tail = jnp.where(lax.broadcasted_iota(jnp.int32, (8, LANES), 0) == 0, v1[0] + v2[k],
                     jnp.where(lax.broadcasted_iota(jnp.int32, (8, LANES), 0) == 1, v1[k] + v2[0], NEG_BIG))
    pieces.append(tail)
    cand = jnp.concatenate(pieces, axis=0)
    tops = _extract_top(cand, k + 1)
    z = jnp.ones_like(tops[0])
    for j in range(1, k):
        z = z + jnp.exp(tops[j] - tops[0])
    tau = 0.5 * (tops[k - 1] + tops[k])
    cnt = jnp.zeros_like(s1)
    for b in range(8):
        cnt = jnp.where(s1 >= tau - v2[b], float(b + 1), cnt)
    extra = jnp.zeros_like(tau)
    for b in range(8, k):
        extra = extra + jnp.where(v1[0] >= tau - v2[b], 1.0, 0.0)
    cnt = cnt + jnp.where(s1 >= v1[0], extra, 0.0)
    e1 = jnp.exp(s1 - v1[0]) * (1.0 / z)
    e2 = jnp.exp(s2 - v2[0])
    return rank2, cnt, e1, e2


def _gelu2(x):
    c = math.sqrt(2.0 / math.pi)
    return x * (1.0 + jnp.tanh(x * (c + (c * 0.044715) * (x * x))))


def _peer_kernel(h2_ref, x1_ref, mod_ref, wq_ref, k1_ref, k2_ref, u_ref, vt_ref, g_ref, b_ref, y_ref,
                 s2_ref, re_ref, cnt_ref, e1_ref, a0_ref, a1_ref, p0_ref, p1_ref, acc_ref,
                 *, alpha, d_model, n_tiles):
    D = d_model
    e = pl.program_id(1)
    t = h2_ref.shape[0]
    nchunk = t // LANES
    half = t // 2
    slabs = u_ref.shape[0] // N_KEYS
    a_refs = (a0_ref, a1_ref)
    p_refs = (p0_ref, p1_ref)

    def gate_chunk(off, a_ref, p_ref):
        cols = pl.ds(off, LANES)
        rk_cols = pl.ds(2 * off, LANES)
        e2_cols = pl.ds(2 * off + LANES, LANES)
        for g in range(slabs // 8):
            rows8 = pl.ds(pl.multiple_of(e * slabs + g * 8, 8), 8)
            cnt8 = [cnt_ref[h, rows8, cols] for h in range(PEER_HEADS)]
            e18 = [e1_ref[h, rows8, cols] for h in range(PEER_HEADS)]
            for k in range(8):
                w = jnp.zeros((N_KEYS, LANES), _bf16)
                for h in range(PEER_HEADS):
                    hit = re_ref[h, :, rk_cols] < cnt8[h][k:k + 1].astype(_bf16)
                    w = w + jnp.where(hit, re_ref[h, :, e2_cols], jnp.zeros((), _bf16)) * e18[h][k:k + 1].astype(_bf16)
                rows = slice((g * 8 + k) * N_KEYS, (g * 8 + k + 1) * N_KEYS)
                p_ref[rows, cols] = _gelu2(a_ref[rows, cols]).astype(_bf16) * w

    def run(parity, front, back):
        cur, prev = parity, 1 - parity
        for hf in range(2):
            off = hf * half
            if front:
                a_refs[cur][:, pl.ds(off, half)] = _dot_nt(u_ref[...], h2_ref[pl.ds(off, half), :])
                for c in range(half // LANES):
                    gate_chunk(off + c * LANES, a_refs[cur], p_refs[cur])
            if back:
                acc_ref[:, pl.ds(off, half)] += _dot(vt_ref[...], p_refs[prev][:, pl.ds(off, half)])

    @pl.when(e == 0)
    def _select():
        q = _dot(h2_ref[...], wq_ref[...]).astype(_bf16)
        for h in range(PEER_HEADS):
            base = h * 2 * PEER_HALF
            rows = slice(h * N_KEYS, (h + 1) * N_KEYS)
            a1_ref[rows, :] = _dot_nt(k1_ref[...], q[:, base:base + PEER_HALF])
            s2_ref[h] = _dot_nt(k2_ref[...], q[:, base + PEER_HALF:base + 2 * PEER_HALF])

        def body(it, carry):
            c = pl.multiple_of((it % nchunk) * LANES, LANES)
            for j in range(2):
                h = 2 * (it // nchunk) + j
                r = pl.multiple_of(h * N_KEYS, N_KEYS)
                rank2, cnt, e1, e2 = _peer_select(a1_ref[pl.ds(r, N_KEYS), pl.ds(c, LANES)],
                                                  s2_ref[h, :, pl.ds(c, LANES)])
                re_ref[h, :, pl.ds(pl.multiple_of(2 * c, LANES), LANES)] = rank2.astype(_bf16)
                cnt_ref[h, :, pl.ds(c, LANES)] = cnt
                e1_ref[h, :, pl.ds(c, LANES)] = 0.5 * e1
                re_ref[h, :, pl.ds(pl.multiple_of(2 * c + LANES, LANES), LANES)] = e2.astype(_bf16)
            return carry

        lax.fori_loop(0, PEER_HEADS // 2 * nchunk, body, 0)
        acc_ref[...] = jnp.zeros_like(acc_ref)
        run(0, True, False)

    for parity in range(2):
        @pl.when((e >= 1) & (e < n_tiles) & (e % 2 == parity))
        def _steady():
            run(parity, True, True)

    @pl.when(e == n_tiles)
    def _finish():
        run(n_tiles % 2, False, True)
        mod = mod_ref[0]
        g2 = mod[:, 5 * D:6 * D]
        ff = acc_ref[...].T
        y_ref[...] = _ln_rows(alpha * x1_ref[...] + g2 * ff) * g_ref[...] + b_ref[...]


def _peer_call(h2, x1, mod, seq, wq, k1, k2, u, vt, g, b, alpha):
    n, d = x1.shape
    t = PEER_T
    eb = PEER_EB
    n_exp = u.shape[0]
    nblk_seq = seq // t
    row = lambda i, e: (i, 0)
    const = lambda i, e: (0, 0)
    n_tiles = n_exp // eb
    assert eb == PEER_HEADS * N_KEYS and n_tiles >= 2
    sel32 = pltpu.VMEM((PEER_HEADS, N_KEYS, t), _f32)
    sel16 = pltpu.VMEM((PEER_HEADS, N_KEYS, 2 * t), _bf16)
    return pl.pallas_call(
        functools.partial(_peer_kernel, alpha=alpha, d_model=d, n_tiles=n_tiles),
        grid=(n // t, n_tiles + 1),
        in_specs=[pl.BlockSpec((t, d), row), pl.BlockSpec((t, d), row),
                  pl.BlockSpec((1, 1, mod.shape[-1]), lambda i, e: (i // nblk_seq, 0, 0)),
                  pl.BlockSpec(wq.shape, const), pl.BlockSpec(k1.shape, const), pl.BlockSpec(k2.shape, const),
                  pl.BlockSpec((eb, d), lambda i, e: (jnp.minimum(e, n_tiles - 1), 0)),
                  pl.BlockSpec((d, eb), lambda i, e: (0, jnp.maximum(e - 1, 0))),
                  pl.BlockSpec((1, d), const), pl.BlockSpec((1, d), const)],
        out_specs=pl.BlockSpec((t, d), row),
        out_shape=jax.ShapeDtypeStruct((n, d), _f32),
        scratch_shapes=[sel32, sel16, sel32, sel32,
                        pltpu.VMEM((eb, t), _f32), pltpu.VMEM((eb, t), _f32),
                        pltpu.VMEM((eb, t), _bf16), pltpu.VMEM((eb, t), _bf16), pltpu.VMEM((d, t), _f32)],
        compiler_params=pltpu.CompilerParams(dimension_semantics=("parallel", "arbitrary"),
                                             vmem_limit_bytes=VMEM_LIMIT),
        name="peer",
    )(h2, x1, mod, wq, k1, k2, u, vt, g, b)


def _rope_tables(seq, head_dim):
    m = head_dim // 4
    lane = np.arange(LANES)
    d = lane % head_dim
    use_col = (d // (2 * m)) == 1
    within = d % (2 * m)
    second = within // m
    freqs = ROPE_BASE ** (-np.arange(m, dtype=np.float64) / m)
    f = freqs[within % m]
    tpos = np.arange(seq)
    pos = np.where(use_col[None, :], (tpos % GRID_W)[:, None], (tpos // GRID_W)[:, None]).astype(np.float64)
    ang = pos * f[None, :]
    sign = np.where(second == 0, -1.0, 1.0)
    return jnp.asarray(np.cos(ang), _f32), jnp.asarray(np.sin(ang) * sign[None, :], _f32)


def _layer_weights(l, w_in, attn_q_norm, attn_k_norm, mla_q_norm, mla_kv_norm, w_uq, w_ukv, gmlp_ws, gmlp_b, w_o):
    offs = np.cumsum([0, QA_W, KA_W, KA_W, M_Q_RANK, M_KV_RANK, M_ROPE, CW, CW])
    qa_perm = np.concatenate([np.arange(HEAD_DIM) + HEAD_DIM * (j + 4 * half)
                              for j in range(A_HEADS // 2) for half in range(2)])
    cols = np.concatenate([qa_perm, np.arange(offs[1], offs[5]),
                           np.tile(np.arange(offs[5], offs[6]), M_HEADS), np.arange(offs[6], offs[8])])
    qd = M_NOPE + M_ROPE
    uq_cols = np.concatenate([np.arange(M_NOPE) + h * qd for h in range(M_HEADS)]
                             + [np.arange(M_ROPE) + h * qd + M_NOPE for h in range(M_HEADS)])
    kd = M_NOPE + M_V
    ukv_cols = np.concatenate([np.arange(M_NOPE) + h * kd for h in range(M_HEADS)]
                              + [np.arange(M_V) + h * kd + M_NOPE for h in range(M_HEADS)])
    wo_rows = np.concatenate([qa_perm, np.arange(QA_W, w_o.shape[1])])
    seg = np.arange(QA_W) // HEAD_DIM
    return {
        'win': w_in[l][:, cols].astype(_bf16),
        'aqn': jnp.tile(attn_q_norm[l], A_HEADS)[None, :],
        'akn': jnp.tile(attn_k_norm[l], A_KV)[None, :],
        'mqn': mla_q_norm[l][None, :],
        'mkvn': mla_kv_norm[l][None, :],
        'wuq': w_uq[l][:, uq_cols].astype(_bf16),
        'wukv': w_ukv[l][:, ukv_cols].astype(_bf16),
        'ws': gmlp_ws[l].astype(_bf16),
        'bsf': jnp.repeat(gmlp_b[l].T, C_DIM, axis=1),
        'bd': jnp.asarray(seg[:, None] == seg[None, :], _bf16),
        'wo': w_o[l][wo_rows, :].astype(_bf16),
    }


def kernel(x_prompt, x_sample, cache_attn_k, cache_attn_v, cache_mla_ckv, cache_mla_krope, c, c_ctx, w_mod, b_mod, w_in, attn_q_norm, attn_k_norm, mla_q_norm, mla_kv_norm, w_uq, w_ukv, gmlp_ws, gmlp_b, w_o, ln1_g, ln1_b, ln2_g, ln2_b, peer_wq, peer_k1, peer_k2, peer_u, peer_v):
    batch, seq, d = x_prompt.shape
    dec_batch, dec_seq, _ = x_sample.shape
    depth = w_in.shape[0]
    past = cache_attn_k.shape[2]
    alpha = (2.0 * depth) ** 0.25

    n_rows = 8 * ((1 + dec_batch + 7) // 8)
    c_all = jnp.zeros((n_rows, d), _f32).at[0].set(c_ctx).at[1:1 + dec_batch].set(c)
    mod_all = _mod_call(c_all, w_mod, b_mod)

    rope = _rope_tables(dec_seq, HEAD_DIM) + _rope_tables(dec_seq, M_ROPE)
    xp = x_prompt.reshape(batch * seq, d)
    xs = x_sample.reshape(dec_batch * dec_seq, d)
    new = []
    for l in range(depth):
        wts = _layer_weights(l, w_in, attn_q_norm, attn_k_norm, mla_q_norm, mla_kv_norm, w_uq, w_ukv,
                             gmlp_ws, gmlp_b, w_o)
        peer_w = (peer_wq[l].astype(_bf16), peer_k1[l].astype(_bf16), peer_k2[l].astype(_bf16),
                  peer_u[l].astype(_bf16), peer_v[l].T.astype(_bf16))
        ln1 = (ln1_g[l][None, :], ln1_b[l][None, :])
        ln2 = (ln2_g[l][None, :], ln2_b[l][None, :])
        mod_ctx = mod_all[l, 0:1][:, None, :]
        mod_lat = mod_all[l, 1:1 + dec_batch][:, None, :]
        cache = (cache_attn_k[:, l].reshape(dec_batch, past, KA_W),
                 cache_attn_v[:, l].reshape(dec_batch, past, KA_W),
                 cache_mla_ckv[:, l],
                 jnp.tile(cache_mla_krope[:, l], (1, 1, M_HEADS)))

        pre = _pre_call(xp, mod_ctx, batch * seq, wts, None, latent=False)
        new.append(pre[7:])
        oa, om = _attn_call(pre[0:6], seq, None, None, latent=False)
        x1, h2 = _post_call(oa, om, pre[6], xp, mod_ctx, batch * seq, wts['wo'], *ln1, alpha)
        xp = _peer_call(h2, x1, mod_ctx, batch * seq, *peer_w, *ln2, alpha)

        pre = _pre_call(xs, mod_lat, dec_seq, wts, rope, latent=True)
        oa, om = _attn_call(pre[0:6], dec_seq, cache, wts['wukv'], latent=True)
        x1, h2 = _post_call(oa, om, pre[6], xs, mod_lat, dec_seq, wts['wo'], *ln1, alpha)
        xs = _peer_call(h2, x1, mod_lat, dec_seq, *peer_w, *ln2, alpha)

    def stack(i, tail):
        return jnp.stack([new[l][i].reshape((batch, seq) + tail) for l in range(depth)], axis=1)

    return (xp.reshape(batch, seq, d), xs.reshape(dec_batch, dec_seq, d),
            stack(0, (A_KV, HEAD_DIM)), stack(1, (A_KV, HEAD_DIM)), stack(2, (M_KV_RANK,)), stack(3, (M_ROPE,)))
```

```python
import functools
import math

import numpy as np
import jax
import jax.numpy as jnp
from jax import lax
from jax.experimental import pallas as pl
from jax.experimental.pallas import tpu as pltpu

A_HEADS = 8
A_KV = 2
HEAD_DIM = 64
M_HEADS = 4
M_Q_RANK = 256
M_KV_RANK = 128
M_NOPE = 64
M_ROPE = 32
M_V = 64
C_GROUPS = 4
C_DIM = 64
C_CHUNK = 128
PEER_HEADS = 8
N_KEYS = 128
PEER_HALF = 128
PEER_TOPK = 16
GRID_W = 64
EPS = 1e-6
ROPE_BASE = 10000.0

LANES = 128
ROWS_B = 16
NEG_BIG = -3.0e38

QA_W = A_HEADS * HEAD_DIM
KA_W = A_KV * HEAD_DIM
CQ_OFF = QA_W + 2 * KA_W
CKV_OFF = CQ_OFF + M_Q_RANK
KR_OFF = CKV_OFF + M_KV_RANK
UC_OFF = KR_OFF + M_HEADS * M_ROPE
CW = C_GROUPS * C_DIM
VC_OFF = UC_OFF + CW
IN_WP = VC_OFF + CW

PRE_TB = 1024
ATT_QB = 256
PEER_T = 512
PEER_EB = 1024
VMEM_LIMIT = 56 * 1024 * 1024

_bf16 = jnp.bfloat16
_f32 = jnp.float32


def _dot(a, b):
    return jnp.dot(a, b, preferred_element_type=_f32)


def _dot_nt(a, b):
    return lax.dot_general(a, b, (((1,), (1,)), ((), ())), preferred_element_type=_f32)


def _split(x):
    hi = x.astype(_bf16)
    lo = (x - hi.astype(_f32)).astype(_bf16)
    return hi, lo


def _segsum(x, ones_bd):
    hi, lo = _split(x)
    return _dot(hi, ones_bd) + _dot(lo, ones_bd)


def _ln_rows(x):
    mu = jnp.mean(x, axis=-1, keepdims=True)
    d = x - mu
    var = jnp.mean(d * d, axis=-1, keepdims=True)
    return d * lax.rsqrt(var + EPS)


def _rot(x, cos, sin, half):
    lane = lax.broadcasted_iota(jnp.int32, x.shape, 1)
    up = pltpu.roll(x, LANES - half, axis=1)
    dn = pltpu.roll(x, half, axis=1)
    partner = jnp.where((lane & half) == 0, up, dn)
    return x * cos + partner * sin


def _mod_kernel(c_ref, w_ref, b_ref, o_ref):
    c = c_ref[...]
    a = c * (1.0 / (1.0 + jnp.exp(-c)))
    a_hi, a_lo = _split(a)
    w_hi, w_lo = _split(w_ref[0])
    o_ref[0] = _dot(a_hi, w_hi) + _dot(a_lo, w_hi) + _dot(a_hi, w_lo) + b_ref[0]


def _mod_call(c_all, w_mod, b_mod):
    depth, d, d6 = w_mod.shape
    rows = c_all.shape[0]
    tn = 1024
    return pl.pallas_call(
        _mod_kernel,
        grid=(depth, d6 // tn),
        in_specs=[pl.BlockSpec((rows, d), lambda l, j: (0, 0)),
                  pl.BlockSpec((1, d, tn), lambda l, j: (l, 0, j)),
                  pl.BlockSpec((1, 1, tn), lambda l, j: (l, 0, j))],
        out_specs=pl.BlockSpec((1, rows, tn), lambda l, j: (l, 0, j)),
        out_shape=jax.ShapeDtypeStruct((depth, rows, d6), _f32),
        compiler_params=pltpu.CompilerParams(dimension_semantics=("parallel", "parallel"),
                                             vmem_limit_bytes=VMEM_LIMIT),
        name="mod",
    )(c_all, w_mod, b_mod.reshape(depth, 1, d6))


def _pre_kernel(*refs, latent, d_model):
    if latent:
        (x_ref, mod_ref, win_ref, aqn_ref, akn_ref, mqn_ref, mkvn_ref, wuq_ref, wukv_ref, ws_ref, bsf_ref,
         bd_ref, cosa_ref, sina_ref, cosm_ref, sinm_ref,
         qa_ref, ka_ref, va_ref, qm_ref, kvm_ref, kr_ref, oc_ref) = refs
    else:
        (x_ref, mod_ref, win_ref, aqn_ref, akn_ref, mqn_ref, mkvn_ref, wuq_ref, wukv_ref, ws_ref, bsf_ref,
         bd_ref,
         qa_ref, ka_ref, va_ref, qm_ref, kvm_ref, kr_ref, oc_ref,
         nk_ref, nv_ref, nckv_ref, nkr_ref) = refs
    D = d_model
    x = x_ref[...]
    mod = mod_ref[0]
    sh1 = mod[:, 0:D]
    sc1 = mod[:, D:2 * D]
    h = _ln_rows(x) * (1.0 + sc1) + sh1
    proj = _dot(h.astype(_bf16), win_ref[...])

    qa = proj[:, 0:QA_W]
    ms = _segsum(qa * qa, bd_ref[...]) * (1.0 / HEAD_DIM)
    qa = qa * lax.rsqrt(ms + EPS) * aqn_ref[...]
    ka = proj[:, QA_W:QA_W + KA_W]
    ms = _segsum(ka * ka, bd_ref[0:KA_W, 0:KA_W]) * (1.0 / HEAD_DIM)
    ka = ka * lax.rsqrt(ms + EPS) * akn_ref[...]
    va = proj[:, QA_W + KA_W:CQ_OFF]
    if not latent:
        nk_ref[...] = ka
        nv_ref[...] = va
    for j in range(QA_W // LANES):
        slab = qa[:, j * LANES:(j + 1) * LANES]
        if latent:
            slab = _rot(slab, cosa_ref[...], sina_ref[...], HEAD_DIM // 4)
        qa_ref[:, j * LANES:(j + 1) * LANES] = (slab * (1.0 / math.sqrt(HEAD_DIM))).astype(_bf16)
    if latent:
        ka = _rot(ka, cosa_ref[...], sina_ref[...], HEAD_DIM // 4)
    ka_ref[...] = ka.astype(_bf16)
    va_ref[...] = va.astype(_bf16)

    cq = proj[:, CQ_OFF:CKV_OFF]
    cq = cq * lax.rsqrt(jnp.mean(cq * cq, axis=-1, keepdims=True) + EPS) * mqn_ref[...]
    qm = _dot(cq.astype(_bf16), wuq_ref[...])
    nope_w = M_HEADS * M_NOPE
    qscale = 1.0 / math.sqrt(M_NOPE + M_ROPE)
    qm_ref[:, 0:nope_w] = (qm[:, 0:nope_w] * qscale).astype(_bf16)
    qr = qm[:, nope_w:]
    if latent:
        qr = _rot(qr, cosm_ref[...], sinm_ref[...], M_ROPE // 4)
    qm_ref[:, nope_w:] = (qr * qscale).astype(_bf16)
    ckv = proj[:, CKV_OFF:KR_OFF]
    ckv = ckv * lax.rsqrt(jnp.mean(ckv * ckv, axis=-1, keepdims=True) + EPS) * mkvn_ref[...]
    kvm_ref[...] = _dot(ckv.astype(_bf16), wukv_ref[...]).astype(_bf16)
    kr = proj[:, KR_OFF:UC_OFF]
    if not latent:
        nckv_ref[...] = ckv
        nkr_ref[...] = kr[:, 0:M_ROPE]
    else:
        kr = _rot(kr, cosm_ref[...], sinm_ref[...], M_ROPE // 4)
    kr_ref[...] = kr.astype(_bf16)

    uc = proj[:, UC_OFF:VC_OFF]
    vc = proj[:, VC_OFF:IN_WP]
    bd_c = bd_ref[0:CW, 0:CW]
    mu = _segsum(vc, bd_c) * (1.0 / C_DIM)
    dv = vc - mu
    var = _segsum(dv * dv, bd_c) * (1.0 / C_DIM)
    vg = (dv * lax.rsqrt(var + EPS)).astype(_bf16)
    lane_c = lax.broadcasted_iota(jnp.int32, (C_CHUNK, CW), 1)
    for r in range(x.shape[0] // C_CHUNK):
        rows = slice(r * C_CHUNK, (r + 1) * C_CHUNK)
        vgc = vg[rows]
        mixed = bsf_ref[...]
        for g in range(C_GROUPS):
            mg = _dot(ws_ref[g], vgc)
            mixed = mixed + jnp.where(lane_c // C_DIM == g, mg, 0.0)
        oc_ref[rows, :] = (uc[rows] * mixed).astype(_bf16)


def _pre_call(x2, mod, seq, wts, rope, latent):
    n, d = x2.shape
    tb = min(PRE_TB, seq)
    nblk_seq = seq // tb
    row = lambda i: (i, 0)
    const = lambda i: (0, 0)
    in_specs = [pl.BlockSpec((tb, d), row),
                pl.BlockSpec((1, 1, mod.shape[-1]), lambda i: (i // nblk_seq, 0, 0)),
                pl.BlockSpec(wts['win'].shape, const),
                pl.BlockSpec((1, QA_W), const), pl.BlockSpec((1, KA_W), const),
                pl.BlockSpec((1, M_Q_RANK), const), pl.BlockSpec((1, M_KV_RANK), const),
                pl.BlockSpec(wts['wuq'].shape, const), pl.BlockSpec(wts['wukv'].shape, const),
                pl.BlockSpec(wts['ws'].shape, lambda i: (0, 0, 0)),
                pl.BlockSpec((C_CHUNK, CW), const),
                pl.BlockSpec((QA_W, QA_W), const)]
    args = [x2, mod, wts['win'], wts['aqn'], wts['akn'], wts['mqn'], wts['mkvn'], wts['wuq'], wts['wukv'],
            wts['ws'], wts['bsf'], wts['bd']]
    if latent:
        pos = lambda i: (i % nblk_seq, 0)
        in_specs += [pl.BlockSpec((tb, LANES), pos)] * 4
        args += list(rope)
    widths = [QA_W, KA_W, KA_W, M_HEADS * (M_NOPE + M_ROPE), M_HEADS * (M_NOPE + M_V), M_HEADS * M_ROPE, CW]
    out_specs = [pl.BlockSpec((tb, w), row) for w in widths]
    out_shape = [jax.ShapeDtypeStruct((n, w), _bf16) for w in widths]
    if not latent:
        for w in (KA_W, KA_W, M_KV_RANK, M_ROPE):
            out_specs.append(pl.BlockSpec((tb, w), row))
            out_shape.append(jax.ShapeDtypeStruct((n, w), _f32))
    return pl.pallas_call(
        functools.partial(_pre_kernel, latent=latent, d_model=d),
        grid=(n // tb,),
        in_specs=in_specs, out_specs=out_specs, out_shape=out_shape,
        compiler_params=pltpu.CompilerParams(dimension_semantics=("parallel",), vmem_limit_bytes=VMEM_LIMIT),
        name="pre_lat" if latent else "pre_ctx",
    )(*args)


def _softmax_pv(q, keys, vals):
    s = [_dot_nt(q, k) for k in keys]
    m = s[0].max(axis=-1, keepdims=True)
    for si in s[1:]:
        m = jnp.maximum(m, si.max(axis=-1, keepdims=True))
    den = None
    o = None
    for si, v in zip(s, vals):
        e = jnp.exp(si - m)
        d = e.sum(axis=-1, keepdims=True)
        den = d if den is None else den + d
        pv = _dot(e.astype(_bf16), v)
        o = pv if o is None else o + pv
    return o * (1.0 / den)


def _attn_kernel(*refs, latent):
    if latent:
        (qa_ref, qm_ref, ka_ref, va_ref, kvm_ref, kr_ref, ck_ref, cv_ref, cckv_ref, ckr_ref, wukv_ref,
         oa_ref, om_ref) = refs
    else:
        qa_ref, qm_ref, ka_ref, va_ref, kvm_ref, kr_ref, oa_ref, om_ref = refs
    qb = qa_ref.shape[0]
    nope_w = M_HEADS * M_NOPE

    keys_a = [ka_ref[...]]
    vals_a = [va_ref[...]]
    if latent:
        keys_a = [ck_ref[0].astype(_bf16)] + keys_a
        vals_a = [cv_ref[0].astype(_bf16)] + vals_a
    lane = lax.broadcasted_iota(jnp.int32, (qb, LANES), 1)
    low = lane < HEAD_DIM
    for j in range(QA_W // LANES):
        slab = qa_ref[:, j * LANES:(j + 1) * LANES]
        q2 = jnp.concatenate([jnp.where(low, slab, jnp.zeros_like(slab)),
                              jnp.where(low, jnp.zeros_like(slab), slab)], axis=0)
        o2 = _softmax_pv(q2, keys_a, vals_a)
        oa_ref[:, j * LANES:(j + 1) * LANES] = jnp.where(low, o2[0:qb], o2[qb:]).astype(_bf16)

    kn = [kvm_ref[:, 0:nope_w]]
    vm = [kvm_ref[:, nope_w:]]
    krs = [kr_ref[...]]
    if latent:
        kvc = _dot(cckv_ref[0].astype(_bf16), wukv_ref[...]).astype(_bf16)
        kn = [kvc[:, 0:nope_w]] + kn
        vm = [kvc[:, nope_w:]] + vm
        krs = [ckr_ref[0].astype(_bf16)] + krs
    qr = qm_ref[:, nope_w:]
    head_r = lax.broadcasted_iota(jnp.int32, qr.shape, 1) // M_ROPE
    lane_o = lax.broadcasted_iota(jnp.int32, (qb, nope_w), 1)
    om = jnp.zeros((qb, nope_w), _f32)
    for p in range(M_HEADS // 2):
        pair = slice(p * LANES, (p + 1) * LANES)
        qn = qm_ref[:, pair]
        zero = jnp.zeros_like(qn)
        q2 = jnp.concatenate(
            [jnp.concatenate([jnp.where(low == (t == 0), qn, zero), jnp.where(head_r == 2 * p + t, qr, zero)], axis=1)
             for t in range(2)], axis=0)
        keys_m = [jnp.concatenate([a[:, pair], b], axis=1) for a, b in zip(kn, krs)]
        o2 = _softmax_pv(q2, keys_m, vm)
        for t in range(2):
            om = jnp.where(lane_o // M_V == 2 * p + t, o2[t * qb:(t + 1) * qb], om)
    om_ref[...] = om.astype(_bf16)


def _attn_call(pre, seq, cache, wukv, latent):
    qa, ka, va, qm, kvm, kr = pre
    n = qa.shape[0]
    nb = n // seq
    qb = min(ATT_QB, seq)
    nq = seq // qb
    qrow = lambda b, i: (b * nq + i, 0)
    krow = lambda b, i: (b, 0)
    in_specs = [pl.BlockSpec((qb, qa.shape[1]), qrow), pl.BlockSpec((qb, qm.shape[1]), qrow),
                pl.BlockSpec((seq, ka.shape[1]), krow), pl.BlockSpec((seq, va.shape[1]), krow),
                pl.BlockSpec((seq, kvm.shape[1]), krow), pl.BlockSpec((seq, kr.shape[1]), krow)]
    args = [qa, qm, ka, va, kvm, kr]
    if latent:
        for c in cache:
            in_specs.append(pl.BlockSpec((1,) + c.shape[1:], lambda b, i: (b, 0, 0)))
            args.append(c)
        in_specs.append(pl.BlockSpec(wukv.shape, lambda b, i: (0, 0)))
        args.append(wukv)
    return pl.pallas_call(
        functools.partial(_attn_kernel, latent=latent),
        grid=(nb, nq),
        in_specs=in_specs,
        out_specs=[pl.BlockSpec((qb, QA_W), qrow), pl.BlockSpec((qb, M_HEADS * M_V), qrow)],
        out_shape=[jax.ShapeDtypeStruct((n, QA_W), _bf16), jax.ShapeDtypeStruct((n, M_HEADS * M_V), _bf16)],
        compiler_params=pltpu.CompilerParams(dimension_semantics=("parallel", "parallel"),
                                             vmem_limit_bytes=VMEM_LIMIT),
        name="attn_lat" if latent else "attn_ctx",
    )(*args)


def _post_kernel(oa_ref, om_ref, oc_ref, x_ref, mod_ref, wo_ref, g_ref, b_ref, x1_ref, h2_ref, *, alpha, d_model):
    D = d_model
    a_w = oa_ref.shape[1]
    m_w = om_ref.shape[1]
    mix = (_dot(oa_ref[...], wo_ref[0:a_w, :]) + _dot(om_ref[...], wo_ref[a_w:a_w + m_w, :])
           + _dot(oc_ref[...], wo_ref[a_w + m_w:, :]))
    mod = mod_ref[0]
    g1 = mod[:, 2 * D:3 * D]
    sh2 = mod[:, 3 * D:4 * D]
    sc2 = mod[:, 4 * D:5 * D]
    x1 = _ln_rows(alpha * x_ref[...] + g1 * mix) * g_ref[...] + b_ref[...]
    x1_ref[...] = x1
    h2_ref[...] = (_ln_rows(x1) * (1.0 + sc2) + sh2).astype(_bf16)


def _post_call(oa, om, oc, x2, mod, seq, wo, g, b, alpha):
    n, d = x2.shape
    tb = min(PRE_TB, seq)
    nblk_seq = seq // tb
    row = lambda i: (i, 0)
    const = lambda i: (0, 0)
    return pl.pallas_call(
        functools.partial(_post_kernel, alpha=alpha, d_model=d),
        grid=(n // tb,),
        in_specs=[pl.BlockSpec((tb, oa.shape[1]), row), pl.BlockSpec((tb, om.shape[1]), row),
                  pl.BlockSpec((tb, oc.shape[1]), row), pl.BlockSpec((tb, d), row),
                  pl.BlockSpec((1, 1, mod.shape[-1]), lambda i: (i // nblk_seq, 0, 0)),
                  pl.BlockSpec(wo.shape, const), pl.BlockSpec((1, d), const), pl.BlockSpec((1, d), const)],
        out_specs=[pl.BlockSpec((tb, d), row), pl.BlockSpec((tb, d), row)],
        out_shape=[jax.ShapeDtypeStruct((n, d), _f32), jax.ShapeDtypeStruct((n, d), _bf16)],
        compiler_params=pltpu.CompilerParams(dimension_semantics=("parallel",), vmem_limit_bytes=VMEM_LIMIT),
        name="post",
    )(oa, om, oc, x2, mod, wo, g, b)


def _sorting_network(n):
    def merge(lo, hi, r):
        step = r * 2
        if step < hi - lo:
            yield from merge(lo, hi, step)
            yield from merge(lo + r, hi, step)
            yield from ((i, i + r) for i in range(lo + r, hi - r, step))
        else:
            yield (lo, lo + r)

    def sort(lo, hi):
        if hi - lo >= 1:
            mid = lo + (hi - lo) // 2
            yield from sort(lo, mid)
            yield from sort(mid + 1, hi)
            yield from merge(lo, hi, 1)

    return tuple(sort(0, n - 1))


SUBLANES = 8
_SORT16 = _sorting_network(N_KEYS // SUBLANES)


def _extract_top(s, n):
    wires = [s[SUBLANES * i:SUBLANES * (i + 1)] for i in range(s.shape[0] // SUBLANES)]
    wires += [None] * (N_KEYS // SUBLANES - len(wires))
    for a, b in _SORT16:
        x, y = wires[a], wires[b]
        if y is None:
            continue
        if x is None:
            wires[a], wires[b] = y, None
        else:
            wires[a], wires[b] = jnp.maximum(x, y), jnp.minimum(x, y)
    cols = [w for w in wires if w is not None]
    tops = []
    for j in range(n):
        m = jnp.max(cols[0], axis=0, keepdims=True)
        tops.append(m)
        if j + 1 < n:
            top = cols[0] == m
            keep = min(len(cols), n - 1 - j)
            cols = [jnp.where(top, cols[r + 1] if r + 1 < len(cols) else NEG_BIG, cols[r]) for r in range(keep)]
    return tops


def _stack_rows(rows):
    n = len(rows)
    rid = lax.broadcasted_iota(jnp.int32, (n, LANES), 0)
    out = jnp.broadcast_to(rows[0], (n, LANES))
    for j in range(1, n):
        out = jnp.where(rid == j, rows[j], out)
    return out


def _peer_select(s1, s2):
    k = PEER_TOPK
    v1 = _extract_top(s1, k + 1)
    v2 = _extract_top(s2, k + 1)
    rank2 = jnp.full(s2.shape, float(k), _f32)
    for j in reversed(range(k)):
        rank2 = jnp.where(s2 >= v2[j], float(j), rank2)
    col1 = _stack_rows(v1[0:k])
    col2 = _stack_rows(v2[0:k])
    pieces = [v1[0] + col2]
    for a in range(1, 8):
        pieces.append(v1[a] + col2[0:8])
    pieces.append(col1[8:16] + v2[0])
    tail = jnp.where(lax.broadcasted_iota(jnp.int32, (8, LANES), 0) == 0, v1[0] + v2[k],
                     jnp.where(lax.broadcasted_iota(jnp.int32, (8, LANES), 0) == 1, v1[k] + v2[0], NEG_BIG))
    pieces.append(tail)
    cand = jnp.concatenate(pieces, axis=0)
    tops = _extract_top(cand, k + 1)
    z = jnp.ones_like(tops[0])
    for j in range(1, k):
        z = z + jnp.exp(tops[j] - tops[0])
    tau = 0.5 * (tops[k - 1] + tops[k])
    cnt = jnp.zeros_like(s1)
    for b in range(8):
        cnt = jnp.where(s1 >= tau - v2[b], float(b + 1), cnt)
    extra = jnp.zeros_like(tau)
    for b in range(8, k):
        extra = extra + jnp.where(v1[0] >= tau - v2[b], 1.0, 0.0)
    cnt = cnt + jnp.where(s1 >= v1[0], extra, 0.0)
    e1 = jnp.exp(s1 - v1[0]) * (1.0 / z)
    e2 = jnp.exp(s2 - v2[0])
    return rank2, cnt, e1, e2


def _gelu2(x):
    c = math.sqrt(2.0 / math.pi)
    return x * (1.0 + jnp.tanh(x * (c + (c * 0.044715) * (x * x))))


def _peer_kernel(h2_ref, x1_ref, mod_ref, wq_ref, k1_ref, k2_ref, u_ref, vt_ref, g_ref, b_ref, y_ref,
                 s2_ref, re_ref, cnt_ref, e1_ref, a0_ref, a1_ref, p0_ref, p1_ref, acc_ref,
                 *, alpha, d_model, n_tiles):
    D = d_model
    e = pl.program_id(1)
    t = h2_ref.shape[0]
    nchunk = t // LANES
    half = t // 2
    slabs = u_ref.shape[0] // N_KEYS
    a_refs = (a0_ref, a1_ref)
    p_refs = (p0_ref, p1_ref)

    def gate_chunk(off, a_ref, p_ref):
        cols = pl.ds(off, LANES)
        rk_cols = pl.ds(2 * off, LANES)
        e2_cols = pl.ds(2 * off + LANES, LANES)
        for g in range(slabs // 8):
            rows8 = pl.ds(pl.multiple_of(e * slabs + g * 8, 8), 8)
            cnt8 = [cnt_ref[h, rows8, cols] for h in range(PEER_HEADS)]
            e18 = [e1_ref[h, rows8, cols] for h in range(PEER_HEADS)]
            for k in range(8):
                w = jnp.zeros((N_KEYS, LANES), _bf16)
                for h in range(PEER_HEADS):
                    hit = re_ref[h, :, rk_cols] < cnt8[h][k:k + 1].astype(_bf16)
                    w = w + jnp.where(hit, re_ref[h, :, e2_cols], jnp.zeros((), _bf16)) * e18[h][k:k + 1].astype(_bf16)
                rows = slice((g * 8 + k) * N_KEYS, (g * 8 + k + 1) * N_KEYS)
                p_ref[rows, cols] = _gelu2(a_ref[rows, cols]).astype(_bf16) * w

    def run(parity, front, back):
        cur, prev = parity, 1 - parity
        for hf in range(2):
            off = hf * half
            if front:
                a_refs[cur][:, pl.ds(off, half)] = _dot_nt(u_ref[...], h2_ref[pl.ds(off, half), :])
                for c in range(half // LANES):
                    gate_chunk(off + c * LANES, a_refs[cur], p_refs[cur])
            if back:
                acc_ref[:, pl.ds(off, half)] += _dot(vt_ref[...], p_refs[prev][:, pl.ds(off, half)])

    @pl.when(e == 0)
    def _select():
        q = _dot(h2_ref[...], wq_ref[...]).astype(_bf16)
        for h in range(PEER_HEADS):
            base = h * 2 * PEER_HALF
            rows = slice(h * N_KEYS, (h + 1) * N_KEYS)
            a1_ref[rows, :] = _dot_nt(k1_ref[...], q[:, base:base + PEER_HALF])
            s2_ref[h] = _dot_nt(k2_ref[...], q[:, base + PEER_HALF:base + 2 * PEER_HALF])

        def body(it, carry):
            c = pl.multiple_of((it % nchunk) * LANES, LANES)
            for j in range(2):
                h = 2 * (it // nchunk) + j
                r = pl.multiple_of(h * N_KEYS, N_KEYS)
                rank2, cnt, e1, e2 = _peer_select(a1_ref[pl.ds(r, N_KEYS), pl.ds(c, LANES)],
                                                  s2_ref[h, :, pl.ds(c, LANES)])
                re_ref[h, :, pl.ds(pl.multiple_of(2 * c, LANES), LANES)] = rank2.astype(_bf16)
                cnt_ref[h, :, pl.ds(c, LANES)] = cnt
                e1_ref[h, :, pl.ds(c, LANES)] = 0.5 * e1
                re_ref[h, :, pl.ds(pl.multiple_of(2 * c + LANES, LANES), LANES)] = e2.astype(_bf16)
            return carry

        lax.fori_loop(0, PEER_HEADS // 2 * nchunk, body, 0)
        acc_ref[...] = jnp.zeros_like(acc_ref)
        run(0, True, False)

    for parity in range(2):
        @pl.when((e >= 1) & (e < n_tiles) & (e % 2 == parity))
        def _steady():
            run(parity, True, True)

    @pl.when(e == n_tiles)
    def _finish():
        run(n_tiles % 2, False, True)
        mod = mod_ref[0]
        g2 = mod[:, 5 * D:6 * D]
        ff = acc_ref[...].T
        y_ref[...] = _ln_rows(alpha * x1_ref[...] + g2 * ff) * g_ref[...] + b_ref[...]


def _peer_call(h2, x1, mod, seq, wq, k1, k2, u, vt, g, b, alpha):
    n, d = x1.shape
    t = PEER_T
    eb = PEER_EB
    n_exp = u.shape[0]
    nblk_seq = seq // t
    row = lambda i, e: (i, 0)
    const = lambda i, e: (0, 0)
    n_tiles = n_exp // eb
    assert eb == PEER_HEADS * N_KEYS and n_tiles >= 2
    sel32 = pltpu.VMEM((PEER_HEADS, N_KEYS, t), _f32)
    sel16 = pltpu.VMEM((PEER_HEADS, N_KEYS, 2 * t), _bf16)
    return pl.pallas_call(
        functools.partial(_peer_kernel, alpha=alpha, d_model=d, n_tiles=n_tiles),
        grid=(n // t, n_tiles + 1),
        in_specs=[pl.BlockSpec((t, d), row), pl.BlockSpec((t, d), row),
                  pl.BlockSpec((1, 1, mod.shape[-1]), lambda i, e: (i // nblk_seq, 0, 0)),
                  pl.BlockSpec(wq.shape, const), pl.BlockSpec(k1.shape, const), pl.BlockSpec(k2.shape, const),
                  pl.BlockSpec((eb, d), lambda i, e: (jnp.minimum(e, n_tiles - 1), 0)),
                  pl.BlockSpec((d, eb), lambda i, e: (0, jnp.maximum(e - 1, 0))),
                  pl.BlockSpec((1, d), const), pl.BlockSpec((1, d), const)],
        out_specs=pl.BlockSpec((t, d), row),
        out_shape=jax.ShapeDtypeStruct((n, d), _f32),
        scratch_shapes=[sel32, sel16, sel32, sel32,
                        pltpu.VMEM((eb, t), _f32), pltpu.VMEM((eb, t), _f32),
                        pltpu.VMEM((eb, t), _bf16), pltpu.VMEM((eb, t), _bf16), pltpu.VMEM((d, t), _f32)],
        compiler_params=pltpu.CompilerParams(dimension_semantics=("parallel", "arbitrary"),
                                             vmem_limit_bytes=VMEM_LIMIT),
        name="peer",
    )(h2, x1, mod, wq, k1, k2, u, vt, g, b)


def _rope_tables(seq, head_dim):
    m = head_dim // 4
    lane = np.arange(LANES)
    d = lane % head_dim
    use_col = (d // (2 * m)) == 1
    within = d % (2 * m)
    second = within // m
    freqs = ROPE_BASE ** (-np.arange(m, dtype=np.float64) / m)
    f = freqs[within % m]
    tpos = np.arange(seq)
    pos = np.where(use_col[None, :], (tpos % GRID_W)[:, None], (tpos // GRID_W)[:, None]).astype(np.float64)
    ang = pos * f[None, :]
    sign = np.where(second == 0, -1.0, 1.0)
    return jnp.asarray(np.cos(ang), _f32), jnp.asarray(np.sin(ang) * sign[None, :], _f32)


def _layer_weights(l, w_in, attn_q_norm, attn_k_norm, mla_q_norm, mla_kv_norm, w_uq, w_ukv, gmlp_ws, gmlp_b, w_o):
    offs = np.cumsum([0, QA_W, KA_W, KA_W, M_Q_RANK, M_KV_RANK, M_ROPE, CW, CW])
    qa_perm = np.concatenate([np.arange(HEAD_DIM) + HEAD_DIM * (j + 4 * half)
                              for j in range(A_HEADS // 2) for half in range(2)])
    cols = np.concatenate([qa_perm, np.arange(offs[1], offs[5]),
                           np.tile(np.arange(offs[5], offs[6]), M_HEADS), np.arange(offs[6], offs[8])])
    qd = M_NOPE + M_ROPE
    uq_cols = np.concatenate([np.arange(M_NOPE) + h * qd for h in range(M_HEADS)]
                             + [np.arange(M_ROPE) + h * qd + M_NOPE for h in range(M_HEADS)])
    kd = M_NOPE + M_V
    ukv_cols = np.concatenate([np.arange(M_NOPE) + h * kd for h in range(M_HEADS)]
                              + [np.arange(M_V) + h * kd + M_NOPE for h in range(M_HEADS)])
    wo_rows = np.concatenate([qa_perm, np.arange(QA_W, w_o.shape[1])])
    seg = np.arange(QA_W) // HEAD_DIM
    return {
        'win': w_in[l][:, cols].astype(_bf16),
        'aqn': jnp.tile(attn_q_norm[l], A_HEADS)[None, :],
        'akn': jnp.tile(attn_k_norm[l], A_KV)[None, :],
        'mqn': mla_q_norm[l][None, :],
        'mkvn': mla_kv_norm[l][None, :],
        'wuq': w_uq[l][:, uq_cols].astype(_bf16),
        'wukv': w_ukv[l][:, ukv_cols].astype(_bf16),
        'ws': gmlp_ws[l].astype(_bf16),
        'bsf': jnp.repeat(gmlp_b[l].T, C_DIM, axis=1),
        'bd': jnp.asarray(seg[:, None] == seg[None, :], _bf16),
        'wo': w_o[l][wo_rows, :].astype(_bf16),
    }


def kernel(x_prompt, x_sample, cache_attn_k, cache_attn_v, cache_mla_ckv, cache_mla_krope, c, c_ctx, w_mod, b_mod, w_in, attn_q_norm, attn_k_norm, mla_q_norm, mla_kv_norm, w_uq, w_ukv, gmlp_ws, gmlp_b, w_o, ln1_g, ln1_b, ln2_g, ln2_b, peer_wq, peer_k1, peer_k2, peer_u, peer_v):
    batch, seq, d = x_prompt.shape
    dec_batch, dec_seq, _ = x_sample.shape
    depth = w_in.shape[0]
    past = cache_attn_k.shape[2]
    alpha = (2.0 * depth) ** 0.25

    n_rows = 8 * ((1 + dec_batch + 7) // 8)
    c_all = jnp.zeros((n_rows, d), _f32).at[0].set(c_ctx).at[1:1 + dec_batch].set(c)
    mod_all = _mod_call(c_all, w_mod, b_mod)

    rope = _rope_tables(dec_seq, HEAD_DIM) + _rope_tables(dec_seq, M_ROPE)
    xp = x_prompt.reshape(batch * seq, d)
    xs = x_sample.reshape(dec_batch * dec_seq, d)
    new = []
    for l in range(depth):
        wts = _layer_weights(l, w_in, attn_q_norm, attn_k_norm, mla_q_norm, mla_kv_norm, w_uq, w_ukv,
                             gmlp_ws, gmlp_b, w_o)
        peer_w = (peer_wq[l].astype(_bf16), peer_k1[l].astype(_bf16), peer_k2[l].astype(_bf16),
                  peer_u[l].astype(_bf16), peer_v[l].T.astype(_bf16))
        ln1 = (ln1_g[l][None, :], ln1_b[l][None, :])
        ln2 = (ln2_g[l][None, :], ln2_b[l][None, :])
        mod_ctx = mod_all[l, 0:1][:, None, :]
        mod_lat = mod_all[l, 1:1 + dec_batch][:, None, :]
        cache = (cache_attn_k[:, l].reshape(dec_batch, past, KA_W),
                 cache_attn_v[:, l].reshape(dec_batch, past, KA_W),
                 cache_mla_ckv[:, l],
                 jnp.tile(cache_mla_krope[:, l], (1, 1, M_HEADS)))

        pre = _pre_call(xp, mod_ctx, batch * seq, wts, None, latent=False)
        new.append(pre[7:])
        oa, om = _attn_call(pre[0:6], seq, None, None, latent=False)
        x1, h2 = _post_call(oa, om, pre[6], xp, mod_ctx, batch * seq, wts['wo'], *ln1, alpha)
        xp = _peer_call(h2, x1, mod_ctx, batch * seq, *peer_w, *ln2, alpha)

        pre = _pre_call(xs, mod_lat, dec_seq, wts, rope, latent=True)
        oa, om = _attn_call(pre[0:6], dec_seq, cache, wts['wukv'], latent=True)
        x1, h2 = _post_call(oa, om, pre[6], xs, mod_lat, dec_seq, wts['wo'], *ln1, alpha)
        xs = _peer_call(h2, x1, mod_lat, dec_seq, *peer_w, *ln2, alpha)

    def stack(i, tail):
        return jnp.stack([new[l][i].reshape((batch, seq) + tail) for l in range(depth)], axis=1)

    return (xp.reshape(batch, seq, d), xs.reshape(dec_batch, dec_seq, d),
            stack(0, (A_KV, HEAD_DIM)), stack(1, (A_KV, HEAD_DIM)), stack(2, (M_KV_RANK,)), stack(3, (M_ROPE,)))
```

```python
import functools
import math

import numpy as np
import jax
import jax.numpy as jnp
from jax import lax
from jax.experimental import pallas as pl
from jax.experimental.pallas import tpu as pltpu

A_HEADS = 8
A_KV = 2
HEAD_DIM = 64
M_HEADS = 4
M_Q_RANK = 256
M_KV_RANK = 128
M_NOPE = 64
M_ROPE = 32
M_V = 64
C_GROUPS = 4
C_DIM = 64
C_CHUNK = 128
PEER_HEADS = 8
N_KEYS = 128
PEER_HALF = 128
PEER_TOPK = 16
GRID_W = 64
EPS = 1e-6
ROPE_BASE = 10000.0

LANES = 128
ROWS_B = 16
NEG_BIG = -3.0e38

QA_W = A_HEADS * HEAD_DIM
KA_W = A_KV * HEAD_DIM
CQ_OFF = QA_W + 2 * KA_W
CKV_OFF = CQ_OFF + M_Q_RANK
KR_OFF = CKV_OFF + M_KV_RANK
UC_OFF = KR_OFF + M_HEADS * M_ROPE
CW = C_GROUPS * C_DIM
VC_OFF = UC_OFF + CW
IN_WP = VC_OFF + CW

PRE_TB = 1024
ATT_QB = 256
PEER_T = 512
PEER_EB = 1024
VMEM_LIMIT = 56 * 1024 * 1024

_bf16 = jnp.bfloat16
_f32 = jnp.float32


def _dot(a, b):
    return jnp.dot(a, b, preferred_element_type=_f32)


def _dot_nt(a, b):
    return lax.dot_general(a, b, (((1,), (1,)), ((), ())), preferred_element_type=_f32)


def _split(x):
    hi = x.astype(_bf16)
    lo = (x - hi.astype(_f32)).astype(_bf16)
    return hi, lo


def _segsum(x, ones_bd):
    hi, lo = _split(x)
    return _dot(hi, ones_bd) + _dot(lo, ones_bd)


def _ln_rows(x):
    mu = jnp.mean(x, axis=-1, keepdims=True)
    d = x - mu
    var = jnp.mean(d * d, axis=-1, keepdims=True)
    return d * lax.rsqrt(var + EPS)


def _rot(x, cos, sin, half):
    lane = lax.broadcasted_iota(jnp.int32, x.shape, 1)
    up = pltpu.roll(x, LANES - half, axis=1)
    dn = pltpu.roll(x, half, axis=1)
    partner = jnp.where((lane & half) == 0, up, dn)
    return x * cos + partner * sin


def _mod_kernel(c_ref, w_ref, b_ref, o_ref):
    c = c_ref[...]
    a = c * (1.0 / (1.0 + jnp.exp(-c)))
    a_hi, a_lo = _split(a)
    w_hi, w_lo = _split(w_ref[0])
    o_ref[0] = _dot(a_hi, w_hi) + _dot(a_lo, w_hi) + _dot(a_hi, w_lo) + b_ref[0]


def _mod_call(c_all, w_mod, b_mod):
    depth, d, d6 = w_mod.shape
    rows = c_all.shape[0]
    tn = 1024
    return pl.pallas_call(
        _mod_kernel,
        grid=(depth, d6 // tn),
        in_specs=[pl.BlockSpec((rows, d), lambda l, j: (0, 0)),
                  pl.BlockSpec((1, d, tn), lambda l, j: (l, 0, j)),
                  pl.BlockSpec((1, 1, tn), lambda l, j: (l, 0, j))],
        out_specs=pl.BlockSpec((1, rows, tn), lambda l, j: (l, 0, j)),
        out_shape=jax.ShapeDtypeStruct((depth, rows, d6), _f32),
        compiler_params=pltpu.CompilerParams(dimension_semantics=("parallel", "parallel"),
                                             vmem_limit_bytes=VMEM_LIMIT),
        name="mod",
    )(c_all, w_mod, b_mod.reshape(depth, 1, d6))


def _pre_kernel(*refs, latent, d_model):
    if latent:
        (x_ref, mod_ref, win_ref, aqn_ref, akn_ref, mqn_ref, mkvn_ref, wuq_ref, wukv_ref, ws_ref, bsf_ref,
         bd_ref, cosa_ref, sina_ref, cosm_ref, sinm_ref,
         qa_ref, ka_ref, va_ref, qm_ref, kvm_ref, kr_ref, oc_ref) = refs
    else:
        (x_ref, mod_ref, win_ref, aqn_ref, akn_ref, mqn_ref, mkvn_ref, wuq_ref, wukv_ref, ws_ref, bsf_ref,
         bd_ref,
         qa_ref, ka_ref, va_ref, qm_ref, kvm_ref, kr_ref, oc_ref,
         nk_ref, nv_ref, nckv_ref, nkr_ref) = refs
    D = d_model
    x = x_ref[...]
    mod = mod_ref[0]
    sh1 = mod[:, 0:D]
    sc1 = mod[:, D:2 * D]
    h = _ln_rows(x) * (1.0 + sc1) + sh1
    proj = _dot(h.astype(_bf16), win_ref[...])

    qa = proj[:, 0:QA_W]
    ms = _segsum(qa * qa, bd_ref[...]) * (1.0 / HEAD_DIM)
    qa = qa * lax.rsqrt(ms + EPS) * aqn_ref[...]
    ka = proj[:, QA_W:QA_W + KA_W]
    ms = _segsum(ka * ka, bd_ref[0:KA_W, 0:KA_W]) * (1.0 / HEAD_DIM)
    ka = ka * lax.rsqrt(ms + EPS) * akn_ref[...]
    va = proj[:, QA_W + KA_W:CQ_OFF]
    if not latent:
        nk_ref[...] = ka
        nv_ref[...] = va
    for j in range(QA_W // LANES):
        slab = qa[:, j * LANES:(j + 1) * LANES]
        if latent:
            slab = _rot(slab, cosa_ref[...], sina_ref[...], HEAD_DIM // 4)
        qa_ref[:, j * LANES:(j + 1) * LANES] = (slab * (1.0 / math.sqrt(HEAD_DIM))).astype(_bf16)
    if latent:
        ka = _rot(ka, cosa_ref[...], sina_ref[...], HEAD_DIM // 4)
    ka_ref[...] = ka.astype(_bf16)
    va_ref[...] = va.astype(_bf16)

    cq = proj[:, CQ_OFF:CKV_OFF]
    cq = cq * lax.rsqrt(jnp.mean(cq * cq, axis=-1, keepdims=True) + EPS) * mqn_ref[...]
    qm = _dot(cq.astype(_bf16), wuq_ref[...])
    nope_w = M_HEADS * M_NOPE
    qscale = 1.0 / math.sqrt(M_NOPE + M_ROPE)
    qm_ref[:, 0:nope_w] = (qm[:, 0:nope_w] * qscale).astype(_bf16)
    qr = qm[:, nope_w:]
    if latent:
        qr = _rot(qr, cosm_ref[...], sinm_ref[...], M_ROPE // 4)
    qm_ref[:, nope_w:] = (qr * qscale).astype(_bf16)
    ckv = proj[:, CKV_OFF:KR_OFF]
    ckv = ckv * lax.rsqrt(jnp.mean(ckv * ckv, axis=-1, keepdims=True) + EPS) * mkvn_ref[...]
    kvm_ref[...] = _dot(ckv.astype(_bf16), wukv_ref[...]).astype(_bf16)
    kr = proj[:, KR_OFF:UC_OFF]
    if not latent:
        nckv_ref[...] = ckv
        nkr_ref[...] = kr[:, 0:M_ROPE]
    else:
        kr = _rot(kr, cosm_ref[...], sinm_ref[...], M_ROPE // 4)
    kr_ref[...] = kr.astype(_bf16)

    uc = proj[:, UC_OFF:VC_OFF]
    vc = proj[:, VC_OFF:IN_WP]
    bd_c = bd_ref[0:CW, 0:CW]
    mu = _segsum(vc, bd_c) * (1.0 / C_DIM)
    dv = vc - mu
    var = _segsum(dv * dv, bd_c) * (1.0 / C_DIM)
    vg = (dv * lax.rsqrt(var + EPS)).astype(_bf16)
    lane_c = lax.broadcasted_iota(jnp.int32, (C_CHUNK, CW), 1)
    for r in range(x.shape[0] // C_CHUNK):
        rows = slice(r * C_CHUNK, (r + 1) * C_CHUNK)
        vgc = vg[rows]
        mixed = bsf_ref[...]
        for g in range(C_GROUPS):
            mg = _dot(ws_ref[g], vgc)
            mixed = mixed + jnp.where(lane_c // C_DIM == g, mg, 0.0)
        oc_ref[rows, :] = (uc[rows] * mixed).astype(_bf16)


def _pre_call(x2, mod, seq, wts, rope, latent):
    n, d = x2.shape
    tb = min(PRE_TB, seq)
    nblk_seq = seq // tb
    row = lambda i: (i, 0)
    const = lambda i: (0, 0)
    in_specs = [pl.BlockSpec((tb, d), row),
                pl.BlockSpec((1, 1, mod.shape[-1]), lambda i: (i // nblk_seq, 0, 0)),
                pl.BlockSpec(wts['win'].shape, const),
                pl.BlockSpec((1, QA_W), const), pl.BlockSpec((1, KA_W), const),
                pl.BlockSpec((1, M_Q_RANK), const), pl.BlockSpec((1, M_KV_RANK), const),
                pl.BlockSpec(wts['wuq'].shape, const), pl.BlockSpec(wts['wukv'].shape, const),
                pl.BlockSpec(wts['ws'].shape, lambda i: (0, 0, 0)),
                pl.BlockSpec((C_CHUNK, CW), const),
                pl.BlockSpec((QA_W, QA_W), const)]
    args = [x2, mod, wts['win'], wts['aqn'], wts['akn'], wts['mqn'], wts['mkvn'], wts['wuq'], wts['wukv'],
            wts['ws'], wts['bsf'], wts['bd']]
    if latent:
        pos = lambda i: (i % nblk_seq, 0)
        in_specs += [pl.BlockSpec((tb, LANES), pos)] * 4
        args += list(rope)
    widths = [QA_W, KA_W, KA_W, M_HEADS * (M_NOPE + M_ROPE), M_HEADS * (M_NOPE + M_V), M_HEADS * M_ROPE, CW]
    out_specs = [pl.BlockSpec((tb, w), row) for w in widths]
    out_shape = [jax.ShapeDtypeStruct((n, w), _bf16) for w in widths]
    if not latent:
        for w in (KA_W, KA_W, M_KV_RANK, M_ROPE):
            out_specs.append(pl.BlockSpec((tb, w), row))
            out_shape.append(jax.ShapeDtypeStruct((n, w), _f32))
    return pl.pallas_call(
        functools.partial(_pre_kernel, latent=latent, d_model=d),
        grid=(n // tb,),
        in_specs=in_specs, out_specs=out_specs, out_shape=out_shape,
        compiler_params=pltpu.CompilerParams(dimension_semantics=("parallel",), vmem_limit_bytes=VMEM_LIMIT),
        name="pre_lat" if latent else "pre_ctx",
    )(*args)


def _softmax_pv(q, keys, vals):
    s = [_dot_nt(q, k) for k in keys]
    m = s[0].max(axis=-1, keepdims=True)
    for si in s[1:]:
        m = jnp.maximum(m, si.max(axis=-1, keepdims=True))
    den = None
    o = None
    for si, v in zip(s, vals):
        e = jnp.exp(si - m)
        d = e.sum(axis=-1, keepdims=True)
        den = d if den is None else den + d
        pv = _dot(e.astype(_bf16), v)
        o = pv if o is None else o + pv
    return o * (1.0 / den)


def _attn_kernel(*refs, latent):
    if latent:
        (qa_ref, qm_ref, ka_ref, va_ref, kvm_ref, kr_ref, ck_ref, cv_ref, cckv_ref, ckr_ref, wukv_ref,
         oa_ref, om_ref) = refs
    else:
        qa_ref, qm_ref, ka_ref, va_ref, kvm_ref, kr_ref, oa_ref, om_ref = refs
    qb = qa_ref.shape[0]
    nope_w = M_HEADS * M_NOPE

    keys_a = [ka_ref[...]]
    vals_a = [va_ref[...]]
    if latent:
        keys_a = [ck_ref[0].astype(_bf16)] + keys_a
        vals_a = [cv_ref[0].astype(_bf16)] + vals_a
    lane = lax.broadcasted_iota(jnp.int32, (qb, LANES), 1)
    low = lane < HEAD_DIM
    for j in range(QA_W // LANES):
        slab = qa_ref[:, j * LANES:(j + 1) * LANES]
        q2 = jnp.concatenate([jnp.where(low, slab, jnp.zeros_like(slab)),
                              jnp.where(low, jnp.zeros_like(slab), slab)], axis=0)
        o2 = _softmax_pv(q2, keys_a, vals_a)
        oa_ref[:, j * LANES:(j + 1) * LANES] = jnp.where(low, o2[0:qb], o2[qb:]).astype(_bf16)

    kn = [kvm_ref[:, 0:nope_w]]
    vm = [kvm_ref[:, nope_w:]]
    krs = [kr_ref[...]]
    if latent:
        kvc = _dot(cckv_ref[0].astype(_bf16), wukv_ref[...]).astype(_bf16)
        kn = [kvc[:, 0:nope_w]] + kn
        vm = [kvc[:, nope_w:]] + vm
        krs = [ckr_ref[0].astype(_bf16)] + krs
    qr = qm_ref[:, nope_w:]
    head_r = lax.broadcasted_iota(jnp.int32, qr.shape, 1) // M_ROPE
    lane_o = lax.broadcasted_iota(jnp.int32, (qb, nope_w), 1)
    om = jnp.zeros((qb, nope_w), _f32)
    for p in range(M_HEADS // 2):
        pair = slice(p * LANES, (p + 1) * LANES)
        qn = qm_ref[:, pair]
        zero = jnp.zeros_like(qn)
        q2 = jnp.concatenate(
            [jnp.concatenate([jnp.where(low == (t == 0), qn, zero), jnp.where(head_r == 2 * p + t, qr, zero)], axis=1)
             for t in range(2)], axis=0)
        keys_m = [jnp.concatenate([a[:, pair], b], axis=1) for a, b in zip(kn, krs)]
        o2 = _softmax_pv(q2, keys_m, vm)
        for t in range(2):
            om = jnp.where(lane_o // M_V == 2 * p + t, o2[t * qb:(t + 1) * qb], om)
    om_ref[...] = om.astype(_bf16)


def _attn_call(pre, seq, cache, wukv, latent):
    qa, ka, va, qm, kvm, kr = pre
    n = qa.shape[0]
    nb = n // seq
    qb = min(ATT_QB, seq)
    nq = seq // qb
    qrow = lambda b, i: (b * nq + i, 0)
    krow = lambda b, i: (b, 0)
    in_specs = [pl.BlockSpec((qb, qa.shape[1]), qrow), pl.BlockSpec((qb, qm.shape[1]), qrow),
                pl.BlockSpec((seq, ka.shape[1]), krow), pl.BlockSpec((seq, va.shape[1]), krow),
                pl.BlockSpec((seq, kvm.shape[1]), krow), pl.BlockSpec((seq, kr.shape[1]), krow)]
    args = [qa, qm, ka, va, kvm, kr]
    if latent:
        for c in cache:
            in_specs.append(pl.BlockSpec((1,) + c.shape[1:], lambda b, i: (b, 0, 0)))
            args.append(c)
        in_specs.append(pl.BlockSpec(wukv.shape, lambda b, i: (0, 0)))
        args.append(wukv)
    return pl.pallas_call(
        functools.partial(_attn_kernel, latent=latent),
        grid=(nb, nq),
        in_specs=in_specs,
        out_specs=[pl.BlockSpec((qb, QA_W), qrow), pl.BlockSpec((qb, M_HEADS * M_V), qrow)],
        out_shape=[jax.ShapeDtypeStruct((n, QA_W), _bf16), jax.ShapeDtypeStruct((n, M_HEADS * M_V), _bf16)],
        compiler_params=pltpu.CompilerParams(dimension_semantics=("parallel", "parallel"),
                                             vmem_limit_bytes=VMEM_LIMIT),
        name="attn_lat" if latent else "attn_ctx",
    )(*args)


def _post_kernel(oa_ref, om_ref, oc_ref, x_ref, mod_ref, wo_ref, g_ref, b_ref, x1_ref, h2_ref, *, alpha, d_model):
    D = d_model
    a_w = oa_ref.shape[1]
    m_w = om_ref.shape[1]
    mix = (_dot(oa_ref[...], wo_ref[0:a_w, :]) + _dot(om_ref[...], wo_ref[a_w:a_w + m_w, :])
           + _dot(oc_ref[...], wo_ref[a_w + m_w:, :]))
    mod = mod_ref[0]
    g1 = mod[:, 2 * D:3 * D]
    sh2 = mod[:, 3 * D:4 * D]
    sc2 = mod[:, 4 * D:5 * D]
    x1 = _ln_rows(alpha * x_ref[...] + g1 * mix) * g_ref[...] + b_ref[...]
    x1_ref[...] = x1
    h2_ref[...] = (_ln_rows(x1) * (1.0 + sc2) + sh2).astype(_bf16)


def _post_call(oa, om, oc, x2, mod, seq, wo, g, b, alpha):
    n, d = x2.shape
    tb = min(PRE_TB, seq)
    nblk_seq = seq // tb
    row = lambda i: (i, 0)
    const = lambda i: (0, 0)
    return pl.pallas_call(
        functools.partial(_post_kernel, alpha=alpha, d_model=d),
        grid=(n // tb,),
        in_specs=[pl.BlockSpec((tb, oa.shape[1]), row), pl.BlockSpec((tb, om.shape[1]), row),
                  pl.BlockSpec((tb, oc.shape[1]), row), pl.BlockSpec((tb, d), row),
                  pl.BlockSpec((1, 1, mod.shape[-1]), lambda i: (i // nblk_seq, 0, 0)),
                  pl.BlockSpec(wo.shape, const), pl.BlockSpec((1, d), const), pl.BlockSpec((1, d), const)],
        out_specs=[pl.BlockSpec((tb, d), row), pl.BlockSpec((tb, d), row)],
        out_shape=[jax.ShapeDtypeStruct((n, d), _f32), jax.ShapeDtypeStruct((n, d), _bf16)],
        compiler_params=pltpu.CompilerParams(dimension_semantics=("parallel",), vmem_limit_bytes=VMEM_LIMIT),
        name="post",
    )(oa, om, oc, x2, mod, wo, g, b)


def _sorting_network(n):
    def merge(lo, hi, r):
        step = r * 2
        if step < hi - lo:
            yield from merge(lo, hi, step)
            yield from merge(lo + r, hi, step)
            yield from ((i, i + r) for i in range(lo + r, hi - r, step))
        else:
            yield (lo, lo + r)

    def sort(lo, hi):
        if hi - lo >= 1:
            mid = lo + (hi - lo) // 2
            yield from sort(lo, mid)
            yield from sort(mid + 1, hi)
            yield from merge(lo, hi, 1)

    return tuple(sort(0, n - 1))


SUBLANES = 8
_SORT16 = _sorting_network(N_KEYS // SUBLANES)


def _extract_top(s, n):
    wires = [s[SUBLANES * i:SUBLANES * (i + 1)] for i in range(s.shape[0] // SUBLANES)]
    wires += [None] * (N_KEYS // SUBLANES - len(wires))
    for a, b in _SORT16:
        x, y = wires[a], wires[b]
        if y is None:
            continue
        if x is None:
            wires[a], wires[b] = y, None
        else:
            wires[a], wires[b] = jnp.maximum(x, y), jnp.minimum(x, y)
    cols = [w for w in wires if w is not None]
    tops = []
    for j in range(n):
        m = jnp.max(cols[0], axis=0, keepdims=True)
        tops.append(m)
        if j + 1 < n:
            top = cols[0] == m
            keep = min(len(cols), n - 1 - j)
            cols = [jnp.where(top, cols[r + 1] if r + 1 < len(cols) else NEG_BIG, cols[r]) for r in range(keep)]
    return tops


def _stack_rows(rows):
    n = len(rows)
    rid = lax.broadcasted_iota(jnp.int32, (n, LANES), 0)
    out = jnp.broadcast_to(rows[0], (n, LANES))
    for j in range(1, n):
        out = jnp.where(rid == j, rows[j], out)
    return out


def _peer_select(s1, s2):
    k = PEER_TOPK
    v1 = _extract_top(s1, k + 1)
    v2 = _extract_top(s2, k + 1)
    rank2 = jnp.full(s2.shape, float(k), _f32)
    for j in reversed(range(k)):
        rank2 = jnp.where(s2 >= v2[j], float(j), rank2)
    col1 = _stack_rows(v1[0:k])
    col2 = _stack_rows(v2[0:k])
    pieces = [v1[0] + col2]
    for a in range(1, 8):
        pieces.append(v1[a] + col2[0:8])
    pieces.append(col1[8:16] + v2[0])
    tail = jnp.where(lax.broadcasted_iota(jnp.int32, (8, LANES), 0) == 0, v1[0] + v2[k],
                     jnp.where(lax.broadcasted_iota(jnp.int32, (8, LANES), 0) == 1, v1[k] + v2[0], NEG_BIG))
    pieces.append(tail)
    cand = jnp.concatenate(pieces, axis=0)
    tops = _extract_top(cand, k + 1)
    z = jnp.ones_like(tops[0])
    for j in range(1, k):
        z = z + jnp.exp(tops[j] - tops[0])
    tau = 0.5 * (tops[k - 1] + tops[k])
    cnt = jnp.zeros_like(s1)
    for b in range(8):
        cnt = jnp.where(s1 >= tau - v2[b], float(b + 1), cnt)
    extra = jnp.zeros_like(tau)
    for b in range(8, k):
        extra = extra + jnp.where(v1[0] >= tau - v2[b], 1.0, 0.0)
    cnt = cnt + jnp.where(s1 >= v1[0], extra, 0.0)
    e1 = jnp.exp(s1 - v1[0]) * (1.0 / z)
    e2 = jnp.exp(s2 - v2[0])
    return rank2, cnt, e1, e2


def _gelu2(x):
    c = math.sqrt(2.0 / math.pi)
    return x * (1.0 + jnp.tanh(x * (c + (c * 0.044715) * (x * x))))


def _peer_kernel(h2_ref, x1_ref, mod_ref, wq_ref, k1_ref, k2_ref, u_ref, vt_ref, g_ref, b_ref, y_ref,
                 s2_ref, re_ref, cnt_ref, e1_ref, a0_ref, a1_ref, p0_ref, p1_ref, acc_ref,
                 *, alpha, d_model, n_tiles):
    D = d_model
    e = pl.program_id(1)
    t = h2_ref.shape[0]
    nchunk = t // LANES
    half = t // 2
    slabs = u_ref.shape[1] // N_KEYS
    a_refs = (a0_ref, a1_ref)
    p_refs = (p0_ref, p1_ref)

    def gate_chunk(off, a_ref, p_ref):
        cols = pl.ds(off, LANES)
        rk_cols = pl.ds(2 * off, LANES)
        e2_cols = pl.ds(2 * off + LANES, LANES)
        for g in range(slabs // 8):
            rows8 = pl.ds(pl.multiple_of(e * slabs + g * 8, 8), 8)
            cnt8 = [cnt_ref[h, rows8, cols] for h in range(PEER_HEADS)]
            e18 = [e1_ref[h, rows8, cols] for h in range(PEER_HEADS)]
            for k in range(8):
                w = jnp.zeros((N_KEYS, LANES), _bf16)
                for h in range(PEER_HEADS):
                    hit = re_ref[h, :, rk_cols] < cnt8[h][k:k + 1].astype(_bf16)
                    w = w + jnp.where(hit, re_ref[h, :, e2_cols], jnp.zeros((), _bf16)) * e18[h][k:k + 1].astype(_bf16)
                rows = slice((g * 8 + k) * N_KEYS, (g * 8 + k + 1) * N_KEYS)
                p_ref[rows, cols] = _gelu2(a_ref[rows, cols]).astype(_bf16) * w

    def run(parity, front, back):
        cur, prev = parity, 1 - parity
        for hf in range(2):
            off = hf * half
            if front:
                a_refs[cur][:, pl.ds(off, half)] = _dot_nt(u_ref[0], h2_ref[pl.ds(off, half), :])
                for c in range(half // LANES):
                    gate_chunk(off + c * LANES, a_refs[cur], p_refs[cur])
            if back:
                acc_ref[:, pl.ds(off, half)] += _dot(vt_ref[0], p_refs[prev][:, pl.ds(off, half)])

    @pl.when(e == 0)
    def _select():
        q = _dot(h2_ref[...], wq_ref[...]).astype(_bf16)
        for h in range(PEER_HEADS):
            base = h * 2 * PEER_HALF
            rows = slice(h * N_KEYS, (h + 1) * N_KEYS)
            a1_ref[rows, :] = _dot_nt(k1_ref[...], q[:, base:base + PEER_HALF])
            s2_ref[h] = _dot_nt(k2_ref[...], q[:, base + PEER_HALF:base + 2 * PEER_HALF])

        def body(it, carry):
            c = pl.multiple_of((it % nchunk) * LANES, LANES)
            for j in range(2):
                h = 2 * (it // nchunk) + j
                r = pl.multiple_of(h * N_KEYS, N_KEYS)
                rank2, cnt, e1, e2 = _peer_select(a1_ref[pl.ds(r, N_KEYS), pl.ds(c, LANES)],
                                                  s2_ref[h, :, pl.ds(c, LANES)])
                re_ref[h, :, pl.ds(pl.multiple_of(2 * c, LANES), LANES)] = rank2.astype(_bf16)
                cnt_ref[h, :, pl.ds(c, LANES)] = cnt
                e1_ref[h, :, pl.ds(c, LANES)] = 0.5 * e1
                re_ref[h, :, pl.ds(pl.multiple_of(2 * c + LANES, LANES), LANES)] = e2.astype(_bf16)
            return carry

        lax.fori_loop(0, PEER_HEADS // 2 * nchunk, body, 0)
        acc_ref[...] = jnp.zeros_like(acc_ref)
        run(0, True, False)

    for parity in range(2):
        @pl.when((e >= 1) & (e < n_tiles) & (e % 2 == parity))
        def _steady():
            run(parity, True, True)

    @pl.when(e == n_tiles)
    def _finish():
        run(n_tiles % 2, False, True)
        mod = mod_ref[0]
        g2 = mod[:, 5 * D:6 * D]
        ff = acc_ref[...].T
        y_ref[...] = _ln_rows(alpha * x1_ref[...] + g2 * ff) * g_ref[...] + b_ref[...]


def _peer_call(h2, x1, mod, seq, wq, k1, k2, u, vt, layer, g, b, alpha):
    n, d = x1.shape
    t = PEER_T
    eb = PEER_EB
    n_exp = u.shape[1]
    nblk_seq = seq // t
    row = lambda i, e: (i, 0)
    const = lambda i, e: (0, 0)
    n_tiles = n_exp // eb
    assert eb == PEER_HEADS * N_KEYS and n_tiles >= 2
    sel32 = pltpu.VMEM((PEER_HEADS, N_KEYS, t), _f32)
    sel16 = pltpu.VMEM((PEER_HEADS, N_KEYS, 2 * t), _bf16)
    return pl.pallas_call(
        functools.partial(_peer_kernel, alpha=alpha, d_model=d, n_tiles=n_tiles),
        grid=(n // t, n_tiles + 1),
        in_specs=[pl.BlockSpec((t, d), row), pl.BlockSpec((t, d), row),
                  pl.BlockSpec((1, 1, mod.shape[-1]), lambda i, e: (i // nblk_seq, 0, 0)),
                  pl.BlockSpec(wq.shape, const), pl.BlockSpec(k1.shape, const), pl.BlockSpec(k2.shape, const),
                  pl.BlockSpec((1, eb, d), lambda i, e: (layer, jnp.minimum(e, n_tiles - 1), 0)),
                  pl.BlockSpec((1, d, eb), lambda i, e: (layer, 0, jnp.maximum(e - 1, 0))),
                  pl.BlockSpec((1, d), const), pl.BlockSpec((1, d), const)],
        out_specs=pl.BlockSpec((t, d), row),
        out_shape=jax.ShapeDtypeStruct((n, d), _f32),
        scratch_shapes=[sel32, sel16, sel32, sel32,
                        pltpu.VMEM((eb, t), _f32), pltpu.VMEM((eb, t), _f32),
                        pltpu.VMEM((eb, t), _bf16), pltpu.VMEM((eb, t), _bf16), pltpu.VMEM((d, t), _f32)],
        compiler_params=pltpu.CompilerParams(dimension_semantics=("parallel", "arbitrary"),
                                             vmem_limit_bytes=VMEM_LIMIT),
        name="peer",
    )(h2, x1, mod, wq, k1, k2, u, vt, g, b)


def _rope_tables(seq, head_dim):
    m = head_dim // 4
    lane = np.arange(LANES)
    d = lane % head_dim
    use_col = (d // (2 * m)) == 1
    within = d % (2 * m)
    second = within // m
    freqs = ROPE_BASE ** (-np.arange(m, dtype=np.float64) / m)
    f = freqs[within % m]
    tpos = np.arange(seq)
    pos = np.where(use_col[None, :], (tpos % GRID_W)[:, None], (tpos // GRID_W)[:, None]).astype(np.float64)
    ang = pos * f[None, :]
    sign = np.where(second == 0, -1.0, 1.0)
    return jnp.asarray(np.cos(ang), _f32), jnp.asarray(np.sin(ang) * sign[None, :], _f32)


def _layer_weights(l, w_in, attn_q_norm, attn_k_norm, mla_q_norm, mla_kv_norm, w_uq, w_ukv, gmlp_ws, gmlp_b, w_o):
    offs = np.cumsum([0, QA_W, KA_W, KA_W, M_Q_RANK, M_KV_RANK, M_ROPE, CW, CW])
    qa_perm = np.concatenate([np.arange(HEAD_DIM) + HEAD_DIM * (j + 4 * half)
                              for j in range(A_HEADS // 2) for half in range(2)])
    cols = np.concatenate([qa_perm, np.arange(offs[1], offs[5]),
                           np.tile(np.arange(offs[5], offs[6]), M_HEADS), np.arange(offs[6], offs[8])])
    qd = M_NOPE + M_ROPE
    uq_cols = np.concatenate([np.arange(M_NOPE) + h * qd for h in range(M_HEADS)]
                             + [np.arange(M_ROPE) + h * qd + M_NOPE for h in range(M_HEADS)])
    kd = M_NOPE + M_V
    ukv_cols = np.concatenate([np.arange(M_NOPE) + h * kd for h in range(M_HEADS)]
                              + [np.arange(M_V) + h * kd + M_NOPE for h in range(M_HEADS)])
    wo_rows = np.concatenate([qa_perm, np.arange(QA_W, w_o.shape[1])])
    seg = np.arange(QA_W) // HEAD_DIM
    return {
        'win': w_in[l][:, cols].astype(_bf16),
        'aqn': jnp.tile(attn_q_norm[l], A_HEADS)[None, :],
        'akn': jnp.tile(attn_k_norm[l], A_KV)[None, :],
        'mqn': mla_q_norm[l][None, :],
        'mkvn': mla_kv_norm[l][None, :],
        'wuq': w_uq[l][:, uq_cols].astype(_bf16),
        'wukv': w_ukv[l][:, ukv_cols].astype(_bf16),
        'ws': gmlp_ws[l].astype(_bf16),
        'bsf': jnp.repeat(gmlp_b[l].T, C_DIM, axis=1),
        'bd': jnp.asarray(seg[:, None] == seg[None, :], _bf16),
        'wo': w_o[l][wo_rows, :].astype(_bf16),
    }


def kernel(x_prompt, x_sample, cache_attn_k, cache_attn_v, cache_mla_ckv, cache_mla_krope, c, c_ctx, w_mod, b_mod, w_in, attn_q_norm, attn_k_norm, mla_q_norm, mla_kv_norm, w_uq, w_ukv, gmlp_ws, gmlp_b, w_o, ln1_g, ln1_b, ln2_g, ln2_b, peer_wq, peer_k1, peer_k2, peer_u, peer_v):
    batch, seq, d = x_prompt.shape
    dec_batch, dec_seq, _ = x_sample.shape
    depth = w_in.shape[0]
    past = cache_attn_k.shape[2]
    alpha = (2.0 * depth) ** 0.25

    n_rows = 8 * ((1 + dec_batch + 7) // 8)
    c_all = jnp.zeros((n_rows, d), _f32).at[0].set(c_ctx).at[1:1 + dec_batch].set(c)
    mod_all = _mod_call(c_all, w_mod, b_mod)

    rope = _rope_tables(dec_seq, HEAD_DIM) + _rope_tables(dec_seq, M_ROPE)
    xp = x_prompt.reshape(batch * seq, d)
    xs = x_sample.reshape(dec_batch * dec_seq, d)
    u_all = peer_u.astype(_bf16)
    vt_all = jnp.swapaxes(peer_v, 1, 2).astype(_bf16)
    new = []
    for l in range(depth):
        wts = _layer_weights(l, w_in, attn_q_norm, attn_k_norm, mla_q_norm, mla_kv_norm, w_uq, w_ukv,
                             gmlp_ws, gmlp_b, w_o)
        peer_w = (peer_wq[l].astype(_bf16), peer_k1[l].astype(_bf16), peer_k2[l].astype(_bf16), u_all, vt_all, l)
        ln1 = (ln1_g[l][None, :], ln1_b[l][None, :])
        ln2 = (ln2_g[l][None, :], ln2_b[l][None, :])
        mod_ctx = mod_all[l, 0:1][:, None, :]
        mod_lat = mod_all[l, 1:1 + dec_batch][:, None, :]
        cache = (cache_attn_k[:, l].reshape(dec_batch, past, KA_W),
                 cache_attn_v[:, l].reshape(dec_batch, past, KA_W),
                 cache_mla_ckv[:, l],
                 jnp.tile(cache_mla_krope[:, l], (1, 1, M_HEADS)))

        pre = _pre_call(xp, mod_ctx, batch * seq, wts, None, latent=False)
        new.append(pre[7:])
        oa, om = _attn_call(pre[0:6], seq, None, None, latent=False)
        x1, h2 = _post_call(oa, om, pre[6], xp, mod_ctx, batch * seq, wts['wo'], *ln1, alpha)
        xp = _peer_call(h2, x1, mod_ctx, batch * seq, *peer_w, *ln2, alpha)

        pre = _pre_call(xs, mod_lat, dec_seq, wts, rope, latent=True)
        oa, om = _attn_call(pre[0:6], dec_seq, cache, wts['wukv'], latent=True)
        x1, h2 = _post_call(oa, om, pre[6], xs, mod_lat, dec_seq, wts['wo'], *ln1, alpha)
        xs = _peer_call(h2, x1, mod_lat, dec_seq, *peer_w, *ln2, alpha)

    def stack(i, tail):
        return jnp.stack([new[l][i].reshape((batch, seq) + tail) for l in range(depth)], axis=1)

    return (xp.reshape(batch, seq, d), xs.reshape(dec_batch, dec_seq, d),
            stack(0, (A_KV, HEAD_DIM)), stack(1, (A_KV, HEAD_DIM)), stack(2, (M_KV_RANK,)), stack(3, (M_ROPE,)))
```

```python
import functools
import math

import numpy as np
import jax
import jax.numpy as jnp
from jax import lax
from jax.experimental import pallas as pl
from jax.experimental.pallas import tpu as pltpu

A_HEADS = 8
A_KV = 2
HEAD_DIM = 64
M_HEADS = 4
M_Q_RANK = 256
M_KV_RANK = 128
M_NOPE = 64
M_ROPE = 32
M_V = 64
C_GROUPS = 4
C_DIM = 64
C_CHUNK = 128
PEER_HEADS = 8
N_KEYS = 128
PEER_HALF = 128
PEER_TOPK = 16
GRID_W = 64
EPS = 1e-6
ROPE_BASE = 10000.0

LANES = 128
SUBLANES = 8
NEG_BIG = -3.0e38

QA_W = A_HEADS * HEAD_DIM
KA_W = A_KV * HEAD_DIM
CQ_OFF = QA_W + 2 * KA_W
CKV_OFF = CQ_OFF + M_Q_RANK
KR_OFF = CKV_OFF + M_KV_RANK
UC_OFF = KR_OFF + M_HEADS * M_ROPE
CW = C_GROUPS * C_DIM
VC_OFF = UC_OFF + CW
IN_WP = VC_OFF + CW

PRE_TB = 1024
ATT_QB = 256
PEER_T = 512
POST_TB = 512
PEER_EB = 1024
SEL_HEADS = 2
VMEM_LIMIT = 56 * 1024 * 1024

_bf16 = jnp.bfloat16
_f32 = jnp.float32


def _dot(a, b):
    return jnp.dot(a, b, preferred_element_type=_f32)


def _dot_nt(a, b):
    return lax.dot_general(a, b, (((1,), (1,)), ((), ())), preferred_element_type=_f32)


def _split(x):
    hi = x.astype(_bf16)
    lo = (x - hi.astype(_f32)).astype(_bf16)
    return hi, lo


def _segsum(x, ones_bd):
    hi, lo = _split(x)
    return _dot(hi, ones_bd) + _dot(lo, ones_bd)


def _ln_rows(x):
    mu = jnp.mean(x, axis=-1, keepdims=True)
    d = x - mu
    var = jnp.mean(d * d, axis=-1, keepdims=True)
    return d * lax.rsqrt(var + EPS)


def _rot(x, cos, sin, half):
    lane = lax.broadcasted_iota(jnp.int32, x.shape, 1)
    up = pltpu.roll(x, LANES - half, axis=1)
    dn = pltpu.roll(x, half, axis=1)
    partner = jnp.where((lane & half) == 0, up, dn)
    return x * cos + partner * sin


def _mod_kernel(c_ref, w_ref, b_ref, o_ref):
    c = c_ref[...]
    a = c * (1.0 / (1.0 + jnp.exp(-c)))
    a_hi, a_lo = _split(a)
    w_hi, w_lo = _split(w_ref[0])
    o_ref[0] = _dot(a_hi, w_hi) + _dot(a_lo, w_hi) + _dot(a_hi, w_lo) + b_ref[0]


def _mod_call(c_all, w_mod, b_mod):
    depth, d, d6 = w_mod.shape
    rows = c_all.shape[0]
    tn = 1024
    return pl.pallas_call(
        _mod_kernel,
        grid=(depth, d6 // tn),
        in_specs=[pl.BlockSpec((rows, d), lambda l, j: (0, 0)),
                  pl.BlockSpec((1, d, tn), lambda l, j: (l, 0, j)),
                  pl.BlockSpec((1, 1, tn), lambda l, j: (l, 0, j))],
        out_specs=pl.BlockSpec((1, rows, tn), lambda l, j: (l, 0, j)),
        out_shape=jax.ShapeDtypeStruct((depth, rows, d6), _f32),
        compiler_params=pltpu.CompilerParams(dimension_semantics=("parallel", "parallel"),
                                             vmem_limit_bytes=VMEM_LIMIT),
        name="mod",
    )(c_all, w_mod, b_mod.reshape(depth, 1, d6))


def _pre_kernel(*refs, latent, d_model):
    if latent:
        (x_ref, mod_ref, win_ref, aqn_ref, akn_ref, mqn_ref, mkvn_ref, wuq_ref, wukv_ref, ws_ref, bsf_ref,
         bd_ref, cosa_ref, sina_ref, cosm_ref, sinm_ref,
         qa_ref, ka_ref, va_ref, qm_ref, kvm_ref, kr_ref, oc_ref) = refs
    else:
        (x_ref, mod_ref, win_ref, aqn_ref, akn_ref, mqn_ref, mkvn_ref, wuq_ref, wukv_ref, ws_ref, bsf_ref,
         bd_ref,
         qa_ref, ka_ref, va_ref, qm_ref, kvm_ref, kr_ref, oc_ref,
         nk_ref, nv_ref, nckv_ref, nkr_ref) = refs
    D = d_model
    x = x_ref[...]
    mod = mod_ref[0]
    sh1 = mod[:, 0:D]
    sc1 = mod[:, D:2 * D]
    h = _ln_rows(x) * (1.0 + sc1) + sh1
    proj = _dot(h.astype(_bf16), win_ref[...])

    qa = proj[:, 0:QA_W]
    ms = _segsum(qa * qa, bd_ref[...]) * (1.0 / HEAD_DIM)
    qa = qa * lax.rsqrt(ms + EPS) * aqn_ref[...]
    ka = proj[:, QA_W:QA_W + KA_W]
    ms = _segsum(ka * ka, bd_ref[0:KA_W, 0:KA_W]) * (1.0 / HEAD_DIM)
    ka = ka * lax.rsqrt(ms + EPS) * akn_ref[...]
    va = proj[:, QA_W + KA_W:CQ_OFF]
    if not latent:
        nk_ref[...] = ka
        nv_ref[...] = va
    for j in range(QA_W // LANES):
        slab = qa[:, j * LANES:(j + 1) * LANES]
        if latent:
            slab = _rot(slab, cosa_ref[...], sina_ref[...], HEAD_DIM // 4)
        qa_ref[:, j * LANES:(j + 1) * LANES] = (slab * (1.0 / math.sqrt(HEAD_DIM))).astype(_bf16)
    if latent:
        ka = _rot(ka, cosa_ref[...], sina_ref[...], HEAD_DIM // 4)
    ka_ref[...] = ka.astype(_bf16)
    va_ref[...] = va.astype(_bf16)

    cq = proj[:, CQ_OFF:CKV_OFF]
    cq = cq * lax.rsqrt(jnp.mean(cq * cq, axis=-1, keepdims=True) + EPS) * mqn_ref[...]
    qm = _dot(cq.astype(_bf16), wuq_ref[...])
    nope_w = M_HEADS * M_NOPE
    qscale = 1.0 / math.sqrt(M_NOPE + M_ROPE)
    qm_ref[:, 0:nope_w] = (qm[:, 0:nope_w] * qscale).astype(_bf16)
    qr = qm[:, nope_w:]
    if latent:
        qr = _rot(qr, cosm_ref[...], sinm_ref[...], M_ROPE // 4)
    qm_ref[:, nope_w:] = (qr * qscale).astype(_bf16)
    ckv = proj[:, CKV_OFF:KR_OFF]
    ckv = ckv * lax.rsqrt(jnp.mean(ckv * ckv, axis=-1, keepdims=True) + EPS) * mkvn_ref[...]
    kvm_ref[...] = _dot(ckv.astype(_bf16), wukv_ref[...]).astype(_bf16)
    kr = proj[:, KR_OFF:UC_OFF]
    if not latent:
        nckv_ref[...] = ckv
        nkr_ref[...] = kr[:, 0:M_ROPE]
    else:
        kr = _rot(kr, cosm_ref[...], sinm_ref[...], M_ROPE // 4)
    kr_ref[...] = kr.astype(_bf16)

    uc = proj[:, UC_OFF:VC_OFF]
    vc = proj[:, VC_OFF:IN_WP]
    bd_c = bd_ref[0:CW, 0:CW]
    mu = _segsum(vc, bd_c) * (1.0 / C_DIM)
    dv = vc - mu
    var = _segsum(dv * dv, bd_c) * (1.0 / C_DIM)
    vg = (dv * lax.rsqrt(var + EPS)).astype(_bf16)
    lane_c = lax.broadcasted_iota(jnp.int32, (C_CHUNK, CW), 1)
    for r in range(x.shape[0] // C_CHUNK):
        rows = slice(r * C_CHUNK, (r + 1) * C_CHUNK)
        vgc = vg[rows]
        mixed = bsf_ref[...]
        for g in range(C_GROUPS):
            mg = _dot(ws_ref[g], vgc)
            mixed = mixed + jnp.where(lane_c // C_DIM == g, mg, 0.0)
        oc_ref[rows, :] = (uc[rows] * mixed).astype(_bf16)


def _pre_call(x2, mod, seq, wts, rope, latent):
    n, d = x2.shape
    tb = min(PRE_TB, seq)
    nblk_seq = seq // tb
    row = lambda i: (i, 0)
    const = lambda i: (0, 0)
    in_specs = [pl.BlockSpec((tb, d), row),
                pl.BlockSpec((1, 1, mod.shape[-1]), lambda i: (i // nblk_seq, 0, 0)),
                pl.BlockSpec(wts['win'].shape, const),
                pl.BlockSpec((1, QA_W), const), pl.BlockSpec((1, KA_W), const),
                pl.BlockSpec((1, M_Q_RANK), const), pl.BlockSpec((1, M_KV_RANK), const),
                pl.BlockSpec(wts['wuq'].shape, const), pl.BlockSpec(wts['wukv'].shape, const),
                pl.BlockSpec(wts['ws'].shape, lambda i: (0, 0, 0)),
                pl.BlockSpec((C_CHUNK, CW), const),
                pl.BlockSpec((QA_W, QA_W), const)]
    args = [x2, mod, wts['win'], wts['aqn'], wts['akn'], wts['mqn'], wts['mkvn'], wts['wuq'], wts['wukv'],
            wts['ws'], wts['bsf'], wts['bd']]
    if latent:
        pos = lambda i: (i % nblk_seq, 0)
        in_specs += [pl.BlockSpec((tb, LANES), pos)] * 4
        args += list(rope)
    widths = [QA_W, KA_W, KA_W, M_HEADS * (M_NOPE + M_ROPE), M_HEADS * (M_NOPE + M_V), M_HEADS * M_ROPE, CW]
    out_specs = [pl.BlockSpec((tb, w), row) for w in widths]
    out_shape = [jax.ShapeDtypeStruct((n, w), _bf16) for w in widths]
    if not latent:
        for w in (KA_W, KA_W, M_KV_RANK, M_ROPE):
            out_specs.append(pl.BlockSpec((tb, w), row))
            out_shape.append(jax.ShapeDtypeStruct((n, w), _f32))
    return pl.pallas_call(
        functools.partial(_pre_kernel, latent=latent, d_model=d),
        grid=(n // tb,),
        in_specs=in_specs, out_specs=out_specs, out_shape=out_shape,
        compiler_params=pltpu.CompilerParams(dimension_semantics=("parallel",), vmem_limit_bytes=VMEM_LIMIT),
        name="pre_lat" if latent else "pre_ctx",
    )(*args)


def _softmax_pv(q, keys, vals):
    s = [_dot_nt(q, k) for k in keys]
    m = s[0].max(axis=-1, keepdims=True)
    for si in s[1:]:
        m = jnp.maximum(m, si.max(axis=-1, keepdims=True))
    den = None
    o = None
    for si, v in zip(s, vals):
        e = jnp.exp(si - m)
        d = e.sum(axis=-1, keepdims=True)
        den = d if den is None else den + d
        pv = _dot(e.astype(_bf16), v)
        o = pv if o is None else o + pv
    return o * (1.0 / den)


def _attn_kernel(*refs, latent):
    if latent:
        (qa_ref, qm_ref, ka_ref, va_ref, kvm_ref, kr_ref, ck_ref, cv_ref, cckv_ref, ckr_ref, wukv_ref,
         oa_ref, om_ref) = refs
    else:
        qa_ref, qm_ref, ka_ref, va_ref, kvm_ref, kr_ref, oa_ref, om_ref = refs
    qb = qa_ref.shape[0]
    nope_w = M_HEADS * M_NOPE

    keys_a = [ka_ref[...]]
    vals_a = [va_ref[...]]
    if latent:
        keys_a = [ck_ref[0].astype(_bf16)] + keys_a
        vals_a = [cv_ref[0].astype(_bf16)] + vals_a
    lane = lax.broadcasted_iota(jnp.int32, (qb, LANES), 1)
    low = lane < HEAD_DIM
    for j in range(QA_W // LANES):
        slab = qa_ref[:, j * LANES:(j + 1) * LANES]
        q2 = jnp.concatenate([jnp.where(low, slab, jnp.zeros_like(slab)),
                              jnp.where(low, jnp.zeros_like(slab), slab)], axis=0)
        o2 = _softmax_pv(q2, keys_a, vals_a)
        oa_ref[:, j * LANES:(j + 1) * LANES] = jnp.where(low, o2[0:qb], o2[qb:]).astype(_bf16)

    kn = [kvm_ref[:, 0:nope_w]]
    vm = [kvm_ref[:, nope_w:]]
    krs = [kr_ref[...]]
    if latent:
        kvc = _dot(cckv_ref[0].astype(_bf16), wukv_ref[...]).astype(_bf16)
        kn = [kvc[:, 0:nope_w]] + kn
        vm = [kvc[:, nope_w:]] + vm
        krs = [ckr_ref[0].astype(_bf16)] + krs
    qr = qm_ref[:, nope_w:]
    head_r = lax.broadcasted_iota(jnp.int32, qr.shape, 1) // M_ROPE
    lane_o = lax.broadcasted_iota(jnp.int32, (qb, nope_w), 1)
    om = jnp.zeros((qb, nope_w), _f32)
    for p in range(M_HEADS // 2):
        pair = slice(p * LANES, (p + 1) * LANES)
        qn = qm_ref[:, pair]
        zero = jnp.zeros_like(qn)
        q2 = jnp.concatenate(
            [jnp.concatenate([jnp.where(low == (t == 0), qn, zero), jnp.where(head_r == 2 * p + t, qr, zero)], axis=1)
             for t in range(2)], axis=0)
        keys_m = [jnp.concatenate([a[:, pair], b], axis=1) for a, b in zip(kn, krs)]
        o2 = _softmax_pv(q2, keys_m, vm)
        for t in range(2):
            om = jnp.where(lane_o // M_V == 2 * p + t, o2[t * qb:(t + 1) * qb], om)
    om_ref[...] = om.astype(_bf16)


def _attn_call(pre, seq, cache, wukv, latent):
    qa, ka, va, qm, kvm, kr = pre
    n = qa.shape[0]
    nb = n // seq
    qb = min(ATT_QB, seq)
    nq = seq // qb
    qrow = lambda b, i: (b * nq + i, 0)
    krow = lambda b, i: (b, 0)
    in_specs = [pl.BlockSpec((qb, qa.shape[1]), qrow), pl.BlockSpec((qb, qm.shape[1]), qrow),
                pl.BlockSpec((seq, ka.shape[1]), krow), pl.BlockSpec((seq, va.shape[1]), krow),
                pl.BlockSpec((seq, kvm.shape[1]), krow), pl.BlockSpec((seq, kr.shape[1]), krow)]
    args = [qa, qm, ka, va, kvm, kr]
    if latent:
        for c in cache:
            in_specs.append(pl.BlockSpec((1,) + c.shape[1:], lambda b, i: (b, 0, 0)))
            args.append(c)
        in_specs.append(pl.BlockSpec(wukv.shape, lambda b, i: (0, 0)))
        args.append(wukv)
    return pl.pallas_call(
        functools.partial(_attn_kernel, latent=latent),
        grid=(nb, nq),
        in_specs=in_specs,
        out_specs=[pl.BlockSpec((qb, QA_W), qrow), pl.BlockSpec((qb, M_HEADS * M_V), qrow)],
        out_shape=[jax.ShapeDtypeStruct((n, QA_W), _bf16), jax.ShapeDtypeStruct((n, M_HEADS * M_V), _bf16)],
        compiler_params=pltpu.CompilerParams(dimension_semantics=("parallel", "parallel"),
                                             vmem_limit_bytes=VMEM_LIMIT),
        name="attn_lat" if latent else "attn_ctx",
    )(*args)


def _post_kernel(oa_ref, om_ref, oc_ref, x_ref, mod_ref, wo_ref, g_ref, b_ref, wq_ref, x1_ref, h2_ref, q_ref,
                 *, alpha, d_model):
    D = d_model
    a_w = oa_ref.shape[1]
    m_w = om_ref.shape[1]
    mix = (_dot(oa_ref[...], wo_ref[0:a_w, :]) + _dot(om_ref[...], wo_ref[a_w:a_w + m_w, :])
           + _dot(oc_ref[...], wo_ref[a_w + m_w:, :]))
    mod = mod_ref[0]
    g1 = mod[:, 2 * D:3 * D]
    sh2 = mod[:, 3 * D:4 * D]
    sc2 = mod[:, 4 * D:5 * D]
    x1 = _ln_rows(alpha * x_ref[...] + g1 * mix) * g_ref[...] + b_ref[...]
    x1_ref[...] = x1
    h2 = (_ln_rows(x1) * (1.0 + sc2) + sh2).astype(_bf16)
    h2_ref[...] = h2
    q_ref[...] = _dot(h2, wq_ref[...]).astype(_bf16)


def _post_call(oa, om, oc, x2, mod, seq, wo, g, b, wq, alpha):
    n, d = x2.shape
    tb = min(POST_TB, seq)
    nblk_seq = seq // tb
    row = lambda i: (i, 0)
    const = lambda i: (0, 0)
    return pl.pallas_call(
        functools.partial(_post_kernel, alpha=alpha, d_model=d),
        grid=(n // tb,),
        in_specs=[pl.BlockSpec((tb, oa.shape[1]), row), pl.BlockSpec((tb, om.shape[1]), row),
                  pl.BlockSpec((tb, oc.shape[1]), row), pl.BlockSpec((tb, d), row),
                  pl.BlockSpec((1, 1, mod.shape[-1]), lambda i: (i // nblk_seq, 0, 0)),
                  pl.BlockSpec(wo.shape, const), pl.BlockSpec((1, d), const), pl.BlockSpec((1, d), const),
                  pl.BlockSpec(wq.shape, const)],
        out_specs=[pl.BlockSpec((tb, d), row), pl.BlockSpec((tb, d), row), pl.BlockSpec((tb, wq.shape[1]), row)],
        out_shape=[jax.ShapeDtypeStruct((n, d), _f32), jax.ShapeDtypeStruct((n, d), _bf16),
                   jax.ShapeDtypeStruct((n, wq.shape[1]), _bf16)],
        compiler_params=pltpu.CompilerParams(dimension_semantics=("parallel",), vmem_limit_bytes=VMEM_LIMIT),
        name="post",
    )(oa, om, oc, x2, mod, wo, g, b, wq)


def _sorting_network(n):
    def merge(lo, hi, r):
        step = r * 2
        if step < hi - lo:
            yield from merge(lo, hi, step)
            yield from merge(lo + r, hi, step)
            yield from ((i, i + r) for i in range(lo + r, hi - r, step))
        else:
            yield (lo, lo + r)

    def sort(lo, hi):
        if hi - lo >= 1:
            mid = lo + (hi - lo) // 2
            yield from sort(lo, mid)
            yield from sort(mid + 1, hi)
            yield from merge(lo, hi, 1)

    return tuple(sort(0, n - 1))


_SORT16 = _sorting_network(N_KEYS // SUBLANES)


def _extract_top(s, n):
    wires = [s[SUBLANES * i:SUBLANES * (i + 1)] for i in range(s.shape[0] // SUBLANES)]
    wires += [None] * (N_KEYS // SUBLANES - len(wires))
    for a, b in _SORT16:
        x, y = wires[a], wires[b]
        if y is None:
            continue
        if x is None:
            wires[a], wires[b] = y, None
        else:
            wires[a], wires[b] = jnp.maximum(x, y), jnp.minimum(x, y)
    cols = [w for w in wires if w is not None]
    tops = []
    for j in range(n):
        m = jnp.max(cols[0], axis=0, keepdims=True)
        tops.append(m)
        if j + 1 < n:
            top = cols[0] == m
            keep = min(len(cols), n - 1 - j)
            cols = [jnp.where(top, cols[r + 1] if r + 1 < len(cols) else NEG_BIG, cols[r]) for r in range(keep)]
    return tops


def _stack_rows(rows):
    n = len(rows)
    rid = lax.broadcasted_iota(jnp.int32, (n, LANES), 0)
    out = jnp.broadcast_to(rows[0], (n, LANES))
    for j in range(1, n):
        out = jnp.where(rid == j, rows[j], out)
    return out


def _peer_select(s1, s2):
    k = PEER_TOPK
    v1 = _extract_top(s1, k + 1)
    v2 = _extract_top(s2, k + 1)
    rank2 = jnp.full(s2.shape, float(k), _f32)
    for j in reversed(range(k)):
        rank2 = jnp.where(s2 >= v2[j], float(j), rank2)
    kh = k // 2
    assert kh == SUBLANES
    col1 = _stack_rows(v1[0:k])
    col2 = _stack_rows(v2[0:k])
    pieces = [v1[0] + col2]
    for a in range(1, kh):
        pieces.append(v1[a] + col2[0:kh])
    pieces.append(col1[kh:k] + v2[0])
    row = lax.broadcasted_iota(jnp.int32, (SUBLANES, LANES), 0)
    tail = jnp.where(row == 0, v1[0] + v2[k], jnp.where(row == 1, v1[k] + v2[0], NEG_BIG))
    pieces.append(tail)
    cand = jnp.concatenate(pieces, axis=0)
    tops = _extract_top(cand, k + 1)
    z = jnp.ones_like(tops[0])
    for j in range(1, k):
        z = z + jnp.exp(tops[j] - tops[0])
    tau = 0.5 * (tops[k - 1] + tops[k])
    cnt = jnp.zeros_like(s1)
    for b in range(kh):
        cnt = jnp.where(s1 >= tau - v2[b], float(b + 1), cnt)
    extra = jnp.zeros_like(tau)
    for b in range(kh, k):
        extra = extra + jnp.where(v1[0] >= tau - v2[b], 1.0, 0.0)
    cnt = cnt + jnp.where(s1 >= v1[0], extra, 0.0)
    e1 = jnp.exp(s1 - v1[0]) * (1.0 / z)
    e2 = jnp.exp(s2 - v2[0])
    return rank2, cnt, e1, e2


def _gelu2(x):
    c = math.sqrt(2.0 / math.pi)
    return x * (1.0 + jnp.tanh(x * (c + (c * 0.044715) * (x * x))))


def _peer_kernel(h2_ref, x1_ref, mod_ref, q_ref, k1_ref, k2_ref, u_ref, vt_ref, g_ref, b_ref, y_ref,
                 s2_ref, re_ref, cnt_ref, e1_ref, a0_ref, a1_ref, p0_ref, p1_ref, acc_ref,
                 *, alpha, d_model, n_tiles):
    D = d_model
    e = pl.program_id(1)
    t = h2_ref.shape[0]
    nchunk = t // LANES
    half = t // 2
    slabs = u_ref.shape[1] // N_KEYS
    a_refs = (a0_ref, a1_ref)
    p_refs = (p0_ref, p1_ref)

    def gate_chunk(off, a_ref, p_ref):
        cols = pl.ds(off, LANES)
        rk_cols = pl.ds(2 * off, LANES)
        e2_cols = pl.ds(2 * off + LANES, LANES)
        for g in range(slabs // SUBLANES):
            rows8 = pl.ds(pl.multiple_of(e * slabs + g * SUBLANES, SUBLANES), SUBLANES)
            cnt8 = [cnt_ref[h, rows8, cols] for h in range(PEER_HEADS)]
            e18 = [e1_ref[h, rows8, cols] for h in range(PEER_HEADS)]
            for k in range(SUBLANES):
                w = jnp.zeros((N_KEYS, LANES), _bf16)
                for h in range(PEER_HEADS):
                    hit = re_ref[h, :, rk_cols] < cnt8[h][k:k + 1].astype(_bf16)
                    w = w + jnp.where(hit, re_ref[h, :, e2_cols], jnp.zeros((), _bf16)) * e18[h][k:k + 1].astype(_bf16)
                rows = slice((g * SUBLANES + k) * N_KEYS, (g * SUBLANES + k + 1) * N_KEYS)
                p_ref[rows, cols] = _gelu2(a_ref[rows, cols].astype(_bf16)) * w

    def run(parity, front, back):
        cur, prev = parity, 1 - parity
        for hf in range(2):
            off = hf * half
            if front:
                a_refs[cur][:, pl.ds(off, half)] = _dot_nt(u_ref[0], h2_ref[pl.ds(off, half), :])
                for c in range(half // LANES):
                    gate_chunk(off + c * LANES, a_refs[cur], p_refs[cur])
            if back:
                acc_ref[:, pl.ds(off, half)] += _dot(vt_ref[0], p_refs[prev][:, pl.ds(off, half)])

    @pl.when(e == 0)
    def _select():
        q = q_ref[...]
        for h in range(PEER_HEADS):
            base = h * 2 * PEER_HALF
            rows = slice(h * N_KEYS, (h + 1) * N_KEYS)
            a1_ref[rows, :] = _dot_nt(k1_ref[...], q[:, base:base + PEER_HALF])
            s2_ref[h] = _dot_nt(k2_ref[...], q[:, base + PEER_HALF:base + 2 * PEER_HALF])

        def body(it, carry):
            c = pl.multiple_of((it % nchunk) * LANES, LANES)
            for j in range(SEL_HEADS):
                h = SEL_HEADS * (it // nchunk) + j
                r = pl.multiple_of(h * N_KEYS, N_KEYS)
                rank2, cnt, e1, e2 = _peer_select(a1_ref[pl.ds(r, N_KEYS), pl.ds(c, LANES)],
                                                  s2_ref[h, :, pl.ds(c, LANES)])
                re_ref[h, :, pl.ds(pl.multiple_of(2 * c, LANES), LANES)] = rank2.astype(_bf16)
                cnt_ref[h, :, pl.ds(c, LANES)] = cnt
                e1_ref[h, :, pl.ds(c, LANES)] = 0.5 * e1
                re_ref[h, :, pl.ds(pl.multiple_of(2 * c + LANES, LANES), LANES)] = e2.astype(_bf16)
            return carry

        lax.fori_loop(0, PEER_HEADS // SEL_HEADS * nchunk, body, 0)
        acc_ref[...] = jnp.zeros_like(acc_ref)
        run(0, True, False)

    for parity in range(2):
        @pl.when((e >= 1) & (e < n_tiles) & (e % 2 == parity))
        def _steady():
            run(parity, True, True)

    @pl.when(e == n_tiles)
    def _finish():
        run(n_tiles % 2, False, True)
        mod = mod_ref[0]
        g2 = mod[:, 5 * D:6 * D]
        ff = acc_ref[...].T
        y_ref[...] = _ln_rows(alpha * x1_ref[...] + g2 * ff) * g_ref[...] + b_ref[...]


def _peer_call(h2, x1, mod, seq, q, k1, k2, u, vt, layer, g, b, alpha):
    n, d = x1.shape
    t = PEER_T
    eb = PEER_EB
    n_exp = u.shape[1]
    nblk_seq = seq // t
    row = lambda i, e: (i, 0)
    const = lambda i, e: (0, 0)
    n_tiles = n_exp // eb
    assert eb % (SUBLANES * N_KEYS) == 0 and eb >= PEER_HEADS * N_KEYS and n_tiles >= 2
    sel32 = pltpu.VMEM((PEER_HEADS, N_KEYS, t), _f32)
    sel16 = pltpu.VMEM((PEER_HEADS, N_KEYS, 2 * t), _bf16)
    return pl.pallas_call(
        functools.partial(_peer_kernel, alpha=alpha, d_model=d, n_tiles=n_tiles),
        grid=(n // t, n_tiles + 1),
        in_specs=[pl.BlockSpec((t, d), row), pl.BlockSpec((t, d), row),
                  pl.BlockSpec((1, 1, mod.shape[-1]), lambda i, e: (i // nblk_seq, 0, 0)),
                  pl.BlockSpec((t, q.shape[1]), row), pl.BlockSpec(k1.shape, const), pl.BlockSpec(k2.shape, const),
                  pl.BlockSpec((1, eb, d), lambda i, e: (layer, jnp.minimum(e, n_tiles - 1), 0)),
                  pl.BlockSpec((1, d, eb), lambda i, e: (layer, 0, jnp.maximum(e - 1, 0))),
                  pl.BlockSpec((1, d), const), pl.BlockSpec((1, d), const)],
        out_specs=pl.BlockSpec((t, d), row),
        out_shape=jax.ShapeDtypeStruct((n, d), _f32),
        scratch_shapes=[sel32, sel16, sel32, sel32,
                        pltpu.VMEM((eb, t), _f32), pltpu.VMEM((eb, t), _f32),
                        pltpu.VMEM((eb, t), _bf16), pltpu.VMEM((eb, t), _bf16), pltpu.VMEM((d, t), _f32)],
        compiler_params=pltpu.CompilerParams(dimension_semantics=("parallel", "arbitrary"),
                                             vmem_limit_bytes=VMEM_LIMIT),
        name="peer",
    )(h2, x1, mod, q, k1, k2, u, vt, g, b)


def _rope_tables(seq, head_dim):
    m = head_dim // 4
    lane = np.arange(LANES)
    d = lane % head_dim
    use_col = (d // (2 * m)) == 1
    within = d % (2 * m)
    second = within // m
    freqs = ROPE_BASE ** (-np.arange(m, dtype=np.float64) / m)
    f = freqs[within % m]
    tpos = np.arange(seq)
    pos = np.where(use_col[None, :], (tpos % GRID_W)[:, None], (tpos // GRID_W)[:, None]).astype(np.float64)
    ang = pos * f[None, :]
    sign = np.where(second == 0, -1.0, 1.0)
    return jnp.asarray(np.cos(ang), _f32), jnp.asarray(np.sin(ang) * sign[None, :], _f32)


def _layer_weights(l, w_in, attn_q_norm, attn_k_norm, mla_q_norm, mla_kv_norm, w_uq, w_ukv, gmlp_ws, gmlp_b, w_o):
    offs = np.cumsum([0, QA_W, KA_W, KA_W, M_Q_RANK, M_KV_RANK, M_ROPE, CW, CW])
    qa_perm = np.concatenate([np.arange(HEAD_DIM) + HEAD_DIM * (j + 4 * half)
                              for j in range(A_HEADS // 2) for half in range(2)])
    cols = np.concatenate([qa_perm, np.arange(offs[1], offs[5]),
                           np.tile(np.arange(offs[5], offs[6]), M_HEADS), np.arange(offs[6], offs[8])])
    qd = M_NOPE + M_ROPE
    uq_cols = np.concatenate([np.arange(M_NOPE) + h * qd for h in range(M_HEADS)]
                             + [np.arange(M_ROPE) + h * qd + M_NOPE for h in range(M_HEADS)])
    kd = M_NOPE + M_V
    ukv_cols = np.concatenate([np.arange(M_NOPE) + h * kd for h in range(M_HEADS)]
                              + [np.arange(M_V) + h * kd + M_NOPE for h in range(M_HEADS)])
    wo_rows = np.concatenate([qa_perm, np.arange(QA_W, w_o.shape[1])])
    seg = np.arange(QA_W) // HEAD_DIM
    return {
        'win': w_in[l][:, cols].astype(_bf16),
        'aqn': jnp.tile(attn_q_norm[l], A_HEADS)[None, :],
        'akn': jnp.tile(attn_k_norm[l], A_KV)[None, :],
        'mqn': mla_q_norm[l][None, :],
        'mkvn': mla_kv_norm[l][None, :],
        'wuq': w_uq[l][:, uq_cols].astype(_bf16),
        'wukv': w_ukv[l][:, ukv_cols].astype(_bf16),
        'ws': gmlp_ws[l].astype(_bf16),
        'bsf': jnp.repeat(gmlp_b[l].T, C_DIM, axis=1),
        'bd': jnp.asarray(seg[:, None] == seg[None, :], _bf16),
        'wo': w_o[l][wo_rows, :].astype(_bf16),
    }


def kernel(x_prompt, x_sample, cache_attn_k, cache_attn_v, cache_mla_ckv, cache_mla_krope, c, c_ctx, w_mod, b_mod, w_in, attn_q_norm, attn_k_norm, mla_q_norm, mla_kv_norm, w_uq, w_ukv, gmlp_ws, gmlp_b, w_o, ln1_g, ln1_b, ln2_g, ln2_b, peer_wq, peer_k1, peer_k2, peer_u, peer_v):
    batch, seq, d = x_prompt.shape
    dec_batch, dec_seq, _ = x_sample.shape
    depth = w_in.shape[0]
    past = cache_attn_k.shape[2]
    alpha = (2.0 * depth) ** 0.25

    n_rows = SUBLANES * pl.cdiv(1 + dec_batch, SUBLANES)
    c_all = jnp.zeros((n_rows, d), _f32).at[0].set(c_ctx).at[1:1 + dec_batch].set(c)
    mod_all = _mod_call(c_all, w_mod, b_mod)

    rope = _rope_tables(dec_seq, HEAD_DIM) + _rope_tables(dec_seq, M_ROPE)
    xp = x_prompt.reshape(batch * seq, d)
    xs = x_sample.reshape(dec_batch * dec_seq, d)
    u_all = peer_u.astype(_bf16)
    vt_all = jnp.swapaxes(peer_v, 1, 2).astype(_bf16)
    new = []
    for l in range(depth):
        wts = _layer_weights(l, w_in, attn_q_norm, attn_k_norm, mla_q_norm, mla_kv_norm, w_uq, w_ukv,
                             gmlp_ws, gmlp_b, w_o)
        wq = peer_wq[l].astype(_bf16)
        peer_w = (peer_k1[l].astype(_bf16), peer_k2[l].astype(_bf16), u_all, vt_all, l)
        ln1 = (ln1_g[l][None, :], ln1_b[l][None, :])
        ln2 = (ln2_g[l][None, :], ln2_b[l][None, :])
        mod_ctx = mod_all[l, 0:1][:, None, :]
        mod_lat = mod_all[l, 1:1 + dec_batch][:, None, :]
        cache = (cache_attn_k[:, l].reshape(dec_batch, past, KA_W),
                 cache_attn_v[:, l].reshape(dec_batch, past, KA_W),
                 cache_mla_ckv[:, l],
                 jnp.tile(cache_mla_krope[:, l], (1, 1, M_HEADS)))

        pre = _pre_call(xp, mod_ctx, batch * seq, wts, None, latent=False)
        new.append(pre[7:])
        oa, om = _attn_call(pre[0:6], seq, None, None, latent=False)
        x1, h2, q = _post_call(oa, om, pre[6], xp, mod_ctx, batch * seq, wts['wo'], *ln1, wq, alpha)
        xp = _peer_call(h2, x1, mod_ctx, batch * seq, q, *peer_w, *ln2, alpha)

        pre = _pre_call(xs, mod_lat, dec_seq, wts, rope, latent=True)
        oa, om = _attn_call(pre[0:6], dec_seq, cache, wts['wukv'], latent=True)
        x1, h2, q = _post_call(oa, om, pre[6], xs, mod_lat, dec_seq, wts['wo'], *ln1, wq, alpha)
        xs = _peer_call(h2, x1, mod_lat, dec_seq, q, *peer_w, *ln2, alpha)

    def stack(i, tail):
        return jnp.stack([new[l][i].reshape((batch, seq) + tail) for l in range(depth)], axis=1)

    return (xp.reshape(batch, seq, d), xs.reshape(dec_batch, dec_seq, d),
            stack(0, (A_KV, HEAD_DIM)), stack(1, (A_KV, HEAD_DIM)), stack(2, (M_KV_RANK,)), stack(3, (M_ROPE,)))
```
